```python
import jax
import jax.numpy as jnp
from jax import lax
import numpy as np

D_MODEL = 1024
BATCH = 4
SEQ = 8192
DEPTH = 1
DEC_BATCH = 32
DEC_SEQ = 64
PAST_LEN = 2048

CHUNK = 64
N_META = 16
D_MIX = D_MODEL
H_GDN = 4
GDN_DK = 128
GDN_DV = 128
CONV_W = 4
CONV_DIM = H_GDN * (2 * GDN_DK + GDN_DV)
H_MLA = 4
Q_LORA = 384
KV_LORA = 256
DN = 128
DR = 64
DV_MLA = 128
ROPE_BASE = 10000.0
SM_SCALE = (DN + DR) ** -0.5
QB = 128
D_FF = 2816
EPS = 1e-6
L2_EPS = 1e-6
PROJ_SIZES = (CONV_DIM, H_GDN * GDN_DV, H_GDN, H_GDN, Q_LORA, KV_LORA, DR)
N_PROJ = sum(PROJ_SIZES)

kernel_name = 'hymba_gdn_mla_macaron_stream_step'


def rmsnorm(x, g):
    xf = x.astype(jnp.float32)
    y = xf * lax.rsqrt(jnp.mean(xf * xf, axis=-1, keepdims=True) + EPS)
    return (y * g.astype(jnp.float32)).astype(x.dtype)


def l2norm(x):
    return x * lax.rsqrt(jnp.sum(x * x, axis=-1, keepdims=True) + L2_EPS)


def half_ffn(x, g, wg, wu, wd):
    h = rmsnorm(x, g)
    return x + 0.5 * ((jax.nn.silu(h @ wg) * (h @ wu)) @ wd)


def split_proj(p):
    offs = np.cumsum(PROJ_SIZES)[:-1].tolist()
    return jnp.split(p, offs, axis=-1)


def rope_tables(pos):
    inv = ROPE_BASE ** (-jnp.arange(0, DR, 2, dtype=jnp.float32) / DR)
    ang = pos.astype(jnp.float32)[:, None] * inv[None, :]
    return jnp.cos(ang), jnp.sin(ang)


def apply_rope(x, cos, sin):
    x1 = x[..., :DR // 2].astype(jnp.float32)
    x2 = x[..., DR // 2:].astype(jnp.float32)
    return jnp.concatenate([x1 * cos - x2 * sin, x2 * cos + x1 * sin], axis=-1).astype(x.dtype)


def causal_conv(xe, w):
    t = xe.shape[1] - (CONV_W - 1)
    return sum(xe[:, j:j + t] * w[j] for j in range(CONV_W))


def gdn_features(u, a, b, a_log, dt_bias):
    u = jax.nn.silu(u.astype(jnp.float32))
    q, k, v = jnp.split(u, [H_GDN * GDN_DK, 2 * H_GDN * GDN_DK], axis=-1)
    bsz, t = u.shape[:2]
    q = l2norm(q.reshape(bsz, t, H_GDN, GDN_DK)) * (GDN_DK ** -0.5)
    k = l2norm(k.reshape(bsz, t, H_GDN, GDN_DK))
    v = v.reshape(bsz, t, H_GDN, GDN_DV)
    g = -jnp.exp(a_log.astype(jnp.float32)) * jax.nn.softplus(a.astype(jnp.float32) + dt_bias.astype(jnp.float32))
    beta = jax.nn.sigmoid(b.astype(jnp.float32))
    tr = lambda z: jnp.swapaxes(z, 1, 2)
    return tr(q), tr(k), tr(v), tr(g), tr(beta)


def gdn_block(m0, q, k, v, g, beta):
    L = q.shape[2]
    causal = jnp.tril(jnp.ones((L, L), dtype=bool))
    strict = jnp.tril(jnp.ones((L, L), dtype=bool), -1)
    gc = jnp.cumsum(g, axis=-1)
    diff = gc[..., :, None] - gc[..., None, :]
    decay = jnp.where(causal, jnp.exp(jnp.where(causal, diff, 0.0)), 0.0)
    a = jnp.where(strict, beta[..., :, None] * jnp.einsum('bhtd,bhsd->bhts', k, k) * decay, 0.0)
    eg = jnp.exp(gc)[..., None]
    rhs = beta[..., None] * (v - eg * jnp.einsum('bhtd,bhde->bhte', k, m0))
    u = lax.linalg.triangular_solve(jnp.eye(L, dtype=a.dtype) + a, rhs, left_side=True, lower=True, unit_diagonal=True)
    o = eg * jnp.einsum('bhtd,bhde->bhte', q, m0) + jnp.einsum('bhts,bhse->bhte', jnp.einsum('bhtd,bhsd->bhts', q, k) * decay, u)
    g_last = gc[..., -1:]
    m = jnp.exp(g_last)[..., None] * m0 + jnp.einsum('bhsd,bhse->bhde', k * jnp.exp(g_last - gc)[..., None], u)
    return m, o


def gdn_prompt(q, k, v, g, beta):
    bsz = q.shape[0]
    m0 = jnp.zeros((bsz, H_GDN, GDN_DK, GDN_DV), jnp.float32)
    m, o_meta = gdn_block(m0, q[:, :, :N_META], k[:, :, :N_META], v[:, :, :N_META], g[:, :, :N_META], beta[:, :, :N_META])

    def to_chunks(z):
        z = z[:, :, N_META:]
        nc = z.shape[2] // CHUNK
        return jnp.moveaxis(z.reshape(z.shape[:2] + (nc, CHUNK) + z.shape[3:]), 2, 0)

    m, o_f = lax.scan(lambda mm, blk: gdn_block(mm, *blk), m, tuple(to_chunks(z) for z in (q, k, v, g, beta)))
    o_f = jnp.moveaxis(o_f, 0, 2)
    o_f = o_f.reshape(o_f.shape[:2] + (-1, GDN_DV))
    return jnp.concatenate([o_meta, o_f], axis=2), m


def gdn_output(o, z, w):
    o = jnp.swapaxes(o, 1, 2)
    bsz, t = o.shape[:2]
    n = rmsnorm(o, w)
    gate = jax.nn.silu(z.astype(jnp.float32)).reshape(bsz, t, H_GDN, GDN_DV)
    return (n * gate).reshape(bsz, t, H_GDN * GDN_DV).astype(z.dtype)


def mla_project(cq, ckv, kr, cos, sin, gq, gkv, w_uq):
    bsz, t = cq.shape[:2]
    q = (rmsnorm(cq, gq) @ w_uq).reshape(bsz, t, H_MLA, DN + DR)
    qn = q[..., :DN]
    qr = apply_rope(q[..., DN:], cos[:, None, :], sin[:, None, :])
    c = rmsnorm(ckv, gkv)
    kr = apply_rope(kr, cos, sin)
    return qn, qr, c, kr


def mla_expand(c, w_ukv):
    bsz, t = c.shape[:2]
    kv = (c @ w_ukv).reshape(bsz, t, H_MLA, DN + DV_MLA)
    return kv[..., :DN], kv[..., DN:]


def mla_attend(qn, qr, kn, kr, v, mask):
    s = (jnp.einsum('bqhd,bkhd->bhqk', qn, kn) + jnp.einsum('bqhr,bkr->bhqk', qr, kr)).astype(jnp.float32) * SM_SCALE
    if mask is not None:
        s = jnp.where(mask, s, -jnp.inf)
    p = jax.nn.softmax(s, axis=-1).astype(v.dtype)
    return jnp.einsum('bhqk,bkhd->bqhd', p, v)


def mla_prompt(qn, qr, c, kr, w_ukv, key_chunk):
    kn, v = mla_expand(c, w_ukv)
    bsz, t = c.shape[:2]
    s_len = t - N_META
    o_meta = mla_attend(qn[:, :N_META], qr[:, :N_META], kn[:, :N_META], kr[:, :N_META], v[:, :N_META], None)
    nb = s_len // QB

    def blocks(z):
        z = z[:, N_META:]
        return jnp.moveaxis(z.reshape((bsz, nb, QB) + z.shape[2:]), 1, 0)

    def one(args):
        qn_b, qr_b, j = args
        q_chunk = (j * QB + jnp.arange(QB)) // CHUNK
        mask = key_chunk[None, :] <= q_chunk[:, None]
        return mla_attend(qn_b, qr_b, kn, kr, v, mask)

    o_f = lax.map(one, (blocks(qn), blocks(qr), jnp.arange(nb)))
    o_f = jnp.moveaxis(o_f, 0, 1).reshape(bsz, s_len, H_MLA * DV_MLA)
    return jnp.concatenate([o_meta.reshape(bsz, N_META, H_MLA * DV_MLA), o_f], axis=1)


def setup_inputs(seed: int = 0) -> dict:
    key = jax.random.key(seed)
    ks = iter(jax.random.split(key, 40))
    f32 = jnp.float32
    nrm = lambda shape, scale: jax.random.normal(next(ks), shape, f32) * scale
    gain = lambda shape: 1.0 + 0.05 * jax.random.normal(next(ks), shape, f32)
    a_log = jnp.log(jax.random.uniform(next(ks), (DEPTH, H_GDN), f32, 1.0, 16.0))
    dt = jnp.exp(jax.random.uniform(next(ks), (DEPTH, H_GDN), f32, np.log(1e-3), np.log(1e-1)))
    dt_bias = dt + jnp.log(-jnp.expm1(-dt))
    return {
        'x_prompt': nrm((BATCH, SEQ, D_MODEL), 1.0),
        'x_sample': nrm((DEC_BATCH, DEC_SEQ, D_MODEL), 1.0),
        'cache_mla_ckv': nrm((DEPTH, DEC_BATCH, N_META + PAST_LEN, KV_LORA), 1.0),
        'cache_mla_krope': nrm((DEPTH, DEC_BATCH, N_META + PAST_LEN, DR), 1.0),
        'state_gdn': nrm((DEPTH, DEC_BATCH, H_GDN, GDN_DK, GDN_DV), GDN_DK ** -0.5),
        'state_conv': nrm((DEPTH, DEC_BATCH, CONV_W - 1, CONV_DIM), 1.0),
        'meta': nrm((N_META, D_MODEL), 1.0),
        'ffn1_norm': gain((DEPTH, D_MODEL)),
        'ffn1_wg': nrm((DEPTH, D_MODEL, D_FF), D_MODEL ** -0.5),
        'ffn1_wu': nrm((DEPTH, D_MODEL, D_FF), D_MODEL ** -0.5),
        'ffn1_wd': nrm((DEPTH, D_FF, D_MODEL), D_FF ** -0.5),
        'mix_norm': gain((DEPTH, D_MODEL)),
        'w_in': nrm((DEPTH, D_MODEL, N_PROJ), D_MODEL ** -0.5),
        'conv_w': nrm((DEPTH, CONV_W, CONV_DIM), CONV_W ** -0.5),
        'a_log': a_log,
        'dt_bias': dt_bias,
        'gdn_norm': gain((DEPTH, GDN_DV)),
        'q_norm': gain((DEPTH, Q_LORA)),
        'kv_norm': gain((DEPTH, KV_LORA)),
        'w_uq': nrm((DEPTH, Q_LORA, H_MLA * (DN + DR)), Q_LORA ** -0.5),
        'w_ukv': nrm((DEPTH, KV_LORA, H_MLA * (DN + DV_MLA)), KV_LORA ** -0.5),
        'w_out': nrm((DEPTH, D_MIX, D_MODEL), D_MIX ** -0.5),
        'ffn2_norm': gain((DEPTH, D_MODEL)),
        'ffn2_wg': nrm((DEPTH, D_MODEL, D_FF), D_MODEL ** -0.5),
        'ffn2_wu': nrm((DEPTH, D_MODEL, D_FF), D_MODEL ** -0.5),
        'ffn2_wd': nrm((DEPTH, D_FF, D_MODEL), D_FF ** -0.5),
        'final_norm': gain((D_MODEL,)),
    }


def reference(x_prompt, x_sample, cache_mla_ckv, cache_mla_krope, state_gdn, state_conv, meta,
              ffn1_norm, ffn1_wg, ffn1_wu, ffn1_wd, mix_norm, w_in, conv_w, a_log, dt_bias, gdn_norm,
              q_norm, kv_norm, w_uq, w_ukv, w_out, ffn2_norm, ffn2_wg, ffn2_wu, ffn2_wd, final_norm):
    bsz, s_len = x_prompt.shape[:2]
    d_seq = x_sample.shape[1]
    xp = jnp.concatenate([jnp.broadcast_to(meta[None].astype(x_prompt.dtype), (bsz, N_META, D_MODEL)), x_prompt], axis=1)
    xs = x_sample
    cos_p, sin_p = rope_tables(jnp.arange(N_META + s_len))
    cos_s, sin_s = rope_tables(cache_mla_ckv.shape[2] + jnp.arange(d_seq))
    key_chunk = jnp.concatenate([jnp.full((N_META,), -1, jnp.int32), jnp.arange(s_len, dtype=jnp.int32) // CHUNK])
    p_ckv, p_kr, p_gdn, p_conv, s_ckv, s_kr, s_gdn, s_conv = [], [], [], [], [], [], [], []
    for l in range(DEPTH):
        xp = half_ffn(xp, ffn1_norm[l], ffn1_wg[l], ffn1_wu[l], ffn1_wd[l])
        xs = half_ffn(xs, ffn1_norm[l], ffn1_wg[l], ffn1_wu[l], ffn1_wd[l])
        qkv_p, z_p, a_p, b_p, cq_p, ckv_p, kr_p = split_proj(rmsnorm(xp, mix_norm[l]) @ w_in[l])
        qkv_s, z_s, a_s, b_s, cq_s, ckv_s, kr_s = split_proj(rmsnorm(xs, mix_norm[l]) @ w_in[l])
        ext_p = jnp.pad(qkv_p, ((0, 0), (CONV_W - 1, 0), (0, 0)))
        o_p, m_p = gdn_prompt(*gdn_features(causal_conv(ext_p, conv_w[l]), a_p, b_p, a_log[l], dt_bias[l]))
        gdn_p = gdn_output(o_p, z_p, gdn_norm[l])
        ext_s = jnp.concatenate([state_conv[l].astype(qkv_s.dtype), qkv_s], axis=1)
        m_s, o_s = gdn_block(state_gdn[l].astype(jnp.float32), *gdn_features(causal_conv(ext_s, conv_w[l]), a_s, b_s, a_log[l], dt_bias[l]))
        gdn_s = gdn_output(o_s, z_s, gdn_norm[l])
        qn_p, qr_p, c_p, kro_p = mla_project(cq_p, ckv_p, kr_p, cos_p, sin_p, q_norm[l], kv_norm[l], w_uq[l])
        mla_p = mla_prompt(qn_p, qr_p, c_p, kro_p, w_ukv[l], key_chunk)
        qn_s, qr_s, c_s, kro_s = mla_project(cq_s, ckv_s, kr_s, cos_s, sin_s, q_norm[l], kv_norm[l], w_uq[l])
        c_all = jnp.concatenate([cache_mla_ckv[l].astype(c_s.dtype), c_s], axis=1)
        kr_all = jnp.concatenate([cache_mla_krope[l].astype(kro_s.dtype), kro_s], axis=1)
        kn_s, v_s = mla_expand(c_all, w_ukv[l])
        mla_s = mla_attend(qn_s, qr_s, kn_s, kr_all, v_s, None).reshape(xs.shape[0], d_seq, H_MLA * DV_MLA)
        xp = xp + jnp.concatenate([gdn_p, mla_p.astype(gdn_p.dtype)], axis=-1) @ w_out[l]
        xs = xs + jnp.concatenate([gdn_s, mla_s.astype(gdn_s.dtype)], axis=-1) @ w_out[l]
        xp = half_ffn(xp, ffn2_norm[l], ffn2_wg[l], ffn2_wu[l], ffn2_wd[l])
        xs = half_ffn(xs, ffn2_norm[l], ffn2_wg[l], ffn2_wu[l], ffn2_wd[l])
        p_ckv.append(c_p)
        p_kr.append(kro_p)
        p_gdn.append(m_p.astype(x_prompt.dtype))
        p_conv.append(qkv_p[:, -(CONV_W - 1):])
        s_ckv.append(c_s)
        s_kr.append(kro_s)
        s_gdn.append(m_s.astype(state_gdn.dtype))
        s_conv.append(ext_s[:, -(CONV_W - 1):])
    y_prompt = rmsnorm(xp[:, N_META:], final_norm)
    y_sample = rmsnorm(xs, final_norm)
    return (y_prompt, y_sample, jnp.stack(p_ckv), jnp.stack(p_kr), jnp.stack(p_gdn), jnp.stack(p_conv),
            jnp.stack(s_ckv), jnp.stack(s_kr), jnp.stack(s_gdn), jnp.stack(s_conv))
```

```python
import functools

import jax
import jax.numpy as jnp
from jax import lax
from jax.experimental import pallas as pl
from jax.experimental.pallas import tpu as pltpu

F32 = jnp.float32
BF16 = jnp.bfloat16

D_MODEL = 1024
CHUNK = 64
N_META = 16
H_GDN = 4
GDN_DK = 128
GDN_DV = 128
CONV_W = 4
CONV_DIM = H_GDN * (2 * GDN_DK + GDN_DV)
H_MLA = 4
Q_LORA = 384
KV_LORA = 256
DN = 128
DR = 64
DV_MLA = 128
ROPE_BASE = 10000.0
SM_SCALE = (DN + DR) ** -0.5
D_FF = 2816
EPS = 1e-6
L2_EPS = 1e-6

LANES = 128
SUBLANES = 8
FF_CHUNK = 256
MLA_IN = Q_LORA + KV_LORA + LANES
QK_PAD = 2 * LANES
AB_LANE = DR
VMEM_LIMIT = 56 * 1024 * 1024

NT_DIMS = (((1,), (1,)), ((), ()))


def _rms(x, g):
    return x * lax.rsqrt(jnp.mean(x * x, axis=-1, keepdims=True) + EPS) * g


def _bdot(a, b):
    return jnp.dot(a.astype(BF16), b.astype(BF16), preferred_element_type=F32)


def _bdot_nt(a, b):
    return lax.dot_general(a.astype(BF16), b.astype(BF16), NT_DIMS, preferred_element_type=F32)


def _const_spec(shape):
    nd = len(shape)
    return pl.BlockSpec(shape, lambda *_: (0,) * nd, pipeline_mode=pl.Buffered(1))


def _params(*sem):
    return pltpu.CompilerParams(dimension_semantics=sem, vmem_limit_bytes=VMEM_LIMIT)


def _swiglu_half(x, g_ref, wg_ref, wu_ref, wd_ref, h_ref, acc_ref):
    h_ref[...] = _rms(x, g_ref[...]).astype(BF16)
    acc_ref[...] = jnp.zeros_like(acc_ref)

    def body(f, carry):
        h = h_ref[...]
        gate = jnp.dot(h, wg_ref[f], preferred_element_type=F32)
        up = jnp.dot(h, wu_ref[f], preferred_element_type=F32)
        act = (jax.nn.silu(gate) * up).astype(BF16)
        acc_ref[...] += jnp.dot(act, wd_ref[f], preferred_element_type=F32)
        return carry

    lax.fori_loop(0, wg_ref.shape[0], body, 0)
    return x + 0.5 * acc_ref[...]


def _ffn_proj_kernel(x_ref, g1_ref, wg_ref, wu_ref, wd_ref, gm_ref, wqkv_ref, wz_ref, wmla_ref,
                     x1_ref, qkv_ref, z_ref, mla_ref, h_ref, acc_ref):
    x1 = _swiglu_half(x_ref[...], g1_ref, wg_ref, wu_ref, wd_ref, h_ref, acc_ref)
    x1_ref[...] = x1
    h_ref[...] = _rms(x1, gm_ref[...]).astype(BF16)
    qkv_ref[...] = jnp.dot(h_ref[...], wqkv_ref[...], preferred_element_type=F32)
    z_ref[...] = jnp.dot(h_ref[...], wz_ref[...], preferred_element_type=F32)
    mla_ref[...] = jnp.dot(h_ref[...], wmla_ref[...], preferred_element_type=F32)


def _row_tile(n, want):
    t = min(want, n)
    assert n % t == 0, (n, t)
    return t


def ffn_proj(x, w, tm=512):
    n = x.shape[0]
    tm = _row_tile(n, tm)
    row = lambda width: pl.BlockSpec((tm, width), lambda i: (i, 0))
    consts = (w['g1'], w['wg1'], w['wu1'], w['wd1'], w['gm'], w['w_qkv'], w['w_z'], w['w_mla'])
    return pl.pallas_call(
        _ffn_proj_kernel,
        grid=(n // tm,),
        in_specs=[row(D_MODEL)] + [_const_spec(c.shape) for c in consts],
        out_specs=[row(D_MODEL), row(CONV_DIM), row(H_GDN * GDN_DV), row(MLA_IN)],
        out_shape=[jax.ShapeDtypeStruct((n, D_MODEL), F32), jax.ShapeDtypeStruct((n, CONV_DIM), F32),
                   jax.ShapeDtypeStruct((n, H_GDN * GDN_DV), F32), jax.ShapeDtypeStruct((n, MLA_IN), F32)],
        scratch_shapes=[pltpu.VMEM((tm, D_MODEL), BF16), pltpu.VMEM((tm, D_MODEL), F32)],
        compiler_params=_params("parallel"),
        name="ffn_proj",
    )(x, *consts)


def _out_ffn_kernel(x1_ref, gdn_ref, mla_ref, wog_ref, wom_ref, g2_ref, wg_ref, wu_ref, wd_ref, gf_ref,
                    y_ref, h_ref, acc_ref):
    x2 = (x1_ref[...] + jnp.dot(gdn_ref[...], wog_ref[...], preferred_element_type=F32)
          + jnp.dot(mla_ref[...], wom_ref[...], preferred_element_type=F32))
    x3 = _swiglu_half(x2, g2_ref, wg_ref, wu_ref, wd_ref, h_ref, acc_ref)
    y_ref[...] = _rms(x3, gf_ref[...])


def out_ffn(x1, gdn, mla, w, tm=512):
    n = x1.shape[0]
    tm = _row_tile(n, tm)
    row = lambda width: pl.BlockSpec((tm, width), lambda i: (i, 0))
    consts = (w['w_out_g'], w['w_out_m'], w['g2'], w['wg2'], w['wu2'], w['wd2'], w['gf'])
    return pl.pallas_call(
        _out_ffn_kernel,
        grid=(n // tm,),
        in_specs=[row(D_MODEL), row(H_GDN * GDN_DV), row(H_MLA * DV_MLA)] + [_const_spec(c.shape) for c in consts],
        out_specs=row(D_MODEL),
        out_shape=jax.ShapeDtypeStruct((n, D_MODEL), F32),
        scratch_shapes=[pltpu.VMEM((tm, D_MODEL), BF16), pltpu.VMEM((tm, D_MODEL), F32)],
        compiler_params=_params("parallel"),
        name="out_ffn",
    )(x1, gdn, mla, *consts)


def _cumsum_rows(x):
    n = x.shape[0]
    row = lax.broadcasted_iota(jnp.int32, x.shape, 0)
    shift = 1
    while shift < n:
        x = x + jnp.where(row >= shift, pltpu.roll(x, shift, 0), 0.0)
        shift *= 2
    return x


def _transpose_rows(x):
    length = x.shape[0]
    sq = jnp.concatenate([x, jnp.zeros((LANES - length, LANES), x.dtype)], axis=0)
    return sq.T[:, :length]


def _gdn_kernel(qkv_ref, z_ref, ab_ref, meta_qkv_ref, meta_ab_ref, m0_ref, conv0_ref, convw_ref,
                alog_ref, dtb_ref, gn_ref, o_ref, mout_ref, conv_scr, m_scr, *, has_meta):
    c = pl.program_id(1)
    bg = qkv_ref.shape[0]
    length = qkv_ref.shape[1]
    hist = SUBLANES

    @pl.when(c == 0)
    def _():
        m_scr[...] = m0_ref[...]
        conv_scr[:, 0:hist, :] = conv0_ref[...]

    row = lax.broadcasted_iota(jnp.int32, (length, LANES), 0)
    ri = lax.broadcasted_iota(jnp.int32, (length, length), 0)
    ci = lax.broadcasted_iota(jnp.int32, (length, length), 1)
    causal = ci <= ri
    strict = ci < ri

    def per_batch(i, carry):
        x = qkv_ref[i]
        ab = ab_ref[i]
        if has_meta:
            is_meta = c == 0
            x = jnp.where(is_meta, meta_qkv_ref[...], x)
            ab = jnp.where(is_meta, meta_ab_ref[...], ab)
        conv_scr[i, hist:hist + length, :] = x
        u = conv_scr[i, pl.ds(hist - (CONV_W - 1), length), :] * convw_ref[0:1, :]
        for j in range(1, CONV_W):
            u = u + conv_scr[i, pl.ds(hist - (CONV_W - 1) + j, length), :] * convw_ref[j:j + 1, :]
        conv_scr[i, 0:hist, :] = conv_scr[i, length:length + hist, :]
        u = jax.nn.silu(u)

        g_all = -jnp.exp(alog_ref[...]) * jax.nn.softplus(ab + dtb_ref[...])
        beta_all = jax.nn.sigmoid(ab)
        if has_meta:
            valid = jnp.logical_or(c > 0, row >= length - N_META)
            g_all = jnp.where(valid, g_all, 0.0)
            beta_all = jnp.where(valid, beta_all, 0.0)
        gc_all = _cumsum_rows(g_all)
        gc_t = _transpose_rows(gc_all)

        for h in range(H_GDN):
            q = u[:, h * GDN_DK:(h + 1) * GDN_DK]
            k = u[:, (H_GDN + h) * GDN_DK:(H_GDN + h + 1) * GDN_DK]
            v = u[:, 2 * H_GDN * GDN_DK + h * GDN_DV:2 * H_GDN * GDN_DK + (h + 1) * GDN_DV]
            q = q * lax.rsqrt(jnp.sum(q * q, axis=-1, keepdims=True) + L2_EPS) * (GDN_DK ** -0.5)
            k = k * lax.rsqrt(jnp.sum(k * k, axis=-1, keepdims=True) + L2_EPS)
            gc = gc_all[:, AB_LANE + h:AB_LANE + h + 1]
            beta = beta_all[:, AB_LANE + H_GDN + h:AB_LANE + H_GDN + h + 1]
            gc_row = gc_t[AB_LANE + h:AB_LANE + h + 1, :]
            decay = jnp.where(causal, jnp.exp(jnp.where(causal, gc - gc_row, 0.0)), 0.0)

            kq = jnp.concatenate([k, q], axis=0).astype(BF16)
            kk_qk = lax.dot_general(kq, k.astype(BF16), NT_DIMS, preferred_element_type=F32)
            a = jnp.where(strict, beta * kk_qk[:length] * decay, 0.0)
            qk_decay = kk_qk[length:] * decay

            y = -a
            pw = _bdot(a, a)
            span = 2
            while span < length:
                y_next = y + pw + _bdot(y, pw)
                span *= 2
                if span < length:
                    pw = _bdot(pw, pw)
                y = y_next

            m0 = m_scr[i, h]
            kqm = jnp.dot(kq, m0.astype(BF16), preferred_element_type=F32)
            eg = jnp.exp(gc)
            rhs = beta * (v - eg * kqm[:length])
            uu = rhs + _bdot(y, rhs)
            o = eg * kqm[length:] + _bdot(qk_decay, uu)
            g_last = gc[length - 1:length, :]
            k_dec = k * jnp.exp(g_last - gc)
            m_scr[i, h] = jnp.exp(g_last) * m0 + _bdot(_transpose_rows(k_dec), uu)

            n = _rms(o, gn_ref[...])
            gate = jax.nn.silu(z_ref[i, :, h * GDN_DV:(h + 1) * GDN_DV])
            o_ref[i, :, h * GDN_DV:(h + 1) * GDN_DV] = (n * gate).astype(o_ref.dtype)
        return carry

    lax.fori_loop(0, bg, per_batch, 0)

    @pl.when(c == pl.num_programs(1) - 1)
    def _():
        mout_ref[...] = m_scr[...]


def gdn(qkv, z, mla_in, meta_qkv, meta_ab, m0, conv0, w, *, has_meta, bg=4):
    b, t, _ = qkv.shape
    assert b % bg == 0 and t % CHUNK == 0
    nblk = t // CHUNK + (1 if has_meta else 0)
    if has_meta:
        blk = lambda g, c: (g, jnp.maximum(c - 1, 0), 0)
    else:
        blk = lambda g, c: (g, c, 0)
    ab_blk = (lambda g, c: blk(g, c)[:2] + (MLA_IN // LANES - 1,))
    const2 = lambda shape: pl.BlockSpec(shape, lambda g, c: (0, 0))
    kern = functools.partial(_gdn_kernel, has_meta=has_meta)
    return pl.pallas_call(
        kern,
        grid=(b // bg, nblk),
        in_specs=[
            pl.BlockSpec((bg, CHUNK, CONV_DIM), blk),
            pl.BlockSpec((bg, CHUNK, H_GDN * GDN_DV), blk),
            pl.BlockSpec((bg, CHUNK, LANES), ab_blk),
            const2((CHUNK, CONV_DIM)),
            const2((CHUNK, LANES)),
            pl.BlockSpec((bg, H_GDN, GDN_DK, GDN_DV), lambda g, c: (g, 0, 0, 0)),
            pl.BlockSpec((bg, SUBLANES, CONV_DIM), lambda g, c: (g, 0, 0)),
            const2((CONV_W, CONV_DIM)),
            const2((1, LANES)),
            const2((1, LANES)),
            const2((1, GDN_DV)),
        ],
        out_specs=[
            pl.BlockSpec((bg, CHUNK, H_GDN * GDN_DV), blk),
            pl.BlockSpec((bg, H_GDN, GDN_DK, GDN_DV), lambda g, c: (g, 0, 0, 0)),
        ],
        out_shape=[jax.ShapeDtypeStruct((b, t, H_GDN * GDN_DV), BF16),
                   jax.ShapeDtypeStruct((b, H_GDN, GDN_DK, GDN_DV), F32)],
        scratch_shapes=[pltpu.VMEM((bg, SUBLANES + CHUNK, CONV_DIM), F32),
                        pltpu.VMEM((bg, H_GDN, GDN_DK, GDN_DV), F32)],
        compiler_params=_params("parallel", "arbitrary"),
        name="gdn_meta" if has_meta else "gdn",
    )(qkv, z, mla_in, meta_qkv, meta_ab, m0, conv0, w['conv_w'], w['alog'], w['dtb'], w['gn'])


def _rope(t, cos, sin):
    lane = lax.broadcasted_iota(jnp.int32, t.shape, 1)
    half = DR // 2
    swapped = jnp.where(lane < half, pltpu.roll(t, LANES - half, 1), pltpu.roll(t, half, 1))
    return t * cos + swapped * sin


def _mla_prep_kernel(x_ref, cos_ref, sin_ref, gq_ref, gkv_ref, wuq_ref, wukv_ref, *out_refs, expand):
    q_ref, c_ref, kr_ref = out_refs[:3]
    x = x_ref[...]
    cos = cos_ref[...]
    sin = sin_ref[...]
    q = _bdot(_rms(x[:, :Q_LORA], gq_ref[...]), wuq_ref[...])
    for h in range(H_MLA):
        lo = h * QK_PAD
        q_ref[:, lo:lo + DN] = (q[:, lo:lo + DN] * SM_SCALE).astype(BF16)
        q_ref[:, lo + DN:lo + QK_PAD] = (_rope(q[:, lo + DN:lo + QK_PAD], cos, sin) * SM_SCALE).astype(BF16)
    c = _rms(x[:, Q_LORA:Q_LORA + KV_LORA], gkv_ref[...])
    c_ref[...] = c
    kr = _rope(x[:, Q_LORA + KV_LORA:], cos, sin)
    kr_ref[...] = kr
    if expand:
        k_ref, v_ref = out_refs[3:]
        kv = _bdot(c, wukv_ref[...])
        for h in range(H_MLA):
            lo = h * (DN + DV_MLA)
            k_ref[:, h * QK_PAD:h * QK_PAD + DN] = kv[:, lo:lo + DN].astype(BF16)
            k_ref[:, h * QK_PAD + DN:(h + 1) * QK_PAD] = kr.astype(BF16)
            v_ref[:, h * DV_MLA:(h + 1) * DV_MLA] = kv[:, lo + DN:lo + DN + DV_MLA].astype(BF16)


def mla_prep(mla_in, cos, sin, w, *, expand, tm=512):
    n = mla_in.shape[0]
    tm = _row_tile(min(n, cos.shape[0]), tm)
    nrep = cos.shape[0] // tm
    row = lambda width: pl.BlockSpec((tm, width), lambda i: (i, 0))
    tab = pl.BlockSpec((tm, LANES), lambda i: (i % nrep, 0))
    consts = (w['gq'], w['gkv'], w['w_uq'], w['w_ukv'])
    out_specs = [row(H_MLA * QK_PAD), row(KV_LORA), row(LANES)]
    out_shape = [jax.ShapeDtypeStruct((n, H_MLA * QK_PAD), BF16), jax.ShapeDtypeStruct((n, KV_LORA), F32),
                 jax.ShapeDtypeStruct((n, LANES), F32)]
    if expand:
        out_specs += [row(H_MLA * QK_PAD), row(H_MLA * DV_MLA)]
        out_shape += [jax.ShapeDtypeStruct((n, H_MLA * QK_PAD), BF16), jax.ShapeDtypeStruct((n, H_MLA * DV_MLA), BF16)]
    return pl.pallas_call(
        functools.partial(_mla_prep_kernel, expand=expand),
        grid=(n // tm,),
        in_specs=[row(MLA_IN), tab, tab] + [_const_spec(c.shape) for c in consts],
        out_specs=out_specs,
        out_shape=out_shape,
        compiler_params=_params("parallel"),
        name="mla_prep_kv" if expand else "mla_prep",
    )(mla_in, cos, sin, *consts)


def _mla_prompt_kernel(q_ref, k_ref, v_ref, km_ref, vm_ref, o_ref):
    qi = pl.program_id(2)
    tq = q_ref.shape[0]
    q = q_ref[...]

    def online(s, v, m, l, acc):
        m_new = jnp.maximum(m, jnp.max(s, axis=-1, keepdims=True))
        alpha = jnp.exp(m - m_new)
        p = jnp.exp(s - m_new)
        l = alpha * l + jnp.sum(p, axis=-1, keepdims=True)
        acc = alpha * acc + jnp.dot(p.astype(BF16), v, preferred_element_type=F32)
        return m_new, l, acc

    s = lax.dot_general(q, km_ref[...], NT_DIMS, preferred_element_type=F32)
    m = jnp.max(s, axis=-1, keepdims=True)
    p = jnp.exp(s - m)
    l = jnp.sum(p, axis=-1, keepdims=True)
    acc = jnp.dot(p.astype(BF16), vm_ref[...], preferred_element_type=F32)

    def full_tile(t, carry):
        off = pl.multiple_of(t * tq, tq)
        s = lax.dot_general(q, k_ref[pl.ds(off, tq), :], NT_DIMS, preferred_element_type=F32)
        return online(s, v_ref[pl.ds(off, tq), :], *carry)

    m, l, acc = lax.fori_loop(0, qi, full_tile, (m, l, acc))

    off = pl.multiple_of(qi * tq, tq)
    s = lax.dot_general(q, k_ref[pl.ds(off, tq), :], NT_DIMS, preferred_element_type=F32)
    rc = lax.broadcasted_iota(jnp.int32, (tq, tq), 0) // CHUNK
    cc = lax.broadcasted_iota(jnp.int32, (tq, tq), 1) // CHUNK
    s = jnp.where(cc <= rc, s, -jnp.inf)
    m, l, acc = online(s, v_ref[pl.ds(off, tq), :], m, l, acc)
    o_ref[...] = (acc / l).astype(o_ref.dtype)


def mla_prompt(q, k, v, k_meta, v_meta, tq=256):
    b, s, _ = q.shape
    tq = _row_tile(s, tq)
    assert tq % CHUNK == 0
    return pl.pallas_call(
        _mla_prompt_kernel,
        grid=(b, H_MLA, s // tq),
        in_specs=[
            pl.BlockSpec((None, tq, QK_PAD), lambda bi, h, i: (bi, i, h)),
            pl.BlockSpec((None, s, QK_PAD), lambda bi, h, i: (bi, 0, h)),
            pl.BlockSpec((None, s, DV_MLA), lambda bi, h, i: (bi, 0, h)),
            pl.BlockSpec((N_META, QK_PAD), lambda bi, h, i: (0, h)),
            pl.BlockSpec((N_META, DV_MLA), lambda bi, h, i: (0, h)),
        ],
        out_specs=pl.BlockSpec((None, tq, DV_MLA), lambda bi, h, i: (bi, i, h)),
        out_shape=jax.ShapeDtypeStruct((b, s, H_MLA * DV_MLA), BF16),
        compiler_params=_params("parallel", "parallel", "arbitrary"),
        name="mla_prompt",
    )(q, k, v, k_meta, v_meta)


def _mla_sample_kernel(q_ref, c_ref, kr_ref, cc_ref, ckr_ref, wukv_ref, o_ref):
    t = q_ref.shape[0]
    q = q_ref[...]
    w = wukv_ref[...]
    qa = jnp.concatenate(
        [lax.dot_general(q[:, h * QK_PAD:h * QK_PAD + DN], w[:, h * (DN + DV_MLA):h * (DN + DV_MLA) + DN],
                         NT_DIMS, preferred_element_type=F32) for h in range(H_MLA)], axis=0).astype(BF16)
    qr = jnp.concatenate([q[:, h * QK_PAD + DN:(h + 1) * QK_PAD] for h in range(H_MLA)], axis=0)
    cache_c = cc_ref[...].astype(BF16)
    own_c = c_ref[...].astype(BF16)
    s_cache = (lax.dot_general(qa, cache_c, NT_DIMS, preferred_element_type=F32)
               + lax.dot_general(qr[:, :DR], ckr_ref[...].astype(BF16), NT_DIMS, preferred_element_type=F32))
    s_own = (lax.dot_general(qa, own_c, NT_DIMS, preferred_element_type=F32)
             + lax.dot_general(qr, kr_ref[...].astype(BF16), NT_DIMS, preferred_element_type=F32))
    m = jnp.maximum(jnp.max(s_cache, axis=-1, keepdims=True), jnp.max(s_own, axis=-1, keepdims=True))
    p_cache = jnp.exp(s_cache - m)
    p_own = jnp.exp(s_own - m)
    l = jnp.sum(p_cache, axis=-1, keepdims=True) + jnp.sum(p_own, axis=-1, keepdims=True)
    pc = (jnp.dot(p_cache.astype(BF16), cache_c, preferred_element_type=F32)
          + jnp.dot(p_own.astype(BF16), own_c, preferred_element_type=F32)) / l
    for h in range(H_MLA):
        lo = h * (DN + DV_MLA) + DN
        o_ref[:, h * DV_MLA:(h + 1) * DV_MLA] = _bdot(pc[h * t:(h + 1) * t], w[:, lo:lo + DV_MLA]).astype(o_ref.dtype)


def mla_sample(q, c, kr, cache_c, cache_kr, w_ukv):
    b, t, _ = q.shape
    p = cache_c.shape[1]
    per_b = lambda rows, width: pl.BlockSpec((None, rows, width), lambda bi: (bi, 0, 0))
    return pl.pallas_call(
        _mla_sample_kernel,
        grid=(b,),
        in_specs=[per_b(t, H_MLA * QK_PAD), per_b(t, KV_LORA), per_b(t, LANES), per_b(p, KV_LORA), per_b(p, DR),
                  _const_spec(w_ukv.shape)],
        out_specs=per_b(t, H_MLA * DV_MLA),
        out_shape=jax.ShapeDtypeStruct((b, t, H_MLA * DV_MLA), BF16),
        compiler_params=_params("parallel"),
        name="mla_sample",
    )(q, c, kr, cache_c, cache_kr, w_ukv)


def _rope_tables(pos):
    inv = ROPE_BASE ** (-jnp.arange(0, DR, 2, dtype=F32) / DR)
    ang = pos.astype(F32)[:, None] * inv[None, :]
    cos, sin = jnp.cos(ang), jnp.sin(ang)
    pad = jnp.zeros((pos.shape[0], LANES - DR), F32)
    return jnp.concatenate([cos, cos, pad], axis=1), jnp.concatenate([-sin, sin, pad], axis=1)


def _ff_chunks(wg, wu, wd):
    nf = D_FF // FF_CHUNK
    col = lambda m: m.astype(BF16).reshape(D_MODEL, nf, FF_CHUNK).transpose(1, 0, 2)
    return col(wg), col(wu), wd.astype(BF16).reshape(nf, FF_CHUNK, D_MODEL)


def _prepare_weights(ffn1_norm, ffn1_wg, ffn1_wu, ffn1_wd, mix_norm, w_in, conv_w, a_log, dt_bias, gdn_norm,
                     q_norm, kv_norm, w_uq, w_ukv, w_out, ffn2_norm, ffn2_wg, ffn2_wu, ffn2_wd, final_norm):
    w = {}
    w['g1'], w['gm'], w['g2'] = ffn1_norm[0][None], mix_norm[0][None], ffn2_norm[0][None]
    w['gf'] = final_norm[None]
    w['wg1'], w['wu1'], w['wd1'] = _ff_chunks(ffn1_wg[0], ffn1_wu[0], ffn1_wd[0])
    w['wg2'], w['wu2'], w['wd2'] = _ff_chunks(ffn2_wg[0], ffn2_wu[0], ffn2_wd[0])
    o_z = CONV_DIM
    o_a = o_z + H_GDN * GDN_DV
    o_b = o_a + H_GDN
    o_cq = o_b + H_GDN
    o_kr = o_cq + Q_LORA + KV_LORA
    wi = w_in[0]
    w['w_qkv'] = wi[:, :o_z].astype(BF16)
    w['w_z'] = wi[:, o_z:o_a].astype(BF16)
    tail_pad = jnp.zeros((D_MODEL, LANES - DR - 2 * H_GDN), wi.dtype)
    w['w_mla'] = jnp.concatenate([wi[:, o_cq:o_kr], wi[:, o_kr:o_kr + DR], wi[:, o_a:o_cq], tail_pad], axis=1).astype(BF16)
    w['conv_w'] = conv_w[0]
    lane_vec = lambda v: jnp.zeros((1, LANES), F32).at[0, AB_LANE:AB_LANE + H_GDN].set(v.astype(F32))
    w['alog'], w['dtb'] = lane_vec(a_log[0]), lane_vec(dt_bias[0])
    w['gn'] = gdn_norm[0][None]
    w['gq'], w['gkv'] = q_norm[0][None], kv_norm[0][None]
    uq = w_uq[0].reshape(Q_LORA, H_MLA, DN + DR)
    uq = jnp.concatenate([uq, jnp.zeros((Q_LORA, H_MLA, QK_PAD - DN - DR), uq.dtype)], axis=-1)
    w['w_uq'] = uq.reshape(Q_LORA, H_MLA * QK_PAD).astype(BF16)
    w['w_ukv'] = w_ukv[0].astype(BF16)
    w['w_out_g'] = w_out[0][:H_GDN * GDN_DV].astype(BF16)
    w['w_out_m'] = w_out[0][H_GDN * GDN_DV:].astype(BF16)
    return w


def kernel(x_prompt, x_sample, cache_mla_ckv, cache_mla_krope, state_gdn, state_conv, meta, ffn1_norm, ffn1_wg,
           ffn1_wu, ffn1_wd, mix_norm, w_in, conv_w, a_log, dt_bias, gdn_norm, q_norm, kv_norm, w_uq, w_ukv, w_out,
           ffn2_norm, ffn2_wg, ffn2_wu, ffn2_wd, final_norm):
    assert ffn1_wg.shape[0] == 1, "one layer: the meta rows are not carried past the mixer"
    bsz, s_len, _ = x_prompt.shape
    dbs, d_seq, _ = x_sample.shape
    past = cache_mla_ckv.shape[2]
    w = _prepare_weights(ffn1_norm, ffn1_wg, ffn1_wu, ffn1_wd, mix_norm, w_in, conv_w, a_log, dt_bias, gdn_norm,
                         q_norm, kv_norm, w_uq, w_ukv, w_out, ffn2_norm, ffn2_wg, ffn2_wu, ffn2_wd, final_norm)

    x1_p, qkv_p, z_p, mla_p = ffn_proj(x_prompt.reshape(bsz * s_len, D_MODEL), w)
    x1_s, qkv_s, z_s, mla_s = ffn_proj(x_sample.reshape(dbs * d_seq, D_MODEL), w)
    _, qkv_m, _, mla_m = ffn_proj(meta.astype(F32), w)

    front = CHUNK - N_META
    meta_qkv = jnp.pad(qkv_m, ((front, 0), (0, 0)))
    meta_ab = jnp.pad(mla_m[:, MLA_IN - LANES:], ((front, 0), (0, 0)))
    qkv_p3 = qkv_p.reshape(bsz, s_len, CONV_DIM)
    qkv_s3 = qkv_s.reshape(dbs, d_seq, CONV_DIM)
    gdn_p, m_p = gdn(qkv_p3, z_p.reshape(bsz, s_len, -1), mla_p.reshape(bsz, s_len, MLA_IN), meta_qkv, meta_ab,
                     jnp.zeros((bsz, H_GDN, GDN_DK, GDN_DV), F32), jnp.zeros((bsz, SUBLANES, CONV_DIM), F32), w,
                     has_meta=True)
    conv0_s = jnp.pad(state_conv[0].astype(F32), ((0, 0), (SUBLANES - (CONV_W - 1), 0), (0, 0)))
    gdn_s, m_s = gdn(qkv_s3, z_s.reshape(dbs, d_seq, -1), mla_s.reshape(dbs, d_seq, MLA_IN), jnp.zeros_like(meta_qkv),
                     jnp.zeros_like(meta_ab), state_gdn[0].astype(F32), conv0_s, w, has_meta=False)

    cos_m, sin_m = _rope_tables(jnp.arange(N_META))
    cos_p, sin_p = _rope_tables(N_META + jnp.arange(s_len))
    cos_s, sin_s = _rope_tables(past + jnp.arange(d_seq))
    q_p, c_p, kr_p, k_p, v_p = mla_prep(mla_p, cos_p, sin_p, w, expand=True)
    _, c_m, kr_m, k_m, v_m = mla_prep(mla_m, cos_m, sin_m, w, expand=True)
    q_s, c_s, kr_s = mla_prep(mla_s, cos_s, sin_s, w, expand=False)
    mla_o_p = mla_prompt(q_p.reshape(bsz, s_len, -1), k_p.reshape(bsz, s_len, -1), v_p.reshape(bsz, s_len, -1), k_m, v_m)
    mla_o_s = mla_sample(q_s.reshape(dbs, d_seq, -1), c_s.reshape(dbs, d_seq, -1), kr_s.reshape(dbs, d_seq, -1),
                         cache_mla_ckv[0].astype(F32), cache_mla_krope[0].astype(F32), w['w_ukv'])

    y_p = out_ffn(x1_p, gdn_p.reshape(bsz * s_len, -1), mla_o_p.reshape(bsz * s_len, -1), w)
    y_s = out_ffn(x1_s, gdn_s.reshape(dbs * d_seq, -1), mla_o_s.reshape(dbs * d_seq, -1), w)

    with_meta = lambda m_rows, rows: jnp.concatenate(
        [jnp.broadcast_to(m_rows[None], (bsz,) + m_rows.shape), rows.reshape(bsz, s_len, -1)], axis=1)
    return (y_p.reshape(bsz, s_len, D_MODEL), y_s.reshape(dbs, d_seq, D_MODEL),
            with_meta(c_m, c_p)[None], with_meta(kr_m[:, :DR], kr_p[:, :DR])[None],
            m_p[None], qkv_p3[:, s_len - (CONV_W - 1):][None],
            c_s.reshape(dbs, d_seq, KV_LORA)[None], kr_s[:, :DR].reshape(dbs, d_seq, DR)[None],
            m_s[None], qkv_s3[:, d_seq - (CONV_W - 1):][None])
```

```python
import functools

import jax
import jax.numpy as jnp
from jax import lax
from jax.experimental import pallas as pl
from jax.experimental.pallas import tpu as pltpu

F32 = jnp.float32
BF16 = jnp.bfloat16

D_MODEL = 1024
CHUNK = 64
N_META = 16
H_GDN = 4
GDN_DK = 128
GDN_DV = 128
CONV_W = 4
CONV_DIM = H_GDN * (2 * GDN_DK + GDN_DV)
H_MLA = 4
Q_LORA = 384
KV_LORA = 256
DN = 128
DR = 64
DV_MLA = 128
ROPE_BASE = 10000.0
SM_SCALE = (DN + DR) ** -0.5
D_FF = 2816
EPS = 1e-6
L2_EPS = 1e-6

LANES = 128
SUBLANES = 8
FF_CHUNK = 256
MLA_IN = Q_LORA + KV_LORA + LANES
QK_PAD = 2 * LANES
AB_LANE = DR
VMEM_LIMIT = 56 * 1024 * 1024

NT_DIMS = (((1,), (1,)), ((), ()))


def _rms(x, g):
    return x * lax.rsqrt(jnp.mean(x * x, axis=-1, keepdims=True) + EPS) * g


def _bdot(a, b):
    return jnp.dot(a.astype(BF16), b.astype(BF16), preferred_element_type=F32)


def _bdot_nt(a, b):
    return lax.dot_general(a.astype(BF16), b.astype(BF16), NT_DIMS, preferred_element_type=F32)


def _const_spec(shape):
    nd = len(shape)
    return pl.BlockSpec(shape, lambda *_: (0,) * nd, pipeline_mode=pl.Buffered(1))


def _params(*sem):
    return pltpu.CompilerParams(dimension_semantics=sem, vmem_limit_bytes=VMEM_LIMIT)


def _swiglu_half(x, g_ref, wg_ref, wu_ref, wd_ref, h_ref, acc_ref):
    h_ref[...] = _rms(x, g_ref[...]).astype(BF16)
    acc_ref[...] = jnp.zeros_like(acc_ref)

    def body(f, carry):
        h = h_ref[...]
        gate = jnp.dot(h, wg_ref[f], preferred_element_type=F32)
        up = jnp.dot(h, wu_ref[f], preferred_element_type=F32)
        act = (jax.nn.silu(gate) * up).astype(BF16)
        acc_ref[...] += jnp.dot(act, wd_ref[f], preferred_element_type=F32)
        return carry

    lax.fori_loop(0, wg_ref.shape[0], body, 0)
    return x + 0.5 * acc_ref[...]


def _ffn_proj_kernel(x_ref, g1_ref, wg_ref, wu_ref, wd_ref, gm_ref, wqkv_ref, wz_ref, wmla_ref,
                     x1_ref, qkv_ref, z_ref, mla_ref, h_ref, acc_ref):
    x1 = _swiglu_half(x_ref[...], g1_ref, wg_ref, wu_ref, wd_ref, h_ref, acc_ref)
    x1_ref[...] = x1
    h_ref[...] = _rms(x1, gm_ref[...]).astype(BF16)
    qkv_ref[...] = jnp.dot(h_ref[...], wqkv_ref[...], preferred_element_type=F32)
    z_ref[...] = jnp.dot(h_ref[...], wz_ref[...], preferred_element_type=F32)
    mla_ref[...] = jnp.dot(h_ref[...], wmla_ref[...], preferred_element_type=F32)


def _row_tile(n, want):
    t = min(want, n)
    assert n % t == 0, (n, t)
    return t


def ffn_proj(x, w, tm=512):
    n = x.shape[0]
    tm = _row_tile(n, tm)
    row = lambda width: pl.BlockSpec((tm, width), lambda i: (i, 0))
    consts = (w['g1'], w['wg1'], w['wu1'], w['wd1'], w['gm'], w['w_qkv'], w['w_z'], w['w_mla'])
    return pl.pallas_call(
        _ffn_proj_kernel,
        grid=(n // tm,),
        in_specs=[row(D_MODEL)] + [_const_spec(c.shape) for c in consts],
        out_specs=[row(D_MODEL), row(CONV_DIM), row(H_GDN * GDN_DV), row(MLA_IN)],
        out_shape=[jax.ShapeDtypeStruct((n, D_MODEL), F32), jax.ShapeDtypeStruct((n, CONV_DIM), F32),
                   jax.ShapeDtypeStruct((n, H_GDN * GDN_DV), F32), jax.ShapeDtypeStruct((n, MLA_IN), F32)],
        scratch_shapes=[pltpu.VMEM((tm, D_MODEL), BF16), pltpu.VMEM((tm, D_MODEL), F32)],
        compiler_params=_params("parallel"),
        name="ffn_proj",
    )(x, *consts)


def _out_ffn_kernel(x1_ref, gdn_ref, mla_ref, wog_ref, wom_ref, g2_ref, wg_ref, wu_ref, wd_ref, gf_ref,
                    y_ref, h_ref, acc_ref):
    x2 = (x1_ref[...] + jnp.dot(gdn_ref[...], wog_ref[...], preferred_element_type=F32)
          + jnp.dot(mla_ref[...], wom_ref[...], preferred_element_type=F32))
    x3 = _swiglu_half(x2, g2_ref, wg_ref, wu_ref, wd_ref, h_ref, acc_ref)
    y_ref[...] = _rms(x3, gf_ref[...])


def out_ffn(x1, gdn, mla, w, tm=512):
    n = x1.shape[0]
    tm = _row_tile(n, tm)
    row = lambda width: pl.BlockSpec((tm, width), lambda i: (i, 0))
    consts = (w['w_out_g'], w['w_out_m'], w['g2'], w['wg2'], w['wu2'], w['wd2'], w['gf'])
    return pl.pallas_call(
        _out_ffn_kernel,
        grid=(n // tm,),
        in_specs=[row(D_MODEL), row(H_GDN * GDN_DV), row(H_MLA * DV_MLA)] + [_const_spec(c.shape) for c in consts],
        out_specs=row(D_MODEL),
        out_shape=jax.ShapeDtypeStruct((n, D_MODEL), F32),
        scratch_shapes=[pltpu.VMEM((tm, D_MODEL), BF16), pltpu.VMEM((tm, D_MODEL), F32)],
        compiler_params=_params("parallel"),
        name="out_ffn",
    )(x1, gdn, mla, *consts)


def _cumsum_rows(x):
    n = x.shape[0]
    row = lax.broadcasted_iota(jnp.int32, x.shape, 0)
    shift = 1
    while shift < n:
        x = x + jnp.where(row >= shift, pltpu.roll(x, shift, 0), 0.0)
        shift *= 2
    return x


def _transpose_rows(x):
    length = x.shape[0]
    sq = jnp.concatenate([x, jnp.zeros((LANES - length, LANES), x.dtype)], axis=0)
    return sq.T[:, :length]


def _gdn_kernel(qkv_ref, z_ref, ab_ref, meta_qkv_ref, meta_ab_ref, m0_ref, conv0_ref, convw_ref,
                alog_ref, dtb_ref, gn_ref, o_ref, mout_ref, conv_scr, m_scr, *, has_meta):
    c = pl.program_id(1)
    bg = qkv_ref.shape[0]
    length = qkv_ref.shape[1]
    hist = SUBLANES

    @pl.when(c == 0)
    def _():
        m_scr[...] = m0_ref[...]
        conv_scr[:, 0:hist, :] = conv0_ref[...]

    row = lax.broadcasted_iota(jnp.int32, (length, LANES), 0)
    ri = lax.broadcasted_iota(jnp.int32, (length, length), 0)
    ci = lax.broadcasted_iota(jnp.int32, (length, length), 1)
    causal = ci <= ri
    strict = ci < ri

    def per_batch(i, carry):
        x = qkv_ref[i]
        ab = ab_ref[i]
        if has_meta:
            is_meta = c == 0
            x = jnp.where(is_meta, meta_qkv_ref[...], x)
            ab = jnp.where(is_meta, meta_ab_ref[...], ab)
        conv_scr[i, hist:hist + length, :] = x
        u = conv_scr[i, pl.ds(hist - (CONV_W - 1), length), :] * convw_ref[0:1, :]
        for j in range(1, CONV_W):
            u = u + conv_scr[i, pl.ds(hist - (CONV_W - 1) + j, length), :] * convw_ref[j:j + 1, :]
        conv_scr[i, 0:hist, :] = conv_scr[i, length:length + hist, :]
        u = jax.nn.silu(u)

        g_all = -jnp.exp(alog_ref[...]) * jax.nn.softplus(ab + dtb_ref[...])
        beta_all = jax.nn.sigmoid(ab)
        if has_meta:
            valid = jnp.logical_or(c > 0, row >= length - N_META)
            g_all = jnp.where(valid, g_all, 0.0)
            beta_all = jnp.where(valid, beta_all, 0.0)
        gc_all = _cumsum_rows(g_all)
        gc_t = _transpose_rows(gc_all)

        chains = []
        for h in range(H_GDN):
            q = u[:, h * GDN_DK:(h + 1) * GDN_DK]
            k = u[:, (H_GDN + h) * GDN_DK:(H_GDN + h + 1) * GDN_DK]
            v = u[:, 2 * H_GDN * GDN_DK + h * GDN_DV:2 * H_GDN * GDN_DK + (h + 1) * GDN_DV]
            q = q * lax.rsqrt(jnp.sum(q * q, axis=-1, keepdims=True) + L2_EPS) * (GDN_DK ** -0.5)
            k = k * lax.rsqrt(jnp.sum(k * k, axis=-1, keepdims=True) + L2_EPS)
            gc = gc_all[:, AB_LANE + h:AB_LANE + h + 1]
            beta = beta_all[:, AB_LANE + H_GDN + h:AB_LANE + H_GDN + h + 1]
            gc_row = gc_t[AB_LANE + h:AB_LANE + h + 1, :]
            decay = jnp.where(causal, jnp.exp(jnp.where(causal, gc - gc_row, 0.0)), 0.0)
            kq = jnp.concatenate([k, q], axis=0).astype(BF16)
            chains.append(dict(i=i, h=h, k=k, v=v, gc=gc, beta=beta, decay=decay, kq=kq))
        return chains

    chains = [ch for i in range(bg) for ch in per_batch(i, 0)]
    for ch in chains:
        ch['kk_qk'] = lax.dot_general(ch['kq'], ch['k'].astype(BF16), NT_DIMS, preferred_element_type=F32)
    for ch in chains:
        ch['a'] = jnp.where(strict, ch['beta'] * ch['kk_qk'][:length] * ch['decay'], 0.0)
        ch['qk_decay'] = ch['kk_qk'][length:] * ch['decay']
        ch['y'] = -ch['a']
    for ch in chains:
        ch['pw'] = _bdot(ch['a'], ch['a'])
    span = 2
    while span < length:
        span *= 2
        for ch in chains:
            ch['y'] = ch['y'] + ch['pw'] + _bdot(ch['y'], ch['pw'])
        if span < length:
            for ch in chains:
                ch['pw'] = _bdot(ch['pw'], ch['pw'])
    for ch in chains:
        ch['m0'] = m_scr[ch['i'], ch['h']]
        ch['kqm'] = jnp.dot(ch['kq'], ch['m0'].astype(BF16), preferred_element_type=F32)
    for ch in chains:
        ch['eg'] = jnp.exp(ch['gc'])
        rhs = ch['beta'] * (ch['v'] - ch['eg'] * ch['kqm'][:length])
        ch['uu'] = rhs + _bdot(ch['y'], rhs)
    for ch in chains:
        g_last = ch['gc'][length - 1:length, :]
        k_dec = ch['k'] * jnp.exp(g_last - ch['gc'])
        m_scr[ch['i'], ch['h']] = jnp.exp(g_last) * ch['m0'] + _bdot(_transpose_rows(k_dec), ch['uu'])
    for ch in chains:
        o = ch['eg'] * ch['kqm'][length:] + _bdot(ch['qk_decay'], ch['uu'])
        n = _rms(o, gn_ref[...])
        cols = slice(ch['h'] * GDN_DV, (ch['h'] + 1) * GDN_DV)
        o_ref[ch['i'], :, cols] = (n * jax.nn.silu(z_ref[ch['i'], :, cols])).astype(o_ref.dtype)

    @pl.when(c == pl.num_programs(1) - 1)
    def _():
        mout_ref[...] = m_scr[...]


def gdn(qkv, z, mla_in, meta_qkv, meta_ab, m0, conv0, w, *, has_meta, bg=4):
    b, t, _ = qkv.shape
    assert b % bg == 0 and t % CHUNK == 0
    nblk = t // CHUNK + (1 if has_meta else 0)
    if has_meta:
        blk = lambda g, c: (g, jnp.maximum(c - 1, 0), 0)
    else:
        blk = lambda g, c: (g, c, 0)
    ab_blk = (lambda g, c: blk(g, c)[:2] + (MLA_IN // LANES - 1,))
    const2 = lambda shape: pl.BlockSpec(shape, lambda g, c: (0, 0))
    kern = functools.partial(_gdn_kernel, has_meta=has_meta)
    return pl.pallas_call(
        kern,
        grid=(b // bg, nblk),
        in_specs=[
            pl.BlockSpec((bg, CHUNK, CONV_DIM), blk),
            pl.BlockSpec((bg, CHUNK, H_GDN * GDN_DV), blk),
            pl.BlockSpec((bg, CHUNK, LANES), ab_blk),
            const2((CHUNK, CONV_DIM)),
            const2((CHUNK, LANES)),
            pl.BlockSpec((bg, H_GDN, GDN_DK, GDN_DV), lambda g, c: (g, 0, 0, 0)),
            pl.BlockSpec((bg, SUBLANES, CONV_DIM), lambda g, c: (g, 0, 0)),
            const2((CONV_W, CONV_DIM)),
            const2((1, LANES)),
            const2((1, LANES)),
            const2((1, GDN_DV)),
        ],
        out_specs=[
            pl.BlockSpec((bg, CHUNK, H_GDN * GDN_DV), blk),
            pl.BlockSpec((bg, H_GDN, GDN_DK, GDN_DV), lambda g, c: (g, 0, 0, 0)),
        ],
        out_shape=[jax.ShapeDtypeStruct((b, t, H_GDN * GDN_DV), BF16),
                   jax.ShapeDtypeStruct((b, H_GDN, GDN_DK, GDN_DV), F32)],
        scratch_shapes=[pltpu.VMEM((bg, SUBLANES + CHUNK, CONV_DIM), F32),
                        pltpu.VMEM((bg, H_GDN, GDN_DK, GDN_DV), F32)],
        compiler_params=_params("parallel", "arbitrary"),
        name="gdn_meta" if has_meta else "gdn",
    )(qkv, z, mla_in, meta_qkv, meta_ab, m0, conv0, w['conv_w'], w['alog'], w['dtb'], w['gn'])


def _rope(t, cos, sin):
    lane = lax.broadcasted_iota(jnp.int32, t.shape, 1)
    half = DR // 2
    swapped = jnp.where(lane < half, pltpu.roll(t, LANES - half, 1), pltpu.roll(t, half, 1))
    return t * cos + swapped * sin


def _mla_prep_kernel(x_ref, cos_ref, sin_ref, gq_ref, gkv_ref, wuq_ref, wukv_ref, *out_refs, expand):
    q_ref, c_ref, kr_ref = out_refs[:3]
    x = x_ref[...]
    cos = cos_ref[...]
    sin = sin_ref[...]
    q = _bdot(_rms(x[:, :Q_LORA], gq_ref[...]), wuq_ref[...])
    for h in range(H_MLA):
        lo = h * QK_PAD
        q_ref[:, lo:lo + DN] = (q[:, lo:lo + DN] * SM_SCALE).astype(BF16)
        q_ref[:, lo + DN:lo + QK_PAD] = (_rope(q[:, lo + DN:lo + QK_PAD], cos, sin) * SM_SCALE).astype(BF16)
    c = _rms(x[:, Q_LORA:Q_LORA + KV_LORA], gkv_ref[...])
    c_ref[...] = c
    kr = _rope(x[:, Q_LORA + KV_LORA:], cos, sin)
    kr_ref[...] = kr
    if expand:
        k_ref, vt_ref = out_refs[3:]
        kv = _bdot(c, wukv_ref[...])
        for h in range(H_MLA):
            lo = h * (DN + DV_MLA)
            k_ref[:, h * QK_PAD:h * QK_PAD + DN] = kv[:, lo:lo + DN].astype(BF16)
            k_ref[:, h * QK_PAD + DN:(h + 1) * QK_PAD] = kr.astype(BF16)
            v = kv[:, lo + DN:lo + DN + DV_MLA]
            v_t = v.T if v.shape[0] % LANES == 0 else _transpose_rows(v)
            vt_ref[h * DV_MLA:(h + 1) * DV_MLA, :] = v_t.astype(BF16)


def mla_prep(mla_in, cos, sin, w, *, expand, tm=512):
    n = mla_in.shape[0]
    tm = _row_tile(min(n, cos.shape[0]), tm)
    nrep = cos.shape[0] // tm
    row = lambda width: pl.BlockSpec((tm, width), lambda i: (i, 0))
    tab = pl.BlockSpec((tm, LANES), lambda i: (i % nrep, 0))
    consts = (w['gq'], w['gkv'], w['w_uq'], w['w_ukv'])
    out_specs = [row(H_MLA * QK_PAD), row(KV_LORA), row(LANES)]
    out_shape = [jax.ShapeDtypeStruct((n, H_MLA * QK_PAD), BF16), jax.ShapeDtypeStruct((n, KV_LORA), F32),
                 jax.ShapeDtypeStruct((n, LANES), F32)]
    if expand:
        out_specs += [row(H_MLA * QK_PAD), pl.BlockSpec((H_MLA * DV_MLA, tm), lambda i: (0, i))]
        out_shape += [jax.ShapeDtypeStruct((n, H_MLA * QK_PAD), BF16), jax.ShapeDtypeStruct((H_MLA * DV_MLA, n), BF16)]
    return pl.pallas_call(
        functools.partial(_mla_prep_kernel, expand=expand),
        grid=(n // tm,),
        in_specs=[row(MLA_IN), tab, tab] + [_const_spec(c.shape) for c in consts],
        out_specs=out_specs,
        out_shape=out_shape,
        compiler_params=_params("parallel"),
        name="mla_prep_kv" if expand else "mla_prep",
    )(mla_in, cos, sin, *consts)


def _mla_prompt_kernel(q_ref, k_ref, vt_ref, km_ref, vmt_ref, o_ref):
    qi = pl.program_id(1)
    tq = q_ref.shape[0]
    heads = range(H_MLA)
    q = [q_ref[:, h * QK_PAD:(h + 1) * QK_PAD] for h in heads]

    def online(s_t, v_t, m, l, acc):
        m_new = jnp.maximum(m, jnp.max(s_t, axis=0, keepdims=True))
        alpha = jnp.exp(m - m_new)
        p_t = jnp.exp(s_t - m_new)
        l = alpha * l + jnp.sum(p_t, axis=0, keepdims=True)
        acc = alpha * acc + jnp.dot(v_t, p_t.astype(BF16), preferred_element_type=F32)
        return m_new, l, acc

    state = []
    for h in heads:
        s_t = lax.dot_general(km_ref[:, h * QK_PAD:(h + 1) * QK_PAD], q[h], NT_DIMS, preferred_element_type=F32)
        m = jnp.max(s_t, axis=0, keepdims=True)
        p_t = jnp.exp(s_t - m)
        l = jnp.sum(p_t, axis=0, keepdims=True)
        acc = jnp.dot(vmt_ref[h * DV_MLA:(h + 1) * DV_MLA, :], p_t.astype(BF16), preferred_element_type=F32)
        state.append((m, l, acc))

    def tile(off, state, masked):
        scores = [lax.dot_general(k_ref[pl.ds(off, tq), h * QK_PAD:(h + 1) * QK_PAD], q[h], NT_DIMS,
                                  preferred_element_type=F32) for h in heads]
        if masked:
            key_chunk = lax.broadcasted_iota(jnp.int32, (tq, tq), 0) // CHUNK
            qry_chunk = lax.broadcasted_iota(jnp.int32, (tq, tq), 1) // CHUNK
            scores = [jnp.where(key_chunk <= qry_chunk, s_t, -jnp.inf) for s_t in scores]
        stats = []
        for h in heads:
            m, l, acc = state[h]
            m_new = jnp.maximum(m, jnp.max(scores[h], axis=0, keepdims=True))
            alpha = jnp.exp(m - m_new)
            p_t = jnp.exp(scores[h] - m_new)
            stats.append((m_new, alpha * l + jnp.sum(p_t, axis=0, keepdims=True), alpha, p_t.astype(BF16)))
        return [(m_new, l, alpha * state[h][2] + jnp.dot(vt_ref[h * DV_MLA:(h + 1) * DV_MLA, pl.ds(off, tq)], p_t,
                                                         preferred_element_type=F32))
                for h, (m_new, l, alpha, p_t) in zip(heads, stats)]

    def full_tile(t, carry):
        state = [carry[3 * h:3 * h + 3] for h in heads]
        return tuple(x for st in tile(pl.multiple_of(t * tq, tq), state, False) for x in st)

    carry = lax.fori_loop(0, qi, full_tile, tuple(x for st in state for x in st))
    state = tile(pl.multiple_of(qi * tq, tq), [carry[3 * h:3 * h + 3] for h in heads], True)
    for h in heads:
        m, l, acc = state[h]
        o_ref[:, h * DV_MLA:(h + 1) * DV_MLA] = (acc / l).T.astype(o_ref.dtype)


def mla_prompt(q, k, v_t, k_meta, v_meta_t, tq=512):
    b, s, _ = q.shape
    tq = _row_tile(s, tq)
    assert tq % LANES == 0
    whole = lambda shape: pl.BlockSpec(shape, lambda bi, i: (0, 0))
    return pl.pallas_call(
        _mla_prompt_kernel,
        grid=(b, s // tq),
        in_specs=[
            pl.BlockSpec((None, tq, H_MLA * QK_PAD), lambda bi, i: (bi, i, 0)),
            pl.BlockSpec((None, s, H_MLA * QK_PAD), lambda bi, i: (bi, 0, 0), pipeline_mode=pl.Buffered(1)),
            pl.BlockSpec((H_MLA * DV_MLA, s), lambda bi, i: (0, bi), pipeline_mode=pl.Buffered(1)),
            whole(k_meta.shape),
            whole(v_meta_t.shape),
        ],
        out_specs=pl.BlockSpec((None, tq, H_MLA * DV_MLA), lambda bi, i: (bi, i, 0)),
        out_shape=jax.ShapeDtypeStruct((b, s, H_MLA * DV_MLA), BF16),
        compiler_params=_params("parallel", "arbitrary"),
        name="mla_prompt",
    )(q, k, v_t, k_meta, v_meta_t)


def _mla_sample_kernel(q_ref, c_ref, kr_ref, cc_ref, ckr_ref, wukv_ref, o_ref):
    t = q_ref.shape[0]
    q = q_ref[...]
    w = wukv_ref[...]
    qa = jnp.concatenate(
        [lax.dot_general(q[:, h * QK_PAD:h * QK_PAD + DN], w[:, h * (DN + DV_MLA):h * (DN + DV_MLA) + DN],
                         NT_DIMS, preferred_element_type=F32) for h in range(H_MLA)], axis=0).astype(BF16)
    qr = jnp.concatenate([q[:, h * QK_PAD + DN:(h + 1) * QK_PAD] for h in range(H_MLA)], axis=0)
    cache_c = cc_ref[...].astype(BF16)
    own_c = c_ref[...].astype(BF16)
    s_cache = (lax.dot_general(qa, cache_c, NT_DIMS, preferred_element_type=F32)
               + lax.dot_general(qr[:, :DR], ckr_ref[...].astype(BF16), NT_DIMS, preferred_element_type=F32))
    s_own = (lax.dot_general(qa, own_c, NT_DIMS, preferred_element_type=F32)
             + lax.dot_general(qr, kr_ref[...].astype(BF16), NT_DIMS, preferred_element_type=F32))
    m = jnp.maximum(jnp.max(s_cache, axis=-1, keepdims=True), jnp.max(s_own, axis=-1, keepdims=True))
    p_cache = jnp.exp(s_cache - m)
    p_own = jnp.exp(s_own - m)
    l = jnp.sum(p_cache, axis=-1, keepdims=True) + jnp.sum(p_own, axis=-1, keepdims=True)
    pc = (jnp.dot(p_cache.astype(BF16), cache_c, preferred_element_type=F32)
          + jnp.dot(p_own.astype(BF16), own_c, preferred_element_type=F32)) / l
    for h in range(H_MLA):
        lo = h * (DN + DV_MLA) + DN
        o_ref[:, h * DV_MLA:(h + 1) * DV_MLA] = _bdot(pc[h * t:(h + 1) * t], w[:, lo:lo + DV_MLA]).astype(o_ref.dtype)


def mla_sample(q, c, kr, cache_c, cache_kr, w_ukv):
    b, t, _ = q.shape
    p = cache_c.shape[1]
    per_b = lambda rows, width: pl.BlockSpec((None, rows, width), lambda bi: (bi, 0, 0))
    return pl.pallas_call(
        _mla_sample_kernel,
        grid=(b,),
        in_specs=[per_b(t, H_MLA * QK_PAD), per_b(t, KV_LORA), per_b(t, LANES), per_b(p, KV_LORA), per_b(p, DR),
                  _const_spec(w_ukv.shape)],
        out_specs=per_b(t, H_MLA * DV_MLA),
        out_shape=jax.ShapeDtypeStruct((b, t, H_MLA * DV_MLA), BF16),
        compiler_params=_params("parallel"),
        name="mla_sample",
    )(q, c, kr, cache_c, cache_kr, w_ukv)


def _rope_tables(pos):
    inv = ROPE_BASE ** (-jnp.arange(0, DR, 2, dtype=F32) / DR)
    ang = pos.astype(F32)[:, None] * inv[None, :]
    cos, sin = jnp.cos(ang), jnp.sin(ang)
    pad = jnp.zeros((pos.shape[0], LANES - DR), F32)
    return jnp.concatenate([cos, cos, pad], axis=1), jnp.concatenate([-sin, sin, pad], axis=1)


def _ff_chunks(wg, wu, wd):
    nf = D_FF // FF_CHUNK
    col = lambda m: m.astype(BF16).reshape(D_MODEL, nf, FF_CHUNK).transpose(1, 0, 2)
    return col(wg), col(wu), wd.astype(BF16).reshape(nf, FF_CHUNK, D_MODEL)


def _prepare_weights(ffn1_norm, ffn1_wg, ffn1_wu, ffn1_wd, mix_norm, w_in, conv_w, a_log, dt_bias, gdn_norm,
                     q_norm, kv_norm, w_uq, w_ukv, w_out, ffn2_norm, ffn2_wg, ffn2_wu, ffn2_wd, final_norm):
    w = {}
    w['g1'], w['gm'], w['g2'] = ffn1_norm[0][None], mix_norm[0][None], ffn2_norm[0][None]
    w['gf'] = final_norm[None]
    w['wg1'], w['wu1'], w['wd1'] = _ff_chunks(ffn1_wg[0], ffn1_wu[0], ffn1_wd[0])
    w['wg2'], w['wu2'], w['wd2'] = _ff_chunks(ffn2_wg[0], ffn2_wu[0], ffn2_wd[0])
    o_z = CONV_DIM
    o_a = o_z + H_GDN * GDN_DV
    o_b = o_a + H_GDN
    o_cq = o_b + H_GDN
    o_kr = o_cq + Q_LORA + KV_LORA
    wi = w_in[0]
    w['w_qkv'] = wi[:, :o_z].astype(BF16)
    w['w_z'] = wi[:, o_z:o_a].astype(BF16)
    tail_pad = jnp.zeros((D_MODEL, LANES - DR - 2 * H_GDN), wi.dtype)
    w['w_mla'] = jnp.concatenate([wi[:, o_cq:o_kr], wi[:, o_kr:o_kr + DR], wi[:, o_a:o_cq], tail_pad], axis=1).astype(BF16)
    w['conv_w'] = conv_w[0]
    lane_vec = lambda v: jnp.zeros((1, LANES), F32).at[0, AB_LANE:AB_LANE + H_GDN].set(v.astype(F32))
    w['alog'], w['dtb'] = lane_vec(a_log[0]), lane_vec(dt_bias[0])
    w['gn'] = gdn_norm[0][None]
    w['gq'], w['gkv'] = q_norm[0][None], kv_norm[0][None]
    uq = w_uq[0].reshape(Q_LORA, H_MLA, DN + DR)
    uq = jnp.concatenate([uq, jnp.zeros((Q_LORA, H_MLA, QK_PAD - DN - DR), uq.dtype)], axis=-1)
    w['w_uq'] = uq.reshape(Q_LORA, H_MLA * QK_PAD).astype(BF16)
    w['w_ukv'] = w_ukv[0].astype(BF16)
    w['w_out_g'] = w_out[0][:H_GDN * GDN_DV].astype(BF16)
    w['w_out_m'] = w_out[0][H_GDN * GDN_DV:].astype(BF16)
    return w


def kernel(x_prompt, x_sample, cache_mla_ckv, cache_mla_krope, state_gdn, state_conv, meta, ffn1_norm, ffn1_wg,
           ffn1_wu, ffn1_wd, mix_norm, w_in, conv_w, a_log, dt_bias, gdn_norm, q_norm, kv_norm, w_uq, w_ukv, w_out,
           ffn2_norm, ffn2_wg, ffn2_wu, ffn2_wd, final_norm):
    assert ffn1_wg.shape[0] == 1, "one layer: the meta rows are not carried past the mixer"
    bsz, s_len, _ = x_prompt.shape
    dbs, d_seq, _ = x_sample.shape
    past = cache_mla_ckv.shape[2]
    w = _prepare_weights(ffn1_norm, ffn1_wg, ffn1_wu, ffn1_wd, mix_norm, w_in, conv_w, a_log, dt_bias, gdn_norm,
                         q_norm, kv_norm, w_uq, w_ukv, w_out, ffn2_norm, ffn2_wg, ffn2_wu, ffn2_wd, final_norm)

    x1_p, qkv_p, z_p, mla_p = ffn_proj(x_prompt.reshape(bsz * s_len, D_MODEL), w)
    x1_s, qkv_s, z_s, mla_s = ffn_proj(x_sample.reshape(dbs * d_seq, D_MODEL), w)
    _, qkv_m, _, mla_m = ffn_proj(meta.astype(F32), w)

    front = CHUNK - N_META
    meta_qkv = jnp.pad(qkv_m, ((front, 0), (0, 0)))
    meta_ab = jnp.pad(mla_m[:, MLA_IN - LANES:], ((front, 0), (0, 0)))
    qkv_p3 = qkv_p.reshape(bsz, s_len, CONV_DIM)
    qkv_s3 = qkv_s.reshape(dbs, d_seq, CONV_DIM)
    gdn_p, m_p = gdn(qkv_p3, z_p.reshape(bsz, s_len, -1), mla_p.reshape(bsz, s_len, MLA_IN), meta_qkv, meta_ab,
                     jnp.zeros((bsz, H_GDN, GDN_DK, GDN_DV), F32), jnp.zeros((bsz, SUBLANES, CONV_DIM), F32), w,
                     has_meta=True)
    conv0_s = jnp.pad(state_conv[0].astype(F32), ((0, 0), (SUBLANES - (CONV_W - 1), 0), (0, 0)))
    gdn_s, m_s = gdn(qkv_s3, z_s.reshape(dbs, d_seq, -1), mla_s.reshape(dbs, d_seq, MLA_IN), jnp.zeros_like(meta_qkv),
                     jnp.zeros_like(meta_ab), state_gdn[0].astype(F32), conv0_s, w, has_meta=False)

    cos_m, sin_m = _rope_tables(jnp.arange(N_META))
    cos_p, sin_p = _rope_tables(N_META + jnp.arange(s_len))
    cos_s, sin_s = _rope_tables(past + jnp.arange(d_seq))
    q_p, c_p, kr_p, k_p, vt_p = mla_prep(mla_p, cos_p, sin_p, w, expand=True)
    _, c_m, kr_m, k_m, vt_m = mla_prep(mla_m, cos_m, sin_m, w, expand=True)
    q_s, c_s, kr_s = mla_prep(mla_s, cos_s, sin_s, w, expand=False)
    mla_o_p = mla_prompt(q_p.reshape(bsz, s_len, -1), k_p.reshape(bsz, s_len, -1), vt_p, k_m, vt_m)
    mla_o_s = mla_sample(q_s.reshape(dbs, d_seq, -1), c_s.reshape(dbs, d_seq, -1), kr_s.reshape(dbs, d_seq, -1),
                         cache_mla_ckv[0].astype(F32), cache_mla_krope[0].astype(F32), w['w_ukv'])

    y_p = out_ffn(x1_p, gdn_p.reshape(bsz * s_len, -1), mla_o_p.reshape(bsz * s_len, -1), w)
    y_s = out_ffn(x1_s, gdn_s.reshape(dbs * d_seq, -1), mla_o_s.reshape(dbs * d_seq, -1), w)

    with_meta = lambda m_rows, rows: jnp.concatenate(
        [jnp.broadcast_to(m_rows[None], (bsz,) + m_rows.shape), rows.reshape(bsz, s_len, -1)], axis=1)
    return (y_p.reshape(bsz, s_len, D_MODEL), y_s.reshape(dbs, d_seq, D_MODEL),
            with_meta(c_m, c_p)[None], with_meta(kr_m[:, :DR], kr_p[:, :DR])[None],
            m_p[None], qkv_p3[:, s_len - (CONV_W - 1):][None],
            c_s.reshape(dbs, d_seq, KV_LORA)[None], kr_s[:, :DR].reshape(dbs, d_seq, DR)[None],
            m_s[None], qkv_s3[:, d_seq - (CONV_W - 1):][None])
```

```python
import functools

import jax
import jax.numpy as jnp
from jax import lax
from jax.experimental import pallas as pl
from jax.experimental.pallas import tpu as pltpu

F32 = jnp.float32
BF16 = jnp.bfloat16

D_MODEL = 1024
CHUNK = 64
N_META = 16
H_GDN = 4
GDN_DK = 128
GDN_DV = 128
CONV_W = 4
CONV_DIM = H_GDN * (2 * GDN_DK + GDN_DV)
H_MLA = 4
Q_LORA = 384
KV_LORA = 256
DN = 128
DR = 64
DV_MLA = 128
ROPE_BASE = 10000.0
SM_SCALE = (DN + DR) ** -0.5
LOG2_E = 1.4426950408889634
Q_SCALE = SM_SCALE * LOG2_E
D_FF = 2816
EPS = 1e-6
L2_EPS = 1e-6

LANES = 128
SUBLANES = 8
FF_CHUNK = 256
MLA_IN = Q_LORA + KV_LORA + LANES
QK_PAD = 2 * LANES
AB_LANE = DR
VMEM_LIMIT = 56 * 1024 * 1024

NT_DIMS = (((1,), (1,)), ((), ()))


def _rms(x, g):
    return x * lax.rsqrt(jnp.mean(x * x, axis=-1, keepdims=True) + EPS) * g


def _bdot(a, b):
    return jnp.dot(a.astype(BF16), b.astype(BF16), preferred_element_type=F32)


def _bdot_nt(a, b):
    return lax.dot_general(a.astype(BF16), b.astype(BF16), NT_DIMS, preferred_element_type=F32)


def _const_spec(shape):
    nd = len(shape)
    return pl.BlockSpec(shape, lambda *_: (0,) * nd, pipeline_mode=pl.Buffered(1))


def _params(*sem):
    return pltpu.CompilerParams(dimension_semantics=sem, vmem_limit_bytes=VMEM_LIMIT)


def _swiglu_half(x, g_ref, wg_ref, wu_ref, wd_ref, h_ref, acc_ref):
    h_ref[...] = _rms(x, g_ref[...]).astype(BF16)
    acc_ref[...] = jnp.zeros_like(acc_ref)

    def body(f, carry):
        h = h_ref[...]
        gate = jnp.dot(h, wg_ref[f], preferred_element_type=F32)
        up = jnp.dot(h, wu_ref[f], preferred_element_type=F32)
        act = (jax.nn.silu(gate) * up).astype(BF16)
        acc_ref[...] += jnp.dot(act, wd_ref[f], preferred_element_type=F32)
        return carry

    lax.fori_loop(0, wg_ref.shape[0], body, 0)
    return x + 0.5 * acc_ref[...]


def _ffn_proj_kernel(x_ref, g1_ref, wg_ref, wu_ref, wd_ref, gm_ref, wqkv_ref, wz_ref, wmla_ref,
                     x1_ref, qkv_ref, z_ref, mla_ref, h_ref, acc_ref):
    x1 = _swiglu_half(x_ref[...], g1_ref, wg_ref, wu_ref, wd_ref, h_ref, acc_ref)
    x1_ref[...] = x1
    h_ref[...] = _rms(x1, gm_ref[...]).astype(BF16)
    qkv_ref[...] = jnp.dot(h_ref[...], wqkv_ref[...], preferred_element_type=F32)
    z_ref[...] = jnp.dot(h_ref[...], wz_ref[...], preferred_element_type=F32)
    mla_ref[...] = jnp.dot(h_ref[...], wmla_ref[...], preferred_element_type=F32)


def _row_tile(n, want):
    t = min(want, n)
    assert n % t == 0, (n, t)
    return t


def ffn_proj(x, w, tm=512):
    n = x.shape[0]
    tm = _row_tile(n, tm)
    row = lambda width: pl.BlockSpec((tm, width), lambda i: (i, 0))
    consts = (w['g1'], w['wg1'], w['wu1'], w['wd1'], w['gm'], w['w_qkv'], w['w_z'], w['w_mla'])
    return pl.pallas_call(
        _ffn_proj_kernel,
        grid=(n // tm,),
        in_specs=[row(D_MODEL)] + [_const_spec(c.shape) for c in consts],
        out_specs=[row(D_MODEL), row(CONV_DIM), row(H_GDN * GDN_DV), row(MLA_IN)],
        out_shape=[jax.ShapeDtypeStruct((n, D_MODEL), F32), jax.ShapeDtypeStruct((n, CONV_DIM), F32),
                   jax.ShapeDtypeStruct((n, H_GDN * GDN_DV), F32), jax.ShapeDtypeStruct((n, MLA_IN), F32)],
        scratch_shapes=[pltpu.VMEM((tm, D_MODEL), BF16), pltpu.VMEM((tm, D_MODEL), F32)],
        compiler_params=_params("parallel"),
        name="ffn_proj",
    )(x, *consts)


def _out_ffn_kernel(x1_ref, gdn_ref, mla_ref, wog_ref, wom_ref, g2_ref, wg_ref, wu_ref, wd_ref, gf_ref,
                    y_ref, h_ref, acc_ref):
    x2 = (x1_ref[...] + jnp.dot(gdn_ref[...], wog_ref[...], preferred_element_type=F32)
          + jnp.dot(mla_ref[...], wom_ref[...], preferred_element_type=F32))
    x3 = _swiglu_half(x2, g2_ref, wg_ref, wu_ref, wd_ref, h_ref, acc_ref)
    y_ref[...] = _rms(x3, gf_ref[...])


def out_ffn(x1, gdn, mla, w, tm=512):
    n = x1.shape[0]
    tm = _row_tile(n, tm)
    row = lambda width: pl.BlockSpec((tm, width), lambda i: (i, 0))
    consts = (w['w_out_g'], w['w_out_m'], w['g2'], w['wg2'], w['wu2'], w['wd2'], w['gf'])
    return pl.pallas_call(
        _out_ffn_kernel,
        grid=(n // tm,),
        in_specs=[row(D_MODEL), row(H_GDN * GDN_DV), row(H_MLA * DV_MLA)] + [_const_spec(c.shape) for c in consts],
        out_specs=row(D_MODEL),
        out_shape=jax.ShapeDtypeStruct((n, D_MODEL), F32),
        scratch_shapes=[pltpu.VMEM((tm, D_MODEL), BF16), pltpu.VMEM((tm, D_MODEL), F32)],
        compiler_params=_params("parallel"),
        name="out_ffn",
    )(x1, gdn, mla, *consts)


def _cumsum_rows(x):
    n = x.shape[0]
    row = lax.broadcasted_iota(jnp.int32, x.shape, 0)
    shift = 1
    while shift < n:
        x = x + jnp.where(row >= shift, pltpu.roll(x, shift, 0), 0.0)
        shift *= 2
    return x


def _transpose_rows(x):
    length = x.shape[0]
    sq = jnp.concatenate([x, jnp.zeros((LANES - length, LANES), x.dtype)], axis=0)
    return sq.T[:, :length]


def _gdn_kernel(qkv_ref, z_ref, ab_ref, meta_qkv_ref, meta_ab_ref, m0_ref, conv0_ref, convw_ref,
                alog_ref, dtb_ref, gn_ref, o_ref, mout_ref, conv_scr, m_scr, *, has_meta):
    c = pl.program_id(1)
    bg = qkv_ref.shape[0]
    length = qkv_ref.shape[1]
    hist = SUBLANES

    @pl.when(c == 0)
    def _():
        m_scr[...] = m0_ref[...]
        conv_scr[:, 0:hist, :] = conv0_ref[...]

    row = lax.broadcasted_iota(jnp.int32, (length, LANES), 0)
    ri = lax.broadcasted_iota(jnp.int32, (length, length), 0)
    ci = lax.broadcasted_iota(jnp.int32, (length, length), 1)
    causal = ci <= ri
    strict = ci < ri

    def per_batch(i, carry):
        x = qkv_ref[i]
        ab = ab_ref[i]
        if has_meta:
            is_meta = c == 0
            x = jnp.where(is_meta, meta_qkv_ref[...], x)
            ab = jnp.where(is_meta, meta_ab_ref[...], ab)
        conv_scr[i, hist:hist + length, :] = x
        u = conv_scr[i, pl.ds(hist - (CONV_W - 1), length), :] * convw_ref[0:1, :]
        for j in range(1, CONV_W):
            u = u + conv_scr[i, pl.ds(hist - (CONV_W - 1) + j, length), :] * convw_ref[j:j + 1, :]
        conv_scr[i, 0:hist, :] = conv_scr[i, length:length + hist, :]
        u = jax.nn.silu(u)

        g_all = -jnp.exp(alog_ref[...]) * jax.nn.softplus(ab + dtb_ref[...])
        beta_all = jax.nn.sigmoid(ab)
        if has_meta:
            valid = jnp.logical_or(c > 0, row >= length - N_META)
            g_all = jnp.where(valid, g_all, 0.0)
            beta_all = jnp.where(valid, beta_all, 0.0)
        gc_all = _cumsum_rows(g_all)
        gc_t = _transpose_rows(gc_all)

        chains = []
        for h in range(H_GDN):
            q = u[:, h * GDN_DK:(h + 1) * GDN_DK]
            k = u[:, (H_GDN + h) * GDN_DK:(H_GDN + h + 1) * GDN_DK]
            v = u[:, 2 * H_GDN * GDN_DK + h * GDN_DV:2 * H_GDN * GDN_DK + (h + 1) * GDN_DV]
            q = q * lax.rsqrt(jnp.sum(q * q, axis=-1, keepdims=True) + L2_EPS) * (GDN_DK ** -0.5)
            k = k * lax.rsqrt(jnp.sum(k * k, axis=-1, keepdims=True) + L2_EPS)
            gc = gc_all[:, AB_LANE + h:AB_LANE + h + 1]
            beta = beta_all[:, AB_LANE + H_GDN + h:AB_LANE + H_GDN + h + 1]
            gc_row = gc_t[AB_LANE + h:AB_LANE + h + 1, :]
            decay = jnp.where(causal, jnp.exp(jnp.where(causal, gc - gc_row, 0.0)), 0.0)
            kq = jnp.concatenate([k, q], axis=0).astype(BF16)
            chains.append(dict(i=i, h=h, k=k, v=v, gc=gc, beta=beta, decay=decay, kq=kq))
        return chains

    chains = [ch for i in range(bg) for ch in per_batch(i, 0)]
    for ch in chains:
        ch['kk_qk'] = lax.dot_general(ch['kq'], ch['k'].astype(BF16), NT_DIMS, preferred_element_type=F32)
    for ch in chains:
        ch['a'] = jnp.where(strict, ch['beta'] * ch['kk_qk'][:length] * ch['decay'], 0.0)
        ch['qk_decay'] = ch['kk_qk'][length:] * ch['decay']
        ch['y'] = -ch['a']
    for ch in chains:
        ch['pw'] = _bdot(ch['a'], ch['a'])
    span = 2
    while span < length:
        span *= 2
        for ch in chains:
            ch['y'] = ch['y'] + ch['pw'] + _bdot(ch['y'], ch['pw'])
        if span < length:
            for ch in chains:
                ch['pw'] = _bdot(ch['pw'], ch['pw'])
    for ch in chains:
        ch['m0'] = m_scr[ch['i'], ch['h']]
        ch['kqm'] = jnp.dot(ch['kq'], ch['m0'].astype(BF16), preferred_element_type=F32)
    for ch in chains:
        ch['eg'] = jnp.exp(ch['gc'])
        rhs = ch['beta'] * (ch['v'] - ch['eg'] * ch['kqm'][:length])
        ch['uu'] = rhs + _bdot(ch['y'], rhs)
    for ch in chains:
        g_last = ch['gc'][length - 1:length, :]
        k_dec = ch['k'] * jnp.exp(g_last - ch['gc'])
        m_scr[ch['i'], ch['h']] = jnp.exp(g_last) * ch['m0'] + _bdot(_transpose_rows(k_dec), ch['uu'])
    for ch in chains:
        o = ch['eg'] * ch['kqm'][length:] + _bdot(ch['qk_decay'], ch['uu'])
        n = _rms(o, gn_ref[...])
        cols = slice(ch['h'] * GDN_DV, (ch['h'] + 1) * GDN_DV)
        o_ref[ch['i'], :, cols] = (n * jax.nn.silu(z_ref[ch['i'], :, cols])).astype(o_ref.dtype)

    @pl.when(c == pl.num_programs(1) - 1)
    def _():
        mout_ref[...] = m_scr[...]


def gdn(qkv, z, mla_in, meta_qkv, meta_ab, m0, conv0, w, *, has_meta, bg=4):
    b, t, _ = qkv.shape
    assert b % bg == 0 and t % CHUNK == 0
    nblk = t // CHUNK + (1 if has_meta else 0)
    if has_meta:
        blk = lambda g, c: (g, jnp.maximum(c - 1, 0), 0)
    else:
        blk = lambda g, c: (g, c, 0)
    ab_blk = (lambda g, c: blk(g, c)[:2] + (MLA_IN // LANES - 1,))
    const2 = lambda shape: pl.BlockSpec(shape, lambda g, c: (0, 0))
    kern = functools.partial(_gdn_kernel, has_meta=has_meta)
    return pl.pallas_call(
        kern,
        grid=(b // bg, nblk),
        in_specs=[
            pl.BlockSpec((bg, CHUNK, CONV_DIM), blk),
            pl.BlockSpec((bg, CHUNK, H_GDN * GDN_DV), blk),
            pl.BlockSpec((bg, CHUNK, LANES), ab_blk),
            const2((CHUNK, CONV_DIM)),
            const2((CHUNK, LANES)),
            pl.BlockSpec((bg, H_GDN, GDN_DK, GDN_DV), lambda g, c: (g, 0, 0, 0)),
            pl.BlockSpec((bg, SUBLANES, CONV_DIM), lambda g, c: (g, 0, 0)),
            const2((CONV_W, CONV_DIM)),
            const2((1, LANES)),
            const2((1, LANES)),
            const2((1, GDN_DV)),
        ],
        out_specs=[
            pl.BlockSpec((bg, CHUNK, H_GDN * GDN_DV), blk),
            pl.BlockSpec((bg, H_GDN, GDN_DK, GDN_DV), lambda g, c: (g, 0, 0, 0)),
        ],
        out_shape=[jax.ShapeDtypeStruct((b, t, H_GDN * GDN_DV), BF16),
                   jax.ShapeDtypeStruct((b, H_GDN, GDN_DK, GDN_DV), F32)],
        scratch_shapes=[pltpu.VMEM((bg, SUBLANES + CHUNK, CONV_DIM), F32),
                        pltpu.VMEM((bg, H_GDN, GDN_DK, GDN_DV), F32)],
        compiler_params=_params("parallel", "arbitrary"),
        name="gdn_meta" if has_meta else "gdn",
    )(qkv, z, mla_in, meta_qkv, meta_ab, m0, conv0, w['conv_w'], w['alog'], w['dtb'], w['gn'])


def _rope(t, cos, sin):
    lane = lax.broadcasted_iota(jnp.int32, t.shape, 1)
    half = DR // 2
    swapped = jnp.where(lane < half, pltpu.roll(t, LANES - half, 1), pltpu.roll(t, half, 1))
    return t * cos + swapped * sin


def _mla_prep_kernel(x_ref, cos_ref, sin_ref, gq_ref, gkv_ref, wuq_ref, wukv_ref, *out_refs, expand):
    q_ref, c_ref, kr_ref = out_refs[:3]
    x = x_ref[...]
    cos = cos_ref[...]
    sin = sin_ref[...]
    q = _bdot(_rms(x[:, :Q_LORA], gq_ref[...]), wuq_ref[...])
    for h in range(H_MLA):
        lo = h * QK_PAD
        q_ref[:, lo:lo + DN] = (q[:, lo:lo + DN] * Q_SCALE).astype(BF16)
        q_ref[:, lo + DN:lo + QK_PAD] = (_rope(q[:, lo + DN:lo + QK_PAD], cos, sin) * Q_SCALE).astype(BF16)
    c = _rms(x[:, Q_LORA:Q_LORA + KV_LORA], gkv_ref[...])
    c_ref[...] = c
    kr = _rope(x[:, Q_LORA + KV_LORA:], cos, sin)
    kr_ref[...] = kr
    if expand:
        k_ref, vt_ref = out_refs[3:]
        kv = _bdot(c, wukv_ref[...])
        for h in range(H_MLA):
            lo = h * (DN + DV_MLA)
            k_ref[:, h * QK_PAD:h * QK_PAD + DN] = kv[:, lo:lo + DN].astype(BF16)
            k_ref[:, h * QK_PAD + DN:(h + 1) * QK_PAD] = kr.astype(BF16)
            v = kv[:, lo + DN:lo + DN + DV_MLA]
            v_t = v.T if v.shape[0] % LANES == 0 else _transpose_rows(v)
            vt_ref[h * DV_MLA:(h + 1) * DV_MLA, :] = v_t.astype(BF16)


def mla_prep(mla_in, cos, sin, w, *, expand, tm=512):
    n = mla_in.shape[0]
    tm = _row_tile(min(n, cos.shape[0]), tm)
    nrep = cos.shape[0] // tm
    row = lambda width: pl.BlockSpec((tm, width), lambda i: (i, 0))
    tab = pl.BlockSpec((tm, LANES), lambda i: (i % nrep, 0))
    consts = (w['gq'], w['gkv'], w['w_uq'], w['w_ukv'])
    out_specs = [row(H_MLA * QK_PAD), row(KV_LORA), row(LANES)]
    out_shape = [jax.ShapeDtypeStruct((n, H_MLA * QK_PAD), BF16), jax.ShapeDtypeStruct((n, KV_LORA), F32),
                 jax.ShapeDtypeStruct((n, LANES), F32)]
    if expand:
        out_specs += [row(H_MLA * QK_PAD), pl.BlockSpec((H_MLA * DV_MLA, tm), lambda i: (0, i))]
        out_shape += [jax.ShapeDtypeStruct((n, H_MLA * QK_PAD), BF16), jax.ShapeDtypeStruct((H_MLA * DV_MLA, n), BF16)]
    return pl.pallas_call(
        functools.partial(_mla_prep_kernel, expand=expand),
        grid=(n // tm,),
        in_specs=[row(MLA_IN), tab, tab] + [_const_spec(c.shape) for c in consts],
        out_specs=out_specs,
        out_shape=out_shape,
        compiler_params=_params("parallel"),
        name="mla_prep_kv" if expand else "mla_prep",
    )(mla_in, cos, sin, *consts)


def _mla_prompt_kernel(q_ref, k_ref, vt_ref, km_ref, vmt_ref, o_ref, s_scr, m_scr, l_scr, acc_scr):
    qi = pl.program_id(1)
    tq = q_ref.shape[0]
    heads = range(H_MLA)

    def scores(off, h):
        return lax.dot_general(k_ref[pl.ds(off, tq), h * QK_PAD:(h + 1) * QK_PAD],
                               q_ref[:, h * QK_PAD:(h + 1) * QK_PAD], NT_DIMS, preferred_element_type=F32)

    def consume(h, off, masked):
        s_t = s_scr[h]
        if masked:
            key_chunk = lax.broadcasted_iota(jnp.int32, (tq, tq), 0) // CHUNK
            qry_chunk = lax.broadcasted_iota(jnp.int32, (tq, tq), 1) // CHUNK
            s_t = jnp.where(key_chunk <= qry_chunk, s_t, -jnp.inf)
        m = m_scr[h]
        m_new = jnp.maximum(m, jnp.max(s_t, axis=0, keepdims=True))
        alpha = jnp.exp2(m - m_new)
        p_t = jnp.exp2(s_t - m_new)
        m_scr[h] = m_new
        l_scr[h] = alpha * l_scr[h] + jnp.sum(p_t, axis=0, keepdims=True)
        acc_scr[h] = alpha * acc_scr[h] + jnp.dot(vt_ref[h * DV_MLA:(h + 1) * DV_MLA, pl.ds(off, tq)],
                                                  p_t.astype(BF16), preferred_element_type=F32)

    for h in heads:
        s_t = lax.dot_general(km_ref[:, h * QK_PAD:(h + 1) * QK_PAD], q_ref[:, h * QK_PAD:(h + 1) * QK_PAD],
                              NT_DIMS, preferred_element_type=F32)
        m = jnp.max(s_t, axis=0, keepdims=True)
        p_t = jnp.exp2(s_t - m)
        m_scr[h] = m
        l_scr[h] = jnp.sum(p_t, axis=0, keepdims=True)
        acc_scr[h] = jnp.dot(vmt_ref[h * DV_MLA:(h + 1) * DV_MLA, :], p_t.astype(BF16), preferred_element_type=F32)
        s_scr[h] = scores(0, h)

    def full_tile(t, carry):
        off = pl.multiple_of(t * tq, tq)
        nxt = pl.multiple_of((t + 1) * tq, tq)
        for h in heads:
            s_next = scores(nxt, h)
            consume(h, off, False)
            s_scr[h] = s_next
        return carry

    lax.fori_loop(0, qi, full_tile, 0)
    for h in heads:
        consume(h, pl.multiple_of(qi * tq, tq), True)
    for h in heads:
        o_ref[:, h * DV_MLA:(h + 1) * DV_MLA] = (acc_scr[h] / l_scr[h]).T.astype(o_ref.dtype)


def mla_prompt(q, k, v_t, k_meta, v_meta_t, tq=512):
    b, s, _ = q.shape
    tq = _row_tile(s, tq)
    assert tq % LANES == 0
    whole = lambda shape: pl.BlockSpec(shape, lambda bi, i: (0, 0))
    return pl.pallas_call(
        _mla_prompt_kernel,
        grid=(b, s // tq),
        in_specs=[
            pl.BlockSpec((None, tq, H_MLA * QK_PAD), lambda bi, i: (bi, i, 0)),
            pl.BlockSpec((None, s, H_MLA * QK_PAD), lambda bi, i: (bi, 0, 0), pipeline_mode=pl.Buffered(1)),
            pl.BlockSpec((H_MLA * DV_MLA, s), lambda bi, i: (0, bi), pipeline_mode=pl.Buffered(1)),
            whole(k_meta.shape),
            whole(v_meta_t.shape),
        ],
        out_specs=pl.BlockSpec((None, tq, H_MLA * DV_MLA), lambda bi, i: (bi, i, 0)),
        out_shape=jax.ShapeDtypeStruct((b, s, H_MLA * DV_MLA), BF16),
        scratch_shapes=[pltpu.VMEM((H_MLA, tq, tq), F32), pltpu.VMEM((H_MLA, 1, tq), F32),
                        pltpu.VMEM((H_MLA, 1, tq), F32), pltpu.VMEM((H_MLA, DV_MLA, tq), F32)],
        compiler_params=_params("parallel", "arbitrary"),
        name="mla_prompt",
    )(q, k, v_t, k_meta, v_meta_t)


def _mla_sample_kernel(q_ref, c_ref, kr_ref, cc_ref, ckr_ref, wukv_ref, o_ref):
    t = q_ref.shape[0]
    q = q_ref[...]
    w = wukv_ref[...]
    qa = jnp.concatenate(
        [lax.dot_general(q[:, h * QK_PAD:h * QK_PAD + DN], w[:, h * (DN + DV_MLA):h * (DN + DV_MLA) + DN],
                         NT_DIMS, preferred_element_type=F32) for h in range(H_MLA)], axis=0).astype(BF16)
    qr = jnp.concatenate([q[:, h * QK_PAD + DN:(h + 1) * QK_PAD] for h in range(H_MLA)], axis=0)
    cache_c = cc_ref[...].astype(BF16)
    own_c = c_ref[...].astype(BF16)
    s_cache = (lax.dot_general(qa, cache_c, NT_DIMS, preferred_element_type=F32)
               + lax.dot_general(qr[:, :DR], ckr_ref[...].astype(BF16), NT_DIMS, preferred_element_type=F32))
    s_own = (lax.dot_general(qa, own_c, NT_DIMS, preferred_element_type=F32)
             + lax.dot_general(qr, kr_ref[...].astype(BF16), NT_DIMS, preferred_element_type=F32))
    m = jnp.maximum(jnp.max(s_cache, axis=-1, keepdims=True), jnp.max(s_own, axis=-1, keepdims=True))
    p_cache = jnp.exp2(s_cache - m)
    p_own = jnp.exp2(s_own - m)
    l = jnp.sum(p_cache, axis=-1, keepdims=True) + jnp.sum(p_own, axis=-1, keepdims=True)
    pc = (jnp.dot(p_cache.astype(BF16), cache_c, preferred_element_type=F32)
          + jnp.dot(p_own.astype(BF16), own_c, preferred_element_type=F32)) / l
    for h in range(H_MLA):
        lo = h * (DN + DV_MLA) + DN
        o_ref[:, h * DV_MLA:(h + 1) * DV_MLA] = _bdot(pc[h * t:(h + 1) * t], w[:, lo:lo + DV_MLA]).astype(o_ref.dtype)


def mla_sample(q, c, kr, cache_c, cache_kr, w_ukv):
    b, t, _ = q.shape
    p = cache_c.shape[1]
    per_b = lambda rows, width: pl.BlockSpec((None, rows, width), lambda bi: (bi, 0, 0))
    return pl.pallas_call(
        _mla_sample_kernel,
        grid=(b,),
        in_specs=[per_b(t, H_MLA * QK_PAD), per_b(t, KV_LORA), per_b(t, LANES), per_b(p, KV_LORA), per_b(p, DR),
                  _const_spec(w_ukv.shape)],
        out_specs=per_b(t, H_MLA * DV_MLA),
        out_shape=jax.ShapeDtypeStruct((b, t, H_MLA * DV_MLA), BF16),
        compiler_params=_params("parallel"),
        name="mla_sample",
    )(q, c, kr, cache_c, cache_kr, w_ukv)


def _rope_tables(pos):
    inv = ROPE_BASE ** (-jnp.arange(0, DR, 2, dtype=F32) / DR)
    ang = pos.astype(F32)[:, None] * inv[None, :]
    cos, sin = jnp.cos(ang), jnp.sin(ang)
    pad = jnp.zeros((pos.shape[0], LANES - DR), F32)
    return jnp.concatenate([cos, cos, pad], axis=1), jnp.concatenate([-sin, sin, pad], axis=1)


def _ff_chunks(wg, wu, wd):
    nf = D_FF // FF_CHUNK
    col = lambda m: m.astype(BF16).reshape(D_MODEL, nf, FF_CHUNK).transpose(1, 0, 2)
    return col(wg), col(wu), wd.astype(BF16).reshape(nf, FF_CHUNK, D_MODEL)


def _prepare_weights(ffn1_norm, ffn1_wg, ffn1_wu, ffn1_wd, mix_norm, w_in, conv_w, a_log, dt_bias, gdn_norm,
                     q_norm, kv_norm, w_uq, w_ukv, w_out, ffn2_norm, ffn2_wg, ffn2_wu, ffn2_wd, final_norm):
    w = {}
    w['g1'], w['gm'], w['g2'] = ffn1_norm[0][None], mix_norm[0][None], ffn2_norm[0][None]
    w['gf'] = final_norm[None]
    w['wg1'], w['wu1'], w['wd1'] = _ff_chunks(ffn1_wg[0], ffn1_wu[0], ffn1_wd[0])
    w['wg2'], w['wu2'], w['wd2'] = _ff_chunks(ffn2_wg[0], ffn2_wu[0], ffn2_wd[0])
    o_z = CONV_DIM
    o_a = o_z + H_GDN * GDN_DV
    o_b = o_a + H_GDN
    o_cq = o_b + H_GDN
    o_kr = o_cq + Q_LORA + KV_LORA
    wi = w_in[0]
    w['w_qkv'] = wi[:, :o_z].astype(BF16)
    w['w_z'] = wi[:, o_z:o_a].astype(BF16)
    tail_pad = jnp.zeros((D_MODEL, LANES - DR - 2 * H_GDN), wi.dtype)
    w['w_mla'] = jnp.concatenate([wi[:, o_cq:o_kr], wi[:, o_kr:o_kr + DR], wi[:, o_a:o_cq], tail_pad], axis=1).astype(BF16)
    w['conv_w'] = conv_w[0]
    lane_vec = lambda v: jnp.zeros((1, LANES), F32).at[0, AB_LANE:AB_LANE + H_GDN].set(v.astype(F32))
    w['alog'], w['dtb'] = lane_vec(a_log[0]), lane_vec(dt_bias[0])
    w['gn'] = gdn_norm[0][None]
    w['gq'], w['gkv'] = q_norm[0][None], kv_norm[0][None]
    uq = w_uq[0].reshape(Q_LORA, H_MLA, DN + DR)
    uq = jnp.concatenate([uq, jnp.zeros((Q_LORA, H_MLA, QK_PAD - DN - DR), uq.dtype)], axis=-1)
    w['w_uq'] = uq.reshape(Q_LORA, H_MLA * QK_PAD).astype(BF16)
    w['w_ukv'] = w_ukv[0].astype(BF16)
    w['w_out_g'] = w_out[0][:H_GDN * GDN_DV].astype(BF16)
    w['w_out_m'] = w_out[0][H_GDN * GDN_DV:].astype(BF16)
    return w


def kernel(x_prompt, x_sample, cache_mla_ckv, cache_mla_krope, state_gdn, state_conv, meta, ffn1_norm, ffn1_wg,
           ffn1_wu, ffn1_wd, mix_norm, w_in, conv_w, a_log, dt_bias, gdn_norm, q_norm, kv_norm, w_uq, w_ukv, w_out,
           ffn2_norm, ffn2_wg, ffn2_wu, ffn2_wd, final_norm):
    assert ffn1_wg.shape[0] == 1, "one layer: the meta rows are not carried past the mixer"
    bsz, s_len, _ = x_prompt.shape
    dbs, d_seq, _ = x_sample.shape
    past = cache_mla_ckv.shape[2]
    w = _prepare_weights(ffn1_norm, ffn1_wg, ffn1_wu, ffn1_wd, mix_norm, w_in, conv_w, a_log, dt_bias, gdn_norm,
                         q_norm, kv_norm, w_uq, w_ukv, w_out, ffn2_norm, ffn2_wg, ffn2_wu, ffn2_wd, final_norm)

    x1_p, qkv_p, z_p, mla_p = ffn_proj(x_prompt.reshape(bsz * s_len, D_MODEL), w)
    x1_s, qkv_s, z_s, mla_s = ffn_proj(x_sample.reshape(dbs * d_seq, D_MODEL), w)
    _, qkv_m, _, mla_m = ffn_proj(meta.astype(F32), w)

    front = CHUNK - N_META
    meta_qkv = jnp.pad(qkv_m, ((front, 0), (0, 0)))
    meta_ab = jnp.pad(mla_m[:, MLA_IN - LANES:], ((front, 0), (0, 0)))
    qkv_p3 = qkv_p.reshape(bsz, s_len, CONV_DIM)
    qkv_s3 = qkv_s.reshape(dbs, d_seq, CONV_DIM)
    gdn_p, m_p = gdn(qkv_p3, z_p.reshape(bsz, s_len, -1), mla_p.reshape(bsz, s_len, MLA_IN), meta_qkv, meta_ab,
                     jnp.zeros((bsz, H_GDN, GDN_DK, GDN_DV), F32), jnp.zeros((bsz, SUBLANES, CONV_DIM), F32), w,
                     has_meta=True)
    conv0_s = jnp.pad(state_conv[0].astype(F32), ((0, 0), (SUBLANES - (CONV_W - 1), 0), (0, 0)))
    gdn_s, m_s = gdn(qkv_s3, z_s.reshape(dbs, d_seq, -1), mla_s.reshape(dbs, d_seq, MLA_IN), jnp.zeros_like(meta_qkv),
                     jnp.zeros_like(meta_ab), state_gdn[0].astype(F32), conv0_s, w, has_meta=False)

    cos_m, sin_m = _rope_tables(jnp.arange(N_META))
    cos_p, sin_p = _rope_tables(N_META + jnp.arange(s_len))
    cos_s, sin_s = _rope_tables(past + jnp.arange(d_seq))
    q_p, c_p, kr_p, k_p, vt_p = mla_prep(mla_p, cos_p, sin_p, w, expand=True)
    _, c_m, kr_m, k_m, vt_m = mla_prep(mla_m, cos_m, sin_m, w, expand=True)
    q_s, c_s, kr_s = mla_prep(mla_s, cos_s, sin_s, w, expand=False)
    mla_o_p = mla_prompt(q_p.reshape(bsz, s_len, -1), k_p.reshape(bsz, s_len, -1), vt_p, k_m, vt_m)
    mla_o_s = mla_sample(q_s.reshape(dbs, d_seq, -1), c_s.reshape(dbs, d_seq, -1), kr_s.reshape(dbs, d_seq, -1),
                         cache_mla_ckv[0].astype(F32), cache_mla_krope[0].astype(F32), w['w_ukv'])

    y_p = out_ffn(x1_p, gdn_p.reshape(bsz * s_len, -1), mla_o_p.reshape(bsz * s_len, -1), w)
    y_s = out_ffn(x1_s, gdn_s.reshape(dbs * d_seq, -1), mla_o_s.reshape(dbs * d_seq, -1), w)

    with_meta = lambda m_rows, rows: jnp.concatenate(
        [jnp.broadcast_to(m_rows[None], (bsz,) + m_rows.shape), rows.reshape(bsz, s_len, -1)], axis=1)
    return (y_p.reshape(bsz, s_len, D_MODEL), y_s.reshape(dbs, d_seq, D_MODEL),
            with_meta(c_m, c_p)[None], with_meta(kr_m[:, :DR], kr_p[:, :DR])[None],
            m_p[None], qkv_p3[:, s_len - (CONV_W - 1):][None],
            c_s.reshape(dbs, d_seq, KV_LORA)[None], kr_s[:, :DR].reshape(dbs, d_seq, DR)[None],
            m_s[None], qkv_s3[:, d_seq - (CONV_W - 1):][None])
```

```python
import functools

import jax
import jax.numpy as jnp
from jax import lax
from jax.experimental import pallas as pl
from jax.experimental.pallas import tpu as pltpu

F32 = jnp.float32
BF16 = jnp.bfloat16

D_MODEL = 1024
CHUNK = 64
N_META = 16
H_GDN = 4
GDN_DK = 128
GDN_DV = 128
CONV_W = 4
CONV_DIM = H_GDN * (2 * GDN_DK + GDN_DV)
H_MLA = 4
Q_LORA = 384
KV_LORA = 256
DN = 128
DR = 64
DV_MLA = 128
ROPE_BASE = 10000.0
SM_SCALE = (DN + DR) ** -0.5
LOG2_E = 1.4426950408889634
Q_SCALE = SM_SCALE * LOG2_E
D_FF = 2816
EPS = 1e-6
L2_EPS = 1e-6

LANES = 128
SUBLANES = 8
FF_CHUNK = 256
MLA_IN = Q_LORA + KV_LORA + LANES
QK_PAD = 2 * LANES
AB_LANE = DR
VMEM_LIMIT = 56 * 1024 * 1024

NT_DIMS = (((1,), (1,)), ((), ()))


def _rms(x, g):
    return x * lax.rsqrt(jnp.mean(x * x, axis=-1, keepdims=True) + EPS) * g


def _bdot(a, b):
    return jnp.dot(a.astype(BF16), b.astype(BF16), preferred_element_type=F32)


def _bdot_nt(a, b):
    return lax.dot_general(a.astype(BF16), b.astype(BF16), NT_DIMS, preferred_element_type=F32)


def _const_spec(shape):
    nd = len(shape)
    return pl.BlockSpec(shape, lambda *_: (0,) * nd, pipeline_mode=pl.Buffered(1))


def _params(*sem):
    return pltpu.CompilerParams(dimension_semantics=sem, vmem_limit_bytes=VMEM_LIMIT)


def _swiglu_half(x, g_ref, wg_ref, wu_ref, wd_ref, h_ref, acc_ref):
    h_ref[...] = _rms(x, g_ref[...]).astype(BF16)
    nf = wg_ref.shape[1] // FF_CHUNK

    def gate_up(f):
        cols = slice(f * FF_CHUNK, (f + 1) * FF_CHUNK)
        gate = jnp.dot(h_ref[...], wg_ref[:, cols], preferred_element_type=F32)
        up = jnp.dot(h_ref[...], wu_ref[:, cols], preferred_element_type=F32)
        return gate, up

    nxt = gate_up(0)
    for f in range(nf):
        gate, up = nxt
        if f + 1 < nf:
            nxt = gate_up(f + 1)
        act = (jax.nn.silu(gate) * up).astype(BF16)
        down = jnp.dot(act, wd_ref[f * FF_CHUNK:(f + 1) * FF_CHUNK, :], preferred_element_type=F32)
        if f == 0:
            acc_ref[...] = down
        else:
            acc_ref[...] += down
    return x + 0.5 * acc_ref[...]


def _ffn_proj_kernel(x_ref, g1_ref, wg_ref, wu_ref, wd_ref, gm_ref, wqkv_ref, wz_ref, wmla_ref,
                     x1_ref, qkv_ref, z_ref, mla_ref, h_ref, acc_ref):
    x1 = _swiglu_half(x_ref[...], g1_ref, wg_ref, wu_ref, wd_ref, h_ref, acc_ref)
    x1_ref[...] = x1
    h_ref[...] = _rms(x1, gm_ref[...]).astype(BF16)
    qkv_ref[...] = jnp.dot(h_ref[...], wqkv_ref[...], preferred_element_type=F32)
    z_ref[...] = jnp.dot(h_ref[...], wz_ref[...], preferred_element_type=F32)
    mla_ref[...] = jnp.dot(h_ref[...], wmla_ref[...], preferred_element_type=F32)


def _row_tile(n, want):
    t = min(want, n)
    assert n % t == 0, (n, t)
    return t


def ffn_proj(x, w, tm=512):
    n = x.shape[0]
    tm = _row_tile(n, tm)
    row = lambda width: pl.BlockSpec((tm, width), lambda i: (i, 0))
    consts = (w['g1'], w['wg1'], w['wu1'], w['wd1'], w['gm'], w['w_qkv'], w['w_z'], w['w_mla'])
    return pl.pallas_call(
        _ffn_proj_kernel,
        grid=(n // tm,),
        in_specs=[row(D_MODEL)] + [_const_spec(c.shape) for c in consts],
        out_specs=[row(D_MODEL), row(CONV_DIM), row(H_GDN * GDN_DV), row(MLA_IN)],
        out_shape=[jax.ShapeDtypeStruct((n, D_MODEL), F32), jax.ShapeDtypeStruct((n, CONV_DIM), F32),
                   jax.ShapeDtypeStruct((n, H_GDN * GDN_DV), F32), jax.ShapeDtypeStruct((n, MLA_IN), F32)],
        scratch_shapes=[pltpu.VMEM((tm, D_MODEL), BF16), pltpu.VMEM((tm, D_MODEL), F32)],
        compiler_params=_params("parallel"),
        name="ffn_proj",
    )(x, *consts)


def _out_ffn_kernel(x1_ref, gdn_ref, mla_ref, wog_ref, wom_ref, g2_ref, wg_ref, wu_ref, wd_ref, gf_ref,
                    y_ref, h_ref, acc_ref):
    x2 = (x1_ref[...] + jnp.dot(gdn_ref[...], wog_ref[...], preferred_element_type=F32)
          + jnp.dot(mla_ref[...], wom_ref[...], preferred_element_type=F32))
    x3 = _swiglu_half(x2, g2_ref, wg_ref, wu_ref, wd_ref, h_ref, acc_ref)
    y_ref[...] = _rms(x3, gf_ref[...])


def out_ffn(x1, gdn, mla, w, tm=512):
    n = x1.shape[0]
    tm = _row_tile(n, tm)
    row = lambda width: pl.BlockSpec((tm, width), lambda i: (i, 0))
    consts = (w['w_out_g'], w['w_out_m'], w['g2'], w['wg2'], w['wu2'], w['wd2'], w['gf'])
    return pl.pallas_call(
        _out_ffn_kernel,
        grid=(n // tm,),
        in_specs=[row(D_MODEL), row(H_GDN * GDN_DV), row(H_MLA * DV_MLA)] + [_const_spec(c.shape) for c in consts],
        out_specs=row(D_MODEL),
        out_shape=jax.ShapeDtypeStruct((n, D_MODEL), F32),
        scratch_shapes=[pltpu.VMEM((tm, D_MODEL), BF16), pltpu.VMEM((tm, D_MODEL), F32)],
        compiler_params=_params("parallel"),
        name="out_ffn",
    )(x1, gdn, mla, *consts)


def _cumsum_rows(x):
    n = x.shape[0]
    row = lax.broadcasted_iota(jnp.int32, x.shape, 0)
    shift = 1
    while shift < n:
        x = x + jnp.where(row >= shift, pltpu.roll(x, shift, 0), 0.0)
        shift *= 2
    return x


def _transpose_rows(x):
    length = x.shape[0]
    sq = jnp.concatenate([x, jnp.zeros((LANES - length, LANES), x.dtype)], axis=0)
    return sq.T[:, :length]


def _gdn_kernel(qkv_ref, z_ref, ab_ref, meta_qkv_ref, meta_ab_ref, m0_ref, conv0_ref, convw_ref,
                alog_ref, dtb_ref, gn_ref, o_ref, mout_ref, conv_scr, m_scr, *, has_meta):
    c = pl.program_id(1)
    bg = qkv_ref.shape[0]
    length = qkv_ref.shape[1]
    hist = SUBLANES

    @pl.when(c == 0)
    def _():
        m_scr[...] = m0_ref[...]
        conv_scr[:, 0:hist, :] = conv0_ref[...]

    row = lax.broadcasted_iota(jnp.int32, (length, LANES), 0)
    ri = lax.broadcasted_iota(jnp.int32, (length, length), 0)
    ci = lax.broadcasted_iota(jnp.int32, (length, length), 1)
    causal = ci <= ri
    strict = ci < ri

    def per_batch(i, carry):
        x = qkv_ref[i]
        ab = ab_ref[i]
        if has_meta:
            is_meta = c == 0
            x = jnp.where(is_meta, meta_qkv_ref[...], x)
            ab = jnp.where(is_meta, meta_ab_ref[...], ab)
        conv_scr[i, hist:hist + length, :] = x
        u = conv_scr[i, pl.ds(hist - (CONV_W - 1), length), :] * convw_ref[0:1, :]
        for j in range(1, CONV_W):
            u = u + conv_scr[i, pl.ds(hist - (CONV_W - 1) + j, length), :] * convw_ref[j:j + 1, :]
        conv_scr[i, 0:hist, :] = conv_scr[i, length:length + hist, :]
        u = jax.nn.silu(u)

        g_all = -jnp.exp(alog_ref[...]) * jax.nn.softplus(ab + dtb_ref[...])
        beta_all = jax.nn.sigmoid(ab)
        if has_meta:
            valid = jnp.logical_or(c > 0, row >= length - N_META)
            g_all = jnp.where(valid, g_all, 0.0)
            beta_all = jnp.where(valid, beta_all, 0.0)
        gc_all = _cumsum_rows(g_all)
        gc_t = _transpose_rows(gc_all)

        chains = []
        for h in range(H_GDN):
            q = u[:, h * GDN_DK:(h + 1) * GDN_DK]
            k = u[:, (H_GDN + h) * GDN_DK:(H_GDN + h + 1) * GDN_DK]
            v = u[:, 2 * H_GDN * GDN_DK + h * GDN_DV:2 * H_GDN * GDN_DK + (h + 1) * GDN_DV]
            q = q * lax.rsqrt(jnp.sum(q * q, axis=-1, keepdims=True) + L2_EPS) * (GDN_DK ** -0.5)
            k = k * lax.rsqrt(jnp.sum(k * k, axis=-1, keepdims=True) + L2_EPS)
            gc = gc_all[:, AB_LANE + h:AB_LANE + h + 1]
            beta = beta_all[:, AB_LANE + H_GDN + h:AB_LANE + H_GDN + h + 1]
            gc_row = gc_t[AB_LANE + h:AB_LANE + h + 1, :]
            decay = jnp.where(causal, jnp.exp(jnp.where(causal, gc - gc_row, 0.0)), 0.0)
            kq = jnp.concatenate([k, q], axis=0).astype(BF16)
            chains.append(dict(i=i, h=h, k=k, v=v, gc=gc, beta=beta, decay=decay, kq=kq))
        return chains

    chains = [ch for i in range(bg) for ch in per_batch(i, 0)]
    for ch in chains:
        ch['kk_qk'] = lax.dot_general(ch['kq'], ch['k'].astype(BF16), NT_DIMS, preferred_element_type=F32)
    for ch in chains:
        ch['a'] = jnp.where(strict, ch['beta'] * ch['kk_qk'][:length] * ch['decay'], 0.0)
        ch['qk_decay'] = ch['kk_qk'][length:] * ch['decay']
        ch['y'] = -ch['a']
    for ch in chains:
        ch['pw'] = _bdot(ch['a'], ch['a'])
    span = 2
    while span < length:
        span *= 2
        for ch in chains:
            ch['y'] = ch['y'] + ch['pw'] + _bdot(ch['y'], ch['pw'])
        if span < length:
            for ch in chains:
                ch['pw'] = _bdot(ch['pw'], ch['pw'])
    for ch in chains:
        ch['m0'] = m_scr[ch['i'], ch['h']]
        ch['kqm'] = jnp.dot(ch['kq'], ch['m0'].astype(BF16), preferred_element_type=F32)
    for ch in chains:
        ch['eg'] = jnp.exp(ch['gc'])
        rhs = ch['beta'] * (ch['v'] - ch['eg'] * ch['kqm'][:length])
        ch['uu'] = rhs + _bdot(ch['y'], rhs)
    for ch in chains:
        g_last = ch['gc'][length - 1:length, :]
        k_dec = ch['k'] * jnp.exp(g_last - ch['gc'])
        m_scr[ch['i'], ch['h']] = jnp.exp(g_last) * ch['m0'] + _bdot(_transpose_rows(k_dec), ch['uu'])
    for ch in chains:
        o = ch['eg'] * ch['kqm'][length:] + _bdot(ch['qk_decay'], ch['uu'])
        n = _rms(o, gn_ref[...])
        cols = slice(ch['h'] * GDN_DV, (ch['h'] + 1) * GDN_DV)
        o_ref[ch['i'], :, cols] = (n * jax.nn.silu(z_ref[ch['i'], :, cols])).astype(o_ref.dtype)

    @pl.when(c == pl.num_programs(1) - 1)
    def _():
        mout_ref[...] = m_scr[...]


def gdn(qkv, z, mla_in, meta_qkv, meta_ab, m0, conv0, w, *, has_meta, bg=4):
    b, t, _ = qkv.shape
    assert b % bg == 0 and t % CHUNK == 0
    nblk = t // CHUNK + (1 if has_meta else 0)
    if has_meta:
        blk = lambda g, c: (g, jnp.maximum(c - 1, 0), 0)
    else:
        blk = lambda g, c: (g, c, 0)
    ab_blk = (lambda g, c: blk(g, c)[:2] + (MLA_IN // LANES - 1,))
    const2 = lambda shape: pl.BlockSpec(shape, lambda g, c: (0, 0))
    kern = functools.partial(_gdn_kernel, has_meta=has_meta)
    return pl.pallas_call(
        kern,
        grid=(b // bg, nblk),
        in_specs=[
            pl.BlockSpec((bg, CHUNK, CONV_DIM), blk),
            pl.BlockSpec((bg, CHUNK, H_GDN * GDN_DV), blk),
            pl.BlockSpec((bg, CHUNK, LANES), ab_blk),
            const2((CHUNK, CONV_DIM)),
            const2((CHUNK, LANES)),
            pl.BlockSpec((bg, H_GDN, GDN_DK, GDN_DV), lambda g, c: (g, 0, 0, 0)),
            pl.BlockSpec((bg, SUBLANES, CONV_DIM), lambda g, c: (g, 0, 0)),
            const2((CONV_W, CONV_DIM)),
            const2((1, LANES)),
            const2((1, LANES)),
            const2((1, GDN_DV)),
        ],
        out_specs=[
            pl.BlockSpec((bg, CHUNK, H_GDN * GDN_DV), blk),
            pl.BlockSpec((bg, H_GDN, GDN_DK, GDN_DV), lambda g, c: (g, 0, 0, 0)),
        ],
        out_shape=[jax.ShapeDtypeStruct((b, t, H_GDN * GDN_DV), BF16),
                   jax.ShapeDtypeStruct((b, H_GDN, GDN_DK, GDN_DV), F32)],
        scratch_shapes=[pltpu.VMEM((bg, SUBLANES + CHUNK, CONV_DIM), F32),
                        pltpu.VMEM((bg, H_GDN, GDN_DK, GDN_DV), F32)],
        compiler_params=_params("parallel", "arbitrary"),
        name="gdn_meta" if has_meta else "gdn",
    )(qkv, z, mla_in, meta_qkv, meta_ab, m0, conv0, w['conv_w'], w['alog'], w['dtb'], w['gn'])


def _rope(t, cos, sin):
    lane = lax.broadcasted_iota(jnp.int32, t.shape, 1)
    half = DR // 2
    swapped = jnp.where(lane < half, pltpu.roll(t, LANES - half, 1), pltpu.roll(t, half, 1))
    return t * cos + swapped * sin


def _mla_prep_kernel(x_ref, cos_ref, sin_ref, gq_ref, gkv_ref, wuq_ref, wukv_ref, *out_refs, expand):
    q_ref, c_ref, kr_ref = out_refs[:3]
    x = x_ref[...]
    cos = cos_ref[...]
    sin = sin_ref[...]
    q = _bdot(_rms(x[:, :Q_LORA], gq_ref[...]), wuq_ref[...])
    for h in range(H_MLA):
        lo = h * QK_PAD
        q_ref[:, lo:lo + DN] = (q[:, lo:lo + DN] * Q_SCALE).astype(BF16)
        q_ref[:, lo + DN:lo + QK_PAD] = (_rope(q[:, lo + DN:lo + QK_PAD], cos, sin) * Q_SCALE).astype(BF16)
    c = _rms(x[:, Q_LORA:Q_LORA + KV_LORA], gkv_ref[...])
    c_ref[...] = c
    kr = _rope(x[:, Q_LORA + KV_LORA:], cos, sin)
    kr_ref[...] = kr
    if expand:
        k_ref, vt_ref = out_refs[3:]
        kv = _bdot(c, wukv_ref[...])
        for h in range(H_MLA):
            lo = h * (DN + DV_MLA)
            k_ref[:, h * QK_PAD:h * QK_PAD + DN] = kv[:, lo:lo + DN].astype(BF16)
            k_ref[:, h * QK_PAD + DN:(h + 1) * QK_PAD] = kr.astype(BF16)
            v = kv[:, lo + DN:lo + DN + DV_MLA]
            v_t = v.T if v.shape[0] % LANES == 0 else _transpose_rows(v)
            vt_ref[h * DV_MLA:(h + 1) * DV_MLA, :] = v_t.astype(BF16)


def mla_prep(mla_in, cos, sin, w, *, expand, tm=512):
    n = mla_in.shape[0]
    tm = _row_tile(min(n, cos.shape[0]), tm)
    nrep = cos.shape[0] // tm
    row = lambda width: pl.BlockSpec((tm, width), lambda i: (i, 0))
    tab = pl.BlockSpec((tm, LANES), lambda i: (i % nrep, 0))
    consts = (w['gq'], w['gkv'], w['w_uq'], w['w_ukv'])
    out_specs = [row(H_MLA * QK_PAD), row(KV_LORA), row(LANES)]
    out_shape = [jax.ShapeDtypeStruct((n, H_MLA * QK_PAD), BF16), jax.ShapeDtypeStruct((n, KV_LORA), F32),
                 jax.ShapeDtypeStruct((n, LANES), F32)]
    if expand:
        out_specs += [row(H_MLA * QK_PAD), pl.BlockSpec((H_MLA * DV_MLA, tm), lambda i: (0, i))]
        out_shape += [jax.ShapeDtypeStruct((n, H_MLA * QK_PAD), BF16), jax.ShapeDtypeStruct((H_MLA * DV_MLA, n), BF16)]
    return pl.pallas_call(
        functools.partial(_mla_prep_kernel, expand=expand),
        grid=(n // tm,),
        in_specs=[row(MLA_IN), tab, tab] + [_const_spec(c.shape) for c in consts],
        out_specs=out_specs,
        out_shape=out_shape,
        compiler_params=_params("parallel"),
        name="mla_prep_kv" if expand else "mla_prep",
    )(mla_in, cos, sin, *consts)


def _mla_prompt_kernel(q_ref, k_ref, vt_ref, km_ref, vmt_ref, o_ref, s_scr, m_scr, l_scr, acc_scr):
    qi = pl.program_id(1)
    tq = q_ref.shape[0]
    heads = range(H_MLA)

    def scores(off, h):
        return lax.dot_general(k_ref[pl.ds(off, tq), h * QK_PAD:(h + 1) * QK_PAD],
                               q_ref[:, h * QK_PAD:(h + 1) * QK_PAD], NT_DIMS, preferred_element_type=F32)

    def consume(h, off, masked):
        s_t = s_scr[h]
        if masked:
            key_chunk = lax.broadcasted_iota(jnp.int32, (tq, tq), 0) // CHUNK
            qry_chunk = lax.broadcasted_iota(jnp.int32, (tq, tq), 1) // CHUNK
            s_t = jnp.where(key_chunk <= qry_chunk, s_t, -jnp.inf)
        m = m_scr[h]
        m_new = jnp.maximum(m, jnp.max(s_t, axis=0, keepdims=True))
        alpha = jnp.exp2(m - m_new)
        p_t = jnp.exp2(s_t - m_new)
        m_scr[h] = m_new
        l_scr[h] = alpha * l_scr[h] + jnp.sum(p_t, axis=0, keepdims=True)
        acc_scr[h] = alpha * acc_scr[h] + jnp.dot(vt_ref[h * DV_MLA:(h + 1) * DV_MLA, pl.ds(off, tq)],
                                                  p_t.astype(BF16), preferred_element_type=F32)

    for h in heads:
        s_t = lax.dot_general(km_ref[:, h * QK_PAD:(h + 1) * QK_PAD], q_ref[:, h * QK_PAD:(h + 1) * QK_PAD],
                              NT_DIMS, preferred_element_type=F32)
        m = jnp.max(s_t, axis=0, keepdims=True)
        p_t = jnp.exp2(s_t - m)
        m_scr[h] = m
        l_scr[h] = jnp.sum(p_t, axis=0, keepdims=True)
        acc_scr[h] = jnp.dot(vmt_ref[h * DV_MLA:(h + 1) * DV_MLA, :], p_t.astype(BF16), preferred_element_type=F32)
        s_scr[h] = scores(0, h)

    def full_tile(t, carry):
        off = pl.multiple_of(t * tq, tq)
        nxt = pl.multiple_of((t + 1) * tq, tq)
        for h in heads:
            s_next = scores(nxt, h)
            consume(h, off, False)
            s_scr[h] = s_next
        return carry

    lax.fori_loop(0, qi, full_tile, 0)
    for h in heads:
        consume(h, pl.multiple_of(qi * tq, tq), True)
    for h in heads:
        o_ref[:, h * DV_MLA:(h + 1) * DV_MLA] = (acc_scr[h] / l_scr[h]).T.astype(o_ref.dtype)


def mla_prompt(q, k, v_t, k_meta, v_meta_t, tq=512):
    b, s, _ = q.shape
    tq = _row_tile(s, tq)
    assert tq % LANES == 0
    whole = lambda shape: pl.BlockSpec(shape, lambda bi, i: (0, 0))
    return pl.pallas_call(
        _mla_prompt_kernel,
        grid=(b, s // tq),
        in_specs=[
            pl.BlockSpec((None, tq, H_MLA * QK_PAD), lambda bi, i: (bi, i, 0)),
            pl.BlockSpec((None, s, H_MLA * QK_PAD), lambda bi, i: (bi, 0, 0), pipeline_mode=pl.Buffered(1)),
            pl.BlockSpec((H_MLA * DV_MLA, s), lambda bi, i: (0, bi), pipeline_mode=pl.Buffered(1)),
            whole(k_meta.shape),
            whole(v_meta_t.shape),
        ],
        out_specs=pl.BlockSpec((None, tq, H_MLA * DV_MLA), lambda bi, i: (bi, i, 0)),
        out_shape=jax.ShapeDtypeStruct((b, s, H_MLA * DV_MLA), BF16),
        scratch_shapes=[pltpu.VMEM((H_MLA, tq, tq), F32), pltpu.VMEM((H_MLA, 1, tq), F32),
                        pltpu.VMEM((H_MLA, 1, tq), F32), pltpu.VMEM((H_MLA, DV_MLA, tq), F32)],
        compiler_params=_params("parallel", "arbitrary"),
        name="mla_prompt",
    )(q, k, v_t, k_meta, v_meta_t)


def _mla_sample_kernel(q_ref, c_ref, kr_ref, cc_ref, ckr_ref, wukv_ref, o_ref):
    t = q_ref.shape[0]
    q = q_ref[...]
    w = wukv_ref[...]
    qa = jnp.concatenate(
        [lax.dot_general(q[:, h * QK_PAD:h * QK_PAD + DN], w[:, h * (DN + DV_MLA):h * (DN + DV_MLA) + DN],
                         NT_DIMS, preferred_element_type=F32) for h in range(H_MLA)], axis=0).astype(BF16)
    qr = jnp.concatenate([q[:, h * QK_PAD + DN:(h + 1) * QK_PAD] for h in range(H_MLA)], axis=0)
    cache_c = cc_ref[...].astype(BF16)
    own_c = c_ref[...].astype(BF16)
    s_cache = (lax.dot_general(qa, cache_c, NT_DIMS, preferred_element_type=F32)
               + lax.dot_general(qr[:, :DR], ckr_ref[...].astype(BF16), NT_DIMS, preferred_element_type=F32))
    s_own = (lax.dot_general(qa, own_c, NT_DIMS, preferred_element_type=F32)
             + lax.dot_general(qr, kr_ref[...].astype(BF16), NT_DIMS, preferred_element_type=F32))
    m = jnp.maximum(jnp.max(s_cache, axis=-1, keepdims=True), jnp.max(s_own, axis=-1, keepdims=True))
    p_cache = jnp.exp2(s_cache - m)
    p_own = jnp.exp2(s_own - m)
    l = jnp.sum(p_cache, axis=-1, keepdims=True) + jnp.sum(p_own, axis=-1, keepdims=True)
    pc = (jnp.dot(p_cache.astype(BF16), cache_c, preferred_element_type=F32)
          + jnp.dot(p_own.astype(BF16), own_c, preferred_element_type=F32)) / l
    for h in range(H_MLA):
        lo = h * (DN + DV_MLA) + DN
        o_ref[:, h * DV_MLA:(h + 1) * DV_MLA] = _bdot(pc[h * t:(h + 1) * t], w[:, lo:lo + DV_MLA]).astype(o_ref.dtype)


def mla_sample(q, c, kr, cache_c, cache_kr, w_ukv):
    b, t, _ = q.shape
    p = cache_c.shape[1]
    per_b = lambda rows, width: pl.BlockSpec((None, rows, width), lambda bi: (bi, 0, 0))
    return pl.pallas_call(
        _mla_sample_kernel,
        grid=(b,),
        in_specs=[per_b(t, H_MLA * QK_PAD), per_b(t, KV_LORA), per_b(t, LANES), per_b(p, KV_LORA), per_b(p, DR),
                  _const_spec(w_ukv.shape)],
        out_specs=per_b(t, H_MLA * DV_MLA),
        out_shape=jax.ShapeDtypeStruct((b, t, H_MLA * DV_MLA), BF16),
        compiler_params=_params("parallel"),
        name="mla_sample",
    )(q, c, kr, cache_c, cache_kr, w_ukv)


def _rope_tables(pos):
    inv = ROPE_BASE ** (-jnp.arange(0, DR, 2, dtype=F32) / DR)
    ang = pos.astype(F32)[:, None] * inv[None, :]
    cos, sin = jnp.cos(ang), jnp.sin(ang)
    pad = jnp.zeros((pos.shape[0], LANES - DR), F32)
    return jnp.concatenate([cos, cos, pad], axis=1), jnp.concatenate([-sin, sin, pad], axis=1)


def _prepare_weights(ffn1_norm, ffn1_wg, ffn1_wu, ffn1_wd, mix_norm, w_in, conv_w, a_log, dt_bias, gdn_norm,
                     q_norm, kv_norm, w_uq, w_ukv, w_out, ffn2_norm, ffn2_wg, ffn2_wu, ffn2_wd, final_norm):
    w = {}
    w['g1'], w['gm'], w['g2'] = ffn1_norm[0][None], mix_norm[0][None], ffn2_norm[0][None]
    w['gf'] = final_norm[None]
    assert D_FF % FF_CHUNK == 0
    w['wg1'], w['wu1'], w['wd1'] = (m[0].astype(BF16) for m in (ffn1_wg, ffn1_wu, ffn1_wd))
    w['wg2'], w['wu2'], w['wd2'] = (m[0].astype(BF16) for m in (ffn2_wg, ffn2_wu, ffn2_wd))
    o_z = CONV_DIM
    o_a = o_z + H_GDN * GDN_DV
    o_b = o_a + H_GDN
    o_cq = o_b + H_GDN
    o_kr = o_cq + Q_LORA + KV_LORA
    wi = w_in[0]
    w['w_qkv'] = wi[:, :o_z].astype(BF16)
    w['w_z'] = wi[:, o_z:o_a].astype(BF16)
    tail_pad = jnp.zeros((D_MODEL, LANES - DR - 2 * H_GDN), wi.dtype)
    w['w_mla'] = jnp.concatenate([wi[:, o_cq:o_kr], wi[:, o_kr:o_kr + DR], wi[:, o_a:o_cq], tail_pad], axis=1).astype(BF16)
    w['conv_w'] = conv_w[0]
    lane_vec = lambda v: jnp.zeros((1, LANES), F32).at[0, AB_LANE:AB_LANE + H_GDN].set(v.astype(F32))
    w['alog'], w['dtb'] = lane_vec(a_log[0]), lane_vec(dt_bias[0])
    w['gn'] = gdn_norm[0][None]
    w['gq'], w['gkv'] = q_norm[0][None], kv_norm[0][None]
    uq = w_uq[0].reshape(Q_LORA, H_MLA, DN + DR)
    uq = jnp.concatenate([uq, jnp.zeros((Q_LORA, H_MLA, QK_PAD - DN - DR), uq.dtype)], axis=-1)
    w['w_uq'] = uq.reshape(Q_LORA, H_MLA * QK_PAD).astype(BF16)
    w['w_ukv'] = w_ukv[0].astype(BF16)
    w['w_out_g'] = w_out[0][:H_GDN * GDN_DV].astype(BF16)
    w['w_out_m'] = w_out[0][H_GDN * GDN_DV:].astype(BF16)
    return w


def kernel(x_prompt, x_sample, cache_mla_ckv, cache_mla_krope, state_gdn, state_conv, meta, ffn1_norm, ffn1_wg,
           ffn1_wu, ffn1_wd, mix_norm, w_in, conv_w, a_log, dt_bias, gdn_norm, q_norm, kv_norm, w_uq, w_ukv, w_out,
           ffn2_norm, ffn2_wg, ffn2_wu, ffn2_wd, final_norm):
    assert ffn1_wg.shape[0] == 1, "one layer: the meta rows are not carried past the mixer"
    bsz, s_len, _ = x_prompt.shape
    dbs, d_seq, _ = x_sample.shape
    past = cache_mla_ckv.shape[2]
    w = _prepare_weights(ffn1_norm, ffn1_wg, ffn1_wu, ffn1_wd, mix_norm, w_in, conv_w, a_log, dt_bias, gdn_norm,
                         q_norm, kv_norm, w_uq, w_ukv, w_out, ffn2_norm, ffn2_wg, ffn2_wu, ffn2_wd, final_norm)

    x1_p, qkv_p, z_p, mla_p = ffn_proj(x_prompt.reshape(bsz * s_len, D_MODEL), w)
    x1_s, qkv_s, z_s, mla_s = ffn_proj(x_sample.reshape(dbs * d_seq, D_MODEL), w)
    _, qkv_m, _, mla_m = ffn_proj(meta.astype(F32), w)

    front = CHUNK - N_META
    meta_qkv = jnp.pad(qkv_m, ((front, 0), (0, 0)))
    meta_ab = jnp.pad(mla_m[:, MLA_IN - LANES:], ((front, 0), (0, 0)))
    qkv_p3 = qkv_p.reshape(bsz, s_len, CONV_DIM)
    qkv_s3 = qkv_s.reshape(dbs, d_seq, CONV_DIM)
    gdn_p, m_p = gdn(qkv_p3, z_p.reshape(bsz, s_len, -1), mla_p.reshape(bsz, s_len, MLA_IN), meta_qkv, meta_ab,
                     jnp.zeros((bsz, H_GDN, GDN_DK, GDN_DV), F32), jnp.zeros((bsz, SUBLANES, CONV_DIM), F32), w,
                     has_meta=True)
    conv0_s = jnp.pad(state_conv[0].astype(F32), ((0, 0), (SUBLANES - (CONV_W - 1), 0), (0, 0)))
    gdn_s, m_s = gdn(qkv_s3, z_s.reshape(dbs, d_seq, -1), mla_s.reshape(dbs, d_seq, MLA_IN), jnp.zeros_like(meta_qkv),
                     jnp.zeros_like(meta_ab), state_gdn[0].astype(F32), conv0_s, w, has_meta=False)

    cos_m, sin_m = _rope_tables(jnp.arange(N_META))
    cos_p, sin_p = _rope_tables(N_META + jnp.arange(s_len))
    cos_s, sin_s = _rope_tables(past + jnp.arange(d_seq))
    q_p, c_p, kr_p, k_p, vt_p = mla_prep(mla_p, cos_p, sin_p, w, expand=True)
    _, c_m, kr_m, k_m, vt_m = mla_prep(mla_m, cos_m, sin_m, w, expand=True)
    q_s, c_s, kr_s = mla_prep(mla_s, cos_s, sin_s, w, expand=False)
    mla_o_p = mla_prompt(q_p.reshape(bsz, s_len, -1), k_p.reshape(bsz, s_len, -1), vt_p, k_m, vt_m)
    mla_o_s = mla_sample(q_s.reshape(dbs, d_seq, -1), c_s.reshape(dbs, d_seq, -1), kr_s.reshape(dbs, d_seq, -1),
                         cache_mla_ckv[0].astype(F32), cache_mla_krope[0].astype(F32), w['w_ukv'])

    y_p = out_ffn(x1_p, gdn_p.reshape(bsz * s_len, -1), mla_o_p.reshape(bsz * s_len, -1), w)
    y_s = out_ffn(x1_s, gdn_s.reshape(dbs * d_seq, -1), mla_o_s.reshape(dbs * d_seq, -1), w)

    with_meta = lambda m_rows, rows: jnp.concatenate(
        [jnp.broadcast_to(m_rows[None], (bsz,) + m_rows.shape), rows.reshape(bsz, s_len, -1)], axis=1)
    return (y_p.reshape(bsz, s_len, D_MODEL), y_s.reshape(dbs, d_seq, D_MODEL),
            with_meta(c_m, c_p)[None], with_meta(kr_m[:, :DR], kr_p[:, :DR])[None],
            m_p[None], qkv_p3[:, s_len - (CONV_W - 1):][None],
            c_s.reshape(dbs, d_seq, KV_LORA)[None], kr_s[:, :DR].reshape(dbs, d_seq, DR)[None],
            m_s[None], qkv_s3[:, d_seq - (CONV_W - 1):][None])
```

```python
import functools

import jax
import jax.numpy as jnp
from jax import lax
from jax.experimental import pallas as pl
from jax.experimental.pallas import tpu as pltpu

F32 = jnp.float32
BF16 = jnp.bfloat16

D_MODEL = 1024
CHUNK = 64
N_META = 16
H_GDN = 4
GDN_DK = 128
GDN_DV = 128
CONV_W = 4
CONV_DIM = H_GDN * (2 * GDN_DK + GDN_DV)
H_MLA = 4
Q_LORA = 384
KV_LORA = 256
DN = 128
DR = 64
DV_MLA = 128
ROPE_BASE = 10000.0
SM_SCALE = (DN + DR) ** -0.5
LOG2_E = 1.4426950408889634
Q_SCALE = SM_SCALE * LOG2_E
D_FF = 2816
EPS = 1e-6
L2_EPS = 1e-6

LANES = 128
SUBLANES = 8
FF_CHUNK = 256
MLA_IN = Q_LORA + KV_LORA + LANES
QK_PAD = 2 * LANES
AB_LANE = DR
VMEM_LIMIT = 56 * 1024 * 1024

NT_DIMS = (((1,), (1,)), ((), ()))


def _rms(x, g):
    return x * lax.rsqrt(jnp.mean(x * x, axis=-1, keepdims=True) + EPS) * g


def _bdot(a, b):
    return jnp.dot(a.astype(BF16), b.astype(BF16), preferred_element_type=F32)


def _bdot_nt(a, b):
    return lax.dot_general(a.astype(BF16), b.astype(BF16), NT_DIMS, preferred_element_type=F32)


def _const_spec(shape):
    nd = len(shape)
    return pl.BlockSpec(shape, lambda *_: (0,) * nd, pipeline_mode=pl.Buffered(1))


def _params(*sem):
    return pltpu.CompilerParams(dimension_semantics=sem, vmem_limit_bytes=VMEM_LIMIT)


def _drain(*stage_gens):
    live = [[g, 0.0, float(t)] for g, t in (sg if isinstance(sg, tuple) else (sg, 1.0) for sg in stage_gens)]
    while live:
        entry = min(live, key=lambda e: e[1] / e[2])
        try:
            entry[1] += next(entry[0]) or 0.0
        except StopIteration:
            live.remove(entry)


def _swiglu_half_stages(x, g_ref, wg_ref, wu_ref, wd_ref, h_ref, acc_ref, result):
    h_ref[...] = _rms(x, g_ref[...]).astype(BF16)
    nf = wg_ref.shape[1] // FF_CHUNK

    def gate_up(f):
        cols = slice(f * FF_CHUNK, (f + 1) * FF_CHUNK)
        gate = jnp.dot(h_ref[...], wg_ref[:, cols], preferred_element_type=F32)
        up = jnp.dot(h_ref[...], wu_ref[:, cols], preferred_element_type=F32)
        return gate, up

    nxt = gate_up(0)
    yield 1.0
    for f in range(nf):
        gate, up = nxt
        if f + 1 < nf:
            nxt = gate_up(f + 1)
            yield 1.0
        act = (jax.nn.silu(gate) * up).astype(BF16)
        down = jnp.dot(act, wd_ref[f * FF_CHUNK:(f + 1) * FF_CHUNK, :], preferred_element_type=F32)
        if f == 0:
            acc_ref[...] = down
        else:
            acc_ref[...] += down
        yield 1.0
    result.append(x + 0.5 * acc_ref[...])


def _swiglu_half(x, g_ref, wg_ref, wu_ref, wd_ref, h_ref, acc_ref):
    result = []
    _drain(_swiglu_half_stages(x, g_ref, wg_ref, wu_ref, wd_ref, h_ref, acc_ref, result))
    return result[0]


def _ffn_proj_kernel(x_ref, g1_ref, wg_ref, wu_ref, wd_ref, gm_ref, wqkv_ref, wz_ref, wmla_ref, *rest,
                     stream_tiles):
    if not stream_tiles:
        x1_ref, qkv_ref, z_ref, mla_ref, h_ref, acc_ref = rest
        x1 = _swiglu_half(x_ref[...], g1_ref, wg_ref, wu_ref, wd_ref, h_ref, acc_ref)
    else:
        hist_ref, convw_ref, x1_ref, qkv_ref, z_ref, mla_ref, tail_ref, h_ref, acc_ref, conv_scr = rest
        i = pl.program_id(0)
        tm = x_ref.shape[0]

        @pl.when(i == 0)
        def _():
            conv_scr[...] = jnp.zeros_like(conv_scr)

        @pl.when((i + stream_tiles - 1) % stream_tiles == 0)
        def _():
            conv_scr[0, 0:GDN_HIST, :] = hist_ref[...]

        def feature_stages():
            for row0 in range(0, tm, CHUNK):
                qkv_ref[row0:row0 + CHUNK, :] = _gdn_features(conv_scr, convw_ref, 0, row0)
                yield 1.0
            conv_scr[0, 0:GDN_HIST, :] = conv_scr[0, tm:tm + GDN_HIST, :]

        result = []
        _drain((_swiglu_half_stages(x_ref[...], g1_ref, wg_ref, wu_ref, wd_ref, h_ref, acc_ref, result),
                2 * (D_FF // FF_CHUNK)), (feature_stages(), tm // CHUNK))
        x1 = result[0]
    x1_ref[...] = x1
    h_ref[...] = _rms(x1, gm_ref[...]).astype(BF16)
    qkv = jnp.dot(h_ref[...], wqkv_ref[...], preferred_element_type=F32)
    z_ref[...] = jnp.dot(h_ref[...], wz_ref[...], preferred_element_type=F32)
    mla_ref[...] = jnp.dot(h_ref[...], wmla_ref[...], preferred_element_type=F32)
    if stream_tiles:
        conv_scr[0, GDN_HIST:GDN_HIST + tm, :] = qkv
        tail_ref[0] = qkv[tm - GDN_HIST:, :]
    else:
        qkv_ref[...] = qkv


def _row_tile(n, want):
    t = min(want, n)
    assert n % t == 0, (n, t)
    return t


def ffn_proj(x, w, hist=None, stream_len=None, tm=512):
    n = x.shape[0]
    tm = _row_tile(n, tm)
    ntile = n // tm
    stream_tiles = 0
    if hist is not None:
        assert stream_len % tm == 0 and tm % CHUNK == 0
        stream_tiles = stream_len // tm
    tile = (lambda i: jnp.minimum(i, ntile - 1)) if stream_tiles else (lambda i: i)
    row = lambda width: pl.BlockSpec((tm, width), lambda i: (tile(i), 0))
    consts = (w['g1'], w['wg1'], w['wu1'], w['wd1'], w['gm'], w['w_qkv'], w['w_z'], w['w_mla'])
    out_specs = [row(D_MODEL), row(CONV_DIM), row(H_GDN * GDN_DV), row(MLA_IN)]
    out_shape = [jax.ShapeDtypeStruct((n, D_MODEL), F32), jax.ShapeDtypeStruct((n, CONV_DIM), F32),
                 jax.ShapeDtypeStruct((n, H_GDN * GDN_DV), F32), jax.ShapeDtypeStruct((n, MLA_IN), F32)]
    scratch = [pltpu.VMEM((tm, D_MODEL), BF16), pltpu.VMEM((tm, D_MODEL), F32)]
    if stream_tiles:
        consts += (hist, w['conv_w'])
        out_specs[1] = pl.BlockSpec((tm, CONV_DIM), lambda i: (jnp.maximum(i - 1, 0), 0))
        out_specs.append(pl.BlockSpec((1, GDN_HIST, CONV_DIM), lambda i: (tile(i), 0, 0)))
        out_shape.append(jax.ShapeDtypeStruct((ntile, GDN_HIST, CONV_DIM), F32))
        scratch.append(pltpu.VMEM((1, GDN_HIST + tm, CONV_DIM), F32))
    return pl.pallas_call(
        functools.partial(_ffn_proj_kernel, stream_tiles=stream_tiles),
        grid=(ntile + (1 if stream_tiles else 0),),
        in_specs=[row(D_MODEL)] + [_const_spec(c.shape) for c in consts],
        out_specs=out_specs,
        out_shape=out_shape,
        scratch_shapes=scratch,
        compiler_params=_params("arbitrary" if stream_tiles else "parallel"),
        name="ffn_proj",
    )(x, *consts)


def _out_ffn_kernel(x1_ref, gdn_ref, mla_ref, wog_ref, wom_ref, g2_ref, wg_ref, wu_ref, wd_ref, gf_ref,
                    y_ref, h_ref, acc_ref):
    x2 = (x1_ref[...] + jnp.dot(gdn_ref[...], wog_ref[...], preferred_element_type=F32)
          + jnp.dot(mla_ref[...], wom_ref[...], preferred_element_type=F32))
    x3 = _swiglu_half(x2, g2_ref, wg_ref, wu_ref, wd_ref, h_ref, acc_ref)
    y_ref[...] = _rms(x3, gf_ref[...])


def out_ffn(x1, gdn, mla, w, tm=512):
    n = x1.shape[0]
    tm = _row_tile(n, tm)
    row = lambda width: pl.BlockSpec((tm, width), lambda i: (i, 0))
    consts = (w['w_out_g'], w['w_out_m'], w['g2'], w['wg2'], w['wu2'], w['wd2'], w['gf'])
    return pl.pallas_call(
        _out_ffn_kernel,
        grid=(n // tm,),
        in_specs=[row(D_MODEL), row(H_GDN * GDN_DV), row(H_MLA * DV_MLA)] + [_const_spec(c.shape) for c in consts],
        out_specs=row(D_MODEL),
        out_shape=jax.ShapeDtypeStruct((n, D_MODEL), F32),
        scratch_shapes=[pltpu.VMEM((tm, D_MODEL), BF16), pltpu.VMEM((tm, D_MODEL), F32)],
        compiler_params=_params("parallel"),
        name="out_ffn",
    )(x1, gdn, mla, *consts)


def _cumsum_rows(x):
    n = x.shape[0]
    row = lax.broadcasted_iota(jnp.int32, x.shape, 0)
    shift = 1
    while shift < n:
        x = x + jnp.where(row >= shift, pltpu.roll(x, shift, 0), 0.0)
        shift *= 2
    return x


def _transpose_rows(x):
    length = x.shape[0]
    sq = jnp.concatenate([x, jnp.zeros((LANES - length, LANES), x.dtype)], axis=0)
    return sq.T[:, :length]


GDN_HIST = SUBLANES
GDN_PREP_COST = 4.0
GDN_PROMPT_ROWS = 2 * CHUNK


def _gdn_features(conv_scr, convw_ref, i, row0):
    base = GDN_HIST - (CONV_W - 1) + row0
    u = conv_scr[i, pl.ds(base, CHUNK), :] * convw_ref[0:1, :]
    for j in range(1, CONV_W):
        u = u + conv_scr[i, pl.ds(base + j, CHUNK), :] * convw_ref[j:j + 1, :]
    u = jax.nn.silu(u)
    parts = []
    for h in range(2 * H_GDN):
        x = u[:, h * GDN_DK:(h + 1) * GDN_DK]
        x = x * lax.rsqrt(jnp.sum(x * x, axis=-1, keepdims=True) + L2_EPS)
        parts.append(x * (GDN_DK ** -0.5) if h < H_GDN else x)
    parts.append(u[:, 2 * H_GDN * GDN_DK:])
    return jnp.concatenate(parts, axis=1)


def _gdn_block_stages(first_block, rows, pad_front, features, ab_ref, z_ref, write_o, m_scr,
                      alog_ref, dtb_ref, gn_ref):
    bg = ab_ref.shape[0]
    length = CHUNK
    row = lax.broadcasted_iota(jnp.int32, (length, LANES), 0)
    ri = lax.broadcasted_iota(jnp.int32, (length, length), 0)
    ci = lax.broadcasted_iota(jnp.int32, (length, length), 1)
    causal = ci <= ri
    strict = ci < ri

    def per_batch(i, row0):
        u = features(i, row0)
        ab = ab_ref[i, row0:row0 + length, :]
        g_all = -jnp.exp(alog_ref[...]) * jax.nn.softplus(ab + dtb_ref[...])
        beta_all = jax.nn.sigmoid(ab)
        if pad_front > row0:
            valid = jnp.logical_or(jnp.logical_not(first_block), row >= pad_front - row0)
            g_all = jnp.where(valid, g_all, 0.0)
            beta_all = jnp.where(valid, beta_all, 0.0)
        gc_all = _cumsum_rows(g_all)
        gc_t = _transpose_rows(gc_all)

        chains = []
        for h in range(H_GDN):
            q = u[:, h * GDN_DK:(h + 1) * GDN_DK]
            k = u[:, (H_GDN + h) * GDN_DK:(H_GDN + h + 1) * GDN_DK]
            v = u[:, 2 * H_GDN * GDN_DK + h * GDN_DV:2 * H_GDN * GDN_DK + (h + 1) * GDN_DV]
            gc = gc_all[:, AB_LANE + h:AB_LANE + h + 1]
            beta = beta_all[:, AB_LANE + H_GDN + h:AB_LANE + H_GDN + h + 1]
            gc_row = gc_t[AB_LANE + h:AB_LANE + h + 1, :]
            decay = jnp.where(causal, jnp.exp(jnp.where(causal, gc - gc_row, 0.0)), 0.0)
            kq = jnp.concatenate([k, q], axis=0).astype(BF16)
            chains.append(dict(i=i, h=h, row0=row0, k=k, v=v, gc=gc, beta=beta, decay=decay, kq=kq))
        return chains

    per_chunk = []
    for row0 in range(0, rows, length):
        per_chunk.append([])
        for i in range(bg):
            per_chunk[-1] += per_batch(i, row0)
            yield GDN_PREP_COST
    chains = [ch for chunk in per_chunk for ch in chunk]
    for ch in chains:
        ch['kk_qk'] = lax.dot_general(ch['kq'], ch['k'].astype(BF16), NT_DIMS, preferred_element_type=F32)
    yield 1.0
    for ch in chains:
        ch['a'] = jnp.where(strict, ch['beta'] * ch['kk_qk'][:length] * ch['decay'], 0.0)
        ch['qk_decay'] = ch['kk_qk'][length:] * ch['decay']
        ch['y'] = -ch['a']
    for ch in chains:
        ch['pw'] = _bdot(ch['a'], ch['a'])
    yield 1.0
    span = 2
    while span < length:
        span *= 2
        for ch in chains:
            ch['y'] = ch['y'] + ch['pw'] + _bdot(ch['y'], ch['pw'])
        yield 1.0
        if span < length:
            for ch in chains:
                ch['pw'] = _bdot(ch['pw'], ch['pw'])
            yield 1.0

    def emit_output(chunk):
        for ch in chunk:
            o = ch['eg'] * ch['kqm'][length:] + _bdot(ch['qk_decay'], ch['uu'])
            n = _rms(o, gn_ref[...])
            rws = slice(ch['row0'], ch['row0'] + length)
            cols = slice(ch['h'] * GDN_DV, (ch['h'] + 1) * GDN_DV)
            write_o(ch['i'], rws, cols, (n * jax.nn.silu(z_ref[ch['i'], rws, cols])).astype(BF16))

    pending = None
    for chunk in per_chunk:
        for ch in chunk:
            ch['m0'] = m_scr[ch['i'], ch['h']]
            ch['kqm'] = jnp.dot(ch['kq'], ch['m0'].astype(BF16), preferred_element_type=F32)
        yield 1.0
        if pending is not None:
            emit_output(pending)
            yield 1.0
        for ch in chunk:
            ch['eg'] = jnp.exp(ch['gc'])
            rhs = ch['beta'] * (ch['v'] - ch['eg'] * ch['kqm'][:length])
            ch['uu'] = rhs + _bdot(ch['y'], rhs)
        yield 1.0
        for ch in chunk:
            g_last = ch['gc'][length - 1:length, :]
            k_dec = ch['k'] * jnp.exp(g_last - ch['gc'])
            m_scr[ch['i'], ch['h']] = jnp.exp(g_last) * ch['m0'] + _bdot(_transpose_rows(k_dec), ch['uu'])
        yield 1.0
        pending = chunk
    emit_output(pending)
    yield 1.0


def _gdn_stage_block(qkv_ref, conv_scr):
    bg, rows, _ = qkv_ref.shape
    for i in range(bg):
        conv_scr[i, GDN_HIST:GDN_HIST + rows, :] = qkv_ref[i]


def _gdn_keep_history(conv_scr, rows):
    for i in range(conv_scr.shape[0]):
        conv_scr[i, 0:GDN_HIST, :] = conv_scr[i, rows:rows + GDN_HIST, :]


def _gdn_kernel(qkv_ref, z_ref, ab_ref, m0_ref, conv0_ref, convw_ref, alog_ref, dtb_ref, gn_ref,
                o_ref, mout_ref, conv_scr, m_scr, *, pad_front, prepared):
    c = pl.program_id(1)
    rows = qkv_ref.shape[1]

    @pl.when(c == 0)
    def _():
        m_scr[...] = m0_ref[...]
        conv_scr[:, 0:GDN_HIST, :] = conv0_ref[...]

    def write_o(i, rws, cols, val):
        o_ref[i, rws, cols] = val

    if prepared:
        features = lambda i, row0: qkv_ref[i, row0:row0 + CHUNK, :]
    else:
        _gdn_stage_block(qkv_ref, conv_scr)
        features = functools.partial(_gdn_features, conv_scr, convw_ref)
    _drain(_gdn_block_stages(c == 0, rows, pad_front, features, ab_ref, z_ref, write_o, m_scr,
                             alog_ref, dtb_ref, gn_ref))
    if not prepared:
        _gdn_keep_history(conv_scr, rows)

    @pl.when(c == pl.num_programs(1) - 1)
    def _():
        mout_ref[...] = m_scr[...]


def _gdn_in_specs(bg, rows, blk):
    nd = len(blk(0, 0))
    const = lambda shape: pl.BlockSpec(shape, lambda *_: (0,) * len(shape))
    return [
        pl.BlockSpec((bg, rows, CONV_DIM), blk),
        pl.BlockSpec((bg, rows, H_GDN * GDN_DV), blk),
        pl.BlockSpec((bg, rows, LANES), lambda *a: blk(*a)[:nd - 1] + (MLA_IN // LANES - 1,)),
        pl.BlockSpec((bg, H_GDN, GDN_DK, GDN_DV), lambda g, c: (g, 0, 0, 0)),
        pl.BlockSpec((bg, GDN_HIST, CONV_DIM), lambda g, c: (g, 0, 0)),
        const((CONV_W, CONV_DIM)),
        const((1, LANES)),
        const((1, LANES)),
        const((1, GDN_DV)),
    ]


def gdn(qkv, z, mla_in, m0, conv0, w, *, pad_front=0, bg=4, rows=CHUNK, prepared=False):
    b, t, _ = qkv.shape
    bg = min(bg, b)
    assert b % bg == 0 and t % rows == 0 and rows % CHUNK == 0
    blk = lambda g, c: (g, c, 0)
    return pl.pallas_call(
        functools.partial(_gdn_kernel, pad_front=pad_front, prepared=prepared),
        grid=(b // bg, t // rows),
        in_specs=_gdn_in_specs(bg, rows, blk),
        out_specs=[
            pl.BlockSpec((bg, rows, H_GDN * GDN_DV), blk),
            pl.BlockSpec((bg, H_GDN, GDN_DK, GDN_DV), lambda g, c: (g, 0, 0, 0)),
        ],
        out_shape=[jax.ShapeDtypeStruct((b, t, H_GDN * GDN_DV), BF16),
                   jax.ShapeDtypeStruct((b, H_GDN, GDN_DK, GDN_DV), F32)],
        scratch_shapes=[pltpu.VMEM((bg, GDN_HIST + (0 if prepared else rows), CONV_DIM), F32),
                        pltpu.VMEM((bg, H_GDN, GDN_DK, GDN_DV), F32)],
        compiler_params=_params("parallel", "arbitrary"),
        name="gdn",
    )(qkv, z, mla_in, m0, conv0, w['conv_w'], w['alog'], w['dtb'], w['gn'])


def _rope(t, cos, sin):
    lane = lax.broadcasted_iota(jnp.int32, t.shape, 1)
    half = DR // 2
    swapped = jnp.where(lane < half, pltpu.roll(t, LANES - half, 1), pltpu.roll(t, half, 1))
    return t * cos + swapped * sin


def _mla_prep_kernel(x_ref, cos_ref, sin_ref, gq_ref, gkv_ref, wuq_ref, wukv_ref, *out_refs, expand):
    q_ref, c_ref, kr_ref = out_refs[:3]
    x = x_ref[...]
    cos = cos_ref[...]
    sin = sin_ref[...]
    q = _bdot(_rms(x[:, :Q_LORA], gq_ref[...]), wuq_ref[...])
    for h in range(H_MLA):
        lo = h * QK_PAD
        q_ref[:, lo:lo + DN] = (q[:, lo:lo + DN] * Q_SCALE).astype(BF16)
        q_ref[:, lo + DN:lo + QK_PAD] = (_rope(q[:, lo + DN:lo + QK_PAD], cos, sin) * Q_SCALE).astype(BF16)
    c = _rms(x[:, Q_LORA:Q_LORA + KV_LORA], gkv_ref[...])
    c_ref[...] = c
    kr = _rope(x[:, Q_LORA + KV_LORA:], cos, sin)
    kr_ref[...] = kr
    if expand:
        k_ref, vt_ref = out_refs[3:]
        kv = _bdot(c, wukv_ref[...])
        for h in range(H_MLA):
            lo = h * (DN + DV_MLA)
            k_ref[:, h * QK_PAD:h * QK_PAD + DN] = kv[:, lo:lo + DN].astype(BF16)
            k_ref[:, h * QK_PAD + DN:(h + 1) * QK_PAD] = kr.astype(BF16)
            v = kv[:, lo + DN:lo + DN + DV_MLA]
            v_t = v.T if v.shape[0] % LANES == 0 else _transpose_rows(v)
            vt_ref[h * DV_MLA:(h + 1) * DV_MLA, :] = v_t.astype(BF16)


def mla_prep(mla_in, cos, sin, w, *, expand, tm=512):
    n = mla_in.shape[0]
    tm = _row_tile(min(n, cos.shape[0]), tm)
    nrep = cos.shape[0] // tm
    row = lambda width: pl.BlockSpec((tm, width), lambda i: (i, 0))
    tab = pl.BlockSpec((tm, LANES), lambda i: (i % nrep, 0))
    consts = (w['gq'], w['gkv'], w['w_uq'], w['w_ukv'])
    out_specs = [row(H_MLA * QK_PAD), row(KV_LORA), row(LANES)]
    out_shape = [jax.ShapeDtypeStruct((n, H_MLA * QK_PAD), BF16), jax.ShapeDtypeStruct((n, KV_LORA), F32),
                 jax.ShapeDtypeStruct((n, LANES), F32)]
    if expand:
        out_specs += [row(H_MLA * QK_PAD), pl.BlockSpec((H_MLA * DV_MLA, tm), lambda i: (0, i))]
        out_shape += [jax.ShapeDtypeStruct((n, H_MLA * QK_PAD), BF16), jax.ShapeDtypeStruct((H_MLA * DV_MLA, n), BF16)]
    return pl.pallas_call(
        functools.partial(_mla_prep_kernel, expand=expand),
        grid=(n // tm,),
        in_specs=[row(MLA_IN), tab, tab] + [_const_spec(c.shape) for c in consts],
        out_specs=out_specs,
        out_shape=out_shape,
        compiler_params=_params("parallel"),
        name="mla_prep_kv" if expand else "mla_prep",
    )(mla_in, cos, sin, *consts)


def _mla_prompt_kernel(q_ref, k_ref, vt_ref, km_ref, vmt_ref, o_ref, s_scr, m_scr, l_scr, acc_scr):
    qi = pl.program_id(1)
    tq = q_ref.shape[0]
    heads = range(H_MLA)

    def scores(off, h):
        return lax.dot_general(k_ref[pl.ds(off, tq), h * QK_PAD:(h + 1) * QK_PAD],
                               q_ref[:, h * QK_PAD:(h + 1) * QK_PAD], NT_DIMS, preferred_element_type=F32)

    def consume(h, off, masked):
        s_t = s_scr[h]
        if masked:
            key_chunk = lax.broadcasted_iota(jnp.int32, (tq, tq), 0) // CHUNK
            qry_chunk = lax.broadcasted_iota(jnp.int32, (tq, tq), 1) // CHUNK
            s_t = jnp.where(key_chunk <= qry_chunk, s_t, -jnp.inf)
        m = m_scr[h]
        m_new = jnp.maximum(m, jnp.max(s_t, axis=0, keepdims=True))
        alpha = jnp.exp2(m - m_new)
        p_t = jnp.exp2(s_t - m_new)
        m_scr[h] = m_new
        l_scr[h] = alpha * l_scr[h] + jnp.sum(p_t, axis=0, keepdims=True)
        acc_scr[h] = alpha * acc_scr[h] + jnp.dot(vt_ref[h * DV_MLA:(h + 1) * DV_MLA, pl.ds(off, tq)],
                                                  p_t.astype(BF16), preferred_element_type=F32)

    for h in heads:
        s_t = lax.dot_general(km_ref[:, h * QK_PAD:(h + 1) * QK_PAD], q_ref[:, h * QK_PAD:(h + 1) * QK_PAD],
                              NT_DIMS, preferred_element_type=F32)
        m = jnp.max(s_t, axis=0, keepdims=True)
        p_t = jnp.exp2(s_t - m)
        m_scr[h] = m
        l_scr[h] = jnp.sum(p_t, axis=0, keepdims=True)
        acc_scr[h] = jnp.dot(vmt_ref[h * DV_MLA:(h + 1) * DV_MLA, :], p_t.astype(BF16), preferred_element_type=F32)
        s_scr[h] = scores(0, h)

    def full_tile(t, carry):
        off = pl.multiple_of(t * tq, tq)
        nxt = pl.multiple_of((t + 1) * tq, tq)
        for h in heads:
            s_next = scores(nxt, h)
            consume(h, off, False)
            s_scr[h] = s_next
        return carry

    lax.fori_loop(0, qi, full_tile, 0)
    for h in heads:
        consume(h, pl.multiple_of(qi * tq, tq), True)
    for h in heads:
        o_ref[:, h * DV_MLA:(h + 1) * DV_MLA] = (acc_scr[h] / l_scr[h]).T.astype(o_ref.dtype)


def mla_prompt(q, k, v_t, k_meta, v_meta_t, tq=512):
    b, s, _ = q.shape
    tq = _row_tile(s, tq)
    assert tq % LANES == 0
    whole = lambda shape: pl.BlockSpec(shape, lambda bi, i: (0, 0))
    return pl.pallas_call(
        _mla_prompt_kernel,
        grid=(b, s // tq),
        in_specs=[
            pl.BlockSpec((None, tq, H_MLA * QK_PAD), lambda bi, i: (bi, i, 0)),
            pl.BlockSpec((None, s, H_MLA * QK_PAD), lambda bi, i: (bi, 0, 0), pipeline_mode=pl.Buffered(1)),
            pl.BlockSpec((H_MLA * DV_MLA, s), lambda bi, i: (0, bi), pipeline_mode=pl.Buffered(1)),
            whole(k_meta.shape),
            whole(v_meta_t.shape),
        ],
        out_specs=pl.BlockSpec((None, tq, H_MLA * DV_MLA), lambda bi, i: (bi, i, 0)),
        out_shape=jax.ShapeDtypeStruct((b, s, H_MLA * DV_MLA), BF16),
        scratch_shapes=[pltpu.VMEM((H_MLA, tq, tq), F32), pltpu.VMEM((H_MLA, 1, tq), F32),
                        pltpu.VMEM((H_MLA, 1, tq), F32), pltpu.VMEM((H_MLA, DV_MLA, tq), F32)],
        compiler_params=_params("parallel", "arbitrary"),
        name="mla_prompt",
    )(q, k, v_t, k_meta, v_meta_t)


def _mla_sample_kernel(q_ref, c_ref, kr_ref, cc_ref, ckr_ref, wukv_ref, o_ref):
    t = q_ref.shape[0]
    q = q_ref[...]
    w = wukv_ref[...]
    qa = jnp.concatenate(
        [lax.dot_general(q[:, h * QK_PAD:h * QK_PAD + DN], w[:, h * (DN + DV_MLA):h * (DN + DV_MLA) + DN],
                         NT_DIMS, preferred_element_type=F32) for h in range(H_MLA)], axis=0).astype(BF16)
    qr = jnp.concatenate([q[:, h * QK_PAD + DN:(h + 1) * QK_PAD] for h in range(H_MLA)], axis=0)
    cache_c = cc_ref[...].astype(BF16)
    own_c = c_ref[...].astype(BF16)
    s_cache = (lax.dot_general(qa, cache_c, NT_DIMS, preferred_element_type=F32)
               + lax.dot_general(qr[:, :DR], ckr_ref[...].astype(BF16), NT_DIMS, preferred_element_type=F32))
    s_own = (lax.dot_general(qa, own_c, NT_DIMS, preferred_element_type=F32)
             + lax.dot_general(qr, kr_ref[...].astype(BF16), NT_DIMS, preferred_element_type=F32))
    m = jnp.maximum(jnp.max(s_cache, axis=-1, keepdims=True), jnp.max(s_own, axis=-1, keepdims=True))
    p_cache = jnp.exp2(s_cache - m)
    p_own = jnp.exp2(s_own - m)
    l = jnp.sum(p_cache, axis=-1, keepdims=True) + jnp.sum(p_own, axis=-1, keepdims=True)
    pc = (jnp.dot(p_cache.astype(BF16), cache_c, preferred_element_type=F32)
          + jnp.dot(p_own.astype(BF16), own_c, preferred_element_type=F32)) / l
    for h in range(H_MLA):
        lo = h * (DN + DV_MLA) + DN
        o_ref[:, h * DV_MLA:(h + 1) * DV_MLA] = _bdot(pc[h * t:(h + 1) * t], w[:, lo:lo + DV_MLA]).astype(o_ref.dtype)


def mla_sample(q, c, kr, cache_c, cache_kr, w_ukv):
    b, t, _ = q.shape
    p = cache_c.shape[1]
    per_b = lambda rows, width: pl.BlockSpec((None, rows, width), lambda bi: (bi, 0, 0))
    return pl.pallas_call(
        _mla_sample_kernel,
        grid=(b,),
        in_specs=[per_b(t, H_MLA * QK_PAD), per_b(t, KV_LORA), per_b(t, LANES), per_b(p, KV_LORA), per_b(p, DR),
                  _const_spec(w_ukv.shape)],
        out_specs=per_b(t, H_MLA * DV_MLA),
        out_shape=jax.ShapeDtypeStruct((b, t, H_MLA * DV_MLA), BF16),
        compiler_params=_params("parallel"),
        name="mla_sample",
    )(q, c, kr, cache_c, cache_kr, w_ukv)


def _rope_tables(pos):
    inv = ROPE_BASE ** (-jnp.arange(0, DR, 2, dtype=F32) / DR)
    ang = pos.astype(F32)[:, None] * inv[None, :]
    cos, sin = jnp.cos(ang), jnp.sin(ang)
    pad = jnp.zeros((pos.shape[0], LANES - DR), F32)
    return jnp.concatenate([cos, cos, pad], axis=1), jnp.concatenate([-sin, sin, pad], axis=1)


def _prepare_weights(ffn1_norm, ffn1_wg, ffn1_wu, ffn1_wd, mix_norm, w_in, conv_w, a_log, dt_bias, gdn_norm,
                     q_norm, kv_norm, w_uq, w_ukv, w_out, ffn2_norm, ffn2_wg, ffn2_wu, ffn2_wd, final_norm):
    w = {}
    w['g1'], w['gm'], w['g2'] = ffn1_norm[0][None], mix_norm[0][None], ffn2_norm[0][None]
    w['gf'] = final_norm[None]
    assert D_FF % FF_CHUNK == 0
    w['wg1'], w['wu1'], w['wd1'] = (m[0].astype(BF16) for m in (ffn1_wg, ffn1_wu, ffn1_wd))
    w['wg2'], w['wu2'], w['wd2'] = (m[0].astype(BF16) for m in (ffn2_wg, ffn2_wu, ffn2_wd))
    o_z = CONV_DIM
    o_a = o_z + H_GDN * GDN_DV
    o_b = o_a + H_GDN
    o_cq = o_b + H_GDN
    o_kr = o_cq + Q_LORA + KV_LORA
    wi = w_in[0]
    w['w_qkv'] = wi[:, :o_z].astype(BF16)
    w['w_z'] = wi[:, o_z:o_a].astype(BF16)
    tail_pad = jnp.zeros((D_MODEL, LANES - DR - 2 * H_GDN), wi.dtype)
    w['w_mla'] = jnp.concatenate([wi[:, o_cq:o_kr], wi[:, o_kr:o_kr + DR], wi[:, o_a:o_cq], tail_pad], axis=1).astype(BF16)
    w['conv_w'] = conv_w[0]
    lane_vec = lambda v: jnp.zeros((1, LANES), F32).at[0, AB_LANE:AB_LANE + H_GDN].set(v.astype(F32))
    w['alog'], w['dtb'] = lane_vec(a_log[0]), lane_vec(dt_bias[0])
    w['gn'] = gdn_norm[0][None]
    w['gq'], w['gkv'] = q_norm[0][None], kv_norm[0][None]
    uq = w_uq[0].reshape(Q_LORA, H_MLA, DN + DR)
    uq = jnp.concatenate([uq, jnp.zeros((Q_LORA, H_MLA, QK_PAD - DN - DR), uq.dtype)], axis=-1)
    w['w_uq'] = uq.reshape(Q_LORA, H_MLA * QK_PAD).astype(BF16)
    w['w_ukv'] = w_ukv[0].astype(BF16)
    w['w_out_g'] = w_out[0][:H_GDN * GDN_DV].astype(BF16)
    w['w_out_m'] = w_out[0][H_GDN * GDN_DV:].astype(BF16)
    return w


def kernel(x_prompt, x_sample, cache_mla_ckv, cache_mla_krope, state_gdn, state_conv, meta, ffn1_norm, ffn1_wg,
           ffn1_wu, ffn1_wd, mix_norm, w_in, conv_w, a_log, dt_bias, gdn_norm, q_norm, kv_norm, w_uq, w_ukv, w_out,
           ffn2_norm, ffn2_wg, ffn2_wu, ffn2_wd, final_norm):
    assert ffn1_wg.shape[0] == 1, "one layer: the meta rows are not carried past the mixer"
    bsz, s_len, _ = x_prompt.shape
    dbs, d_seq, _ = x_sample.shape
    past = cache_mla_ckv.shape[2]
    w = _prepare_weights(ffn1_norm, ffn1_wg, ffn1_wu, ffn1_wd, mix_norm, w_in, conv_w, a_log, dt_bias, gdn_norm,
                         q_norm, kv_norm, w_uq, w_ukv, w_out, ffn2_norm, ffn2_wg, ffn2_wu, ffn2_wd, final_norm)

    front = CHUNK - N_META
    _, qkv_m, _, mla_m = ffn_proj(meta.astype(F32), w)
    meta_qkv = jnp.pad(qkv_m, ((front, 0), (0, 0)))[None]
    x1_p, feat_p, z_p, mla_p, tail_p = ffn_proj(x_prompt.reshape(bsz * s_len, D_MODEL), w,
                                                 hist=meta_qkv[0, CHUNK - GDN_HIST:], stream_len=s_len)
    x1_s, qkv_s, z_s, mla_s = ffn_proj(x_sample.reshape(dbs * d_seq, D_MODEL), w)

    _, m_meta = gdn(meta_qkv, jnp.zeros((1, CHUNK, H_GDN * GDN_DV), F32), jnp.pad(mla_m, ((front, 0), (0, 0)))[None],
                    jnp.zeros((1, H_GDN, GDN_DK, GDN_DV), F32), jnp.zeros((1, GDN_HIST, CONV_DIM), F32), w,
                    pad_front=front)
    m0_p = jnp.broadcast_to(m_meta, (bsz,) + m_meta.shape[1:])
    gdn_p, m_p = gdn(feat_p.reshape(bsz, s_len, CONV_DIM), z_p.reshape(bsz, s_len, -1),
                     mla_p.reshape(bsz, s_len, MLA_IN), m0_p, jnp.zeros((bsz, GDN_HIST, CONV_DIM), F32), w,
                     rows=GDN_PROMPT_ROWS, prepared=True)
    conv_p = tail_p.reshape(bsz, -1, GDN_HIST, CONV_DIM)[:, -1, GDN_HIST - (CONV_W - 1):]
    qkv_s3 = qkv_s.reshape(dbs, d_seq, CONV_DIM)
    conv0_s = jnp.pad(state_conv[0].astype(F32), ((0, 0), (GDN_HIST - (CONV_W - 1), 0), (0, 0)))
    gdn_s, m_s = gdn(qkv_s3, z_s.reshape(dbs, d_seq, -1), mla_s.reshape(dbs, d_seq, MLA_IN),
                     state_gdn[0].astype(F32), conv0_s, w)

    cos_m, sin_m = _rope_tables(jnp.arange(N_META))
    cos_p, sin_p = _rope_tables(N_META + jnp.arange(s_len))
    cos_s, sin_s = _rope_tables(past + jnp.arange(d_seq))
    q_p, c_p, kr_p, k_p, vt_p = mla_prep(mla_p, cos_p, sin_p, w, expand=True)
    _, c_m, kr_m, k_m, vt_m = mla_prep(mla_m, cos_m, sin_m, w, expand=True)
    q_s, c_s, kr_s = mla_prep(mla_s, cos_s, sin_s, w, expand=False)
    mla_o_p = mla_prompt(q_p.reshape(bsz, s_len, -1), k_p.reshape(bsz, s_len, -1), vt_p, k_m, vt_m)
    mla_o_s = mla_sample(q_s.reshape(dbs, d_seq, -1), c_s.reshape(dbs, d_seq, -1), kr_s.reshape(dbs, d_seq, -1),
                         cache_mla_ckv[0].astype(F32), cache_mla_krope[0].astype(F32), w['w_ukv'])

    y_p = out_ffn(x1_p, gdn_p.reshape(bsz * s_len, -1), mla_o_p.reshape(bsz * s_len, -1), w)
    y_s = out_ffn(x1_s, gdn_s.reshape(dbs * d_seq, -1), mla_o_s.reshape(dbs * d_seq, -1), w)

    with_meta = lambda m_rows, rows: jnp.concatenate(
        [jnp.broadcast_to(m_rows[None], (bsz,) + m_rows.shape), rows.reshape(bsz, s_len, -1)], axis=1)
    return (y_p.reshape(bsz, s_len, D_MODEL), y_s.reshape(dbs, d_seq, D_MODEL),
            with_meta(c_m, c_p)[None], with_meta(kr_m[:, :DR], kr_p[:, :DR])[None],
            m_p[None], conv_p[None],
            c_s.reshape(dbs, d_seq, KV_LORA)[None], kr_s[:, :DR].reshape(dbs, d_seq, DR)[None],
            m_s[None], qkv_s3[:, d_seq - (CONV_W - 1):][None])
```

```python
import functools

import jax
import jax.numpy as jnp
import numpy as np
from jax import lax
from jax.experimental import pallas as pl
from jax.experimental.pallas import tpu as pltpu

F32 = jnp.float32
BF16 = jnp.bfloat16

D_MODEL = 1024
CHUNK = 64
N_META = 16
H_GDN = 4
GDN_DK = 128
GDN_DV = 128
CONV_W = 4
CONV_DIM = H_GDN * (2 * GDN_DK + GDN_DV)
H_MLA = 4
Q_LORA = 384
KV_LORA = 256
DN = 128
DR = 64
DV_MLA = 128
ROPE_BASE = 10000.0
SM_SCALE = (DN + DR) ** -0.5
LOG2_E = 1.4426950408889634
Q_SCALE = SM_SCALE * LOG2_E
D_FF = 2816
EPS = 1e-6
L2_EPS = 1e-6

LANES = 128
SUBLANES = 8
FF_CHUNK = 256
MLA_IN = Q_LORA + KV_LORA + LANES
QK_PAD = 2 * LANES
AB_LANE = DR
VMEM_LIMIT = 56 * 1024 * 1024

NT_DIMS = (((1,), (1,)), ((), ()))


def _rms(x, g):
    return x * lax.rsqrt(jnp.mean(x * x, axis=-1, keepdims=True) + EPS) * g


def _bdot(a, b):
    return jnp.dot(a.astype(BF16), b.astype(BF16), preferred_element_type=F32)


def _bdot_nt(a, b):
    return lax.dot_general(a.astype(BF16), b.astype(BF16), NT_DIMS, preferred_element_type=F32)


def _const_spec(shape):
    nd = len(shape)
    return pl.BlockSpec(shape, lambda *_: (0,) * nd, pipeline_mode=pl.Buffered(1))


def _params(*sem):
    return pltpu.CompilerParams(dimension_semantics=sem, vmem_limit_bytes=VMEM_LIMIT)


def _drain(*stage_gens):
    live = [[g, 0.0, float(t)] for g, t in (sg if isinstance(sg, tuple) else (sg, 1.0) for sg in stage_gens)]
    while live:
        entry = min(live, key=lambda e: e[1] / e[2])
        try:
            entry[1] += next(entry[0]) or 0.0
        except StopIteration:
            live.remove(entry)


def _swiglu_half_stages(x, g_ref, wg_ref, wu_ref, wd_ref, h_ref, acc_ref, result):
    h_ref[...] = _rms(x, g_ref[...]).astype(BF16)
    nf = wg_ref.shape[1] // FF_CHUNK

    def gate_up(f):
        cols = slice(f * FF_CHUNK, (f + 1) * FF_CHUNK)
        gate = jnp.dot(h_ref[...], wg_ref[:, cols], preferred_element_type=F32)
        up = jnp.dot(h_ref[...], wu_ref[:, cols], preferred_element_type=F32)
        return gate, up

    nxt = gate_up(0)
    yield 1.0
    for f in range(nf):
        gate, up = nxt
        if f + 1 < nf:
            nxt = gate_up(f + 1)
            yield 1.0
        act = (jax.nn.silu(gate) * up).astype(BF16)
        down = jnp.dot(act, wd_ref[f * FF_CHUNK:(f + 1) * FF_CHUNK, :], preferred_element_type=F32)
        if f == 0:
            acc_ref[...] = down
        else:
            acc_ref[...] += down
        yield 1.0
    result.append(x + 0.5 * acc_ref[...])


def _swiglu_half(x, g_ref, wg_ref, wu_ref, wd_ref, h_ref, acc_ref):
    result = []
    _drain(_swiglu_half_stages(x, g_ref, wg_ref, wu_ref, wd_ref, h_ref, acc_ref, result))
    return result[0]


def _ffn_proj_kernel(x_ref, g1_ref, wg_ref, wu_ref, wd_ref, gm_ref, wqkv_ref, wz_ref, wmla_ref, *rest,
                     stream_tiles):
    if not stream_tiles:
        x1_ref, qkv_ref, z_ref, mla_ref, h_ref, acc_ref = rest
        x1 = _swiglu_half(x_ref[...], g1_ref, wg_ref, wu_ref, wd_ref, h_ref, acc_ref)
    else:
        hist_ref, convw_ref, x1_ref, qkv_ref, z_ref, mla_ref, tail_ref, h_ref, acc_ref, conv_scr = rest
        i = pl.program_id(0)
        tm = x_ref.shape[0]

        @pl.when(i == 0)
        def _():
            conv_scr[...] = jnp.zeros_like(conv_scr)

        @pl.when((i + stream_tiles - 1) % stream_tiles == 0)
        def _():
            conv_scr[0, 0:GDN_HIST, :] = hist_ref[...]

        def feature_stages():
            for row0 in range(0, tm, CHUNK):
                qkv_ref[row0:row0 + CHUNK, :] = _gdn_features(conv_scr, convw_ref, 0, row0)
                yield 1.0
            conv_scr[0, 0:GDN_HIST, :] = conv_scr[0, tm:tm + GDN_HIST, :]

        result = []
        _drain((_swiglu_half_stages(x_ref[...], g1_ref, wg_ref, wu_ref, wd_ref, h_ref, acc_ref, result),
                2 * (D_FF // FF_CHUNK)), (feature_stages(), tm // CHUNK))
        x1 = result[0]
    x1_ref[...] = x1
    h_ref[...] = _rms(x1, gm_ref[...]).astype(BF16)
    qkv = jnp.dot(h_ref[...], wqkv_ref[...], preferred_element_type=F32)
    z_ref[...] = jnp.dot(h_ref[...], wz_ref[...], preferred_element_type=F32)
    mla_ref[...] = jnp.dot(h_ref[...], wmla_ref[...], preferred_element_type=F32)
    if stream_tiles:
        conv_scr[0, GDN_HIST:GDN_HIST + tm, :] = qkv
        tail_ref[0] = qkv[tm - GDN_HIST:, :]
    else:
        qkv_ref[...] = qkv


def _row_tile(n, want):
    t = min(want, n)
    assert n % t == 0, (n, t)
    return t


def ffn_proj(x, w, hist=None, stream_len=None, tm=512):
    n = x.shape[0]
    tm = _row_tile(n, tm)
    ntile = n // tm
    stream_tiles = 0
    if hist is not None:
        assert stream_len % tm == 0 and tm % CHUNK == 0
        stream_tiles = stream_len // tm
    tile = (lambda i: jnp.minimum(i, ntile - 1)) if stream_tiles else (lambda i: i)
    row = lambda width: pl.BlockSpec((tm, width), lambda i: (tile(i), 0))
    consts = (w['g1'], w['wg1'], w['wu1'], w['wd1'], w['gm'], w['w_qkv'], w['w_z'], w['w_mla'])
    out_specs = [row(D_MODEL), row(CONV_DIM), row(H_GDN * GDN_DV), row(MLA_IN)]
    out_shape = [jax.ShapeDtypeStruct((n, D_MODEL), F32), jax.ShapeDtypeStruct((n, CONV_DIM), F32),
                 jax.ShapeDtypeStruct((n, H_GDN * GDN_DV), F32), jax.ShapeDtypeStruct((n, MLA_IN), F32)]
    scratch = [pltpu.VMEM((tm, D_MODEL), BF16), pltpu.VMEM((tm, D_MODEL), F32)]
    if stream_tiles:
        consts += (hist, w['conv_w'])
        out_specs[1] = pl.BlockSpec((tm, CONV_DIM), lambda i: (jnp.maximum(i - 1, 0), 0))
        out_specs.append(pl.BlockSpec((1, GDN_HIST, CONV_DIM), lambda i: (tile(i), 0, 0)))
        out_shape.append(jax.ShapeDtypeStruct((ntile, GDN_HIST, CONV_DIM), F32))
        scratch.append(pltpu.VMEM((1, GDN_HIST + tm, CONV_DIM), F32))
    return pl.pallas_call(
        functools.partial(_ffn_proj_kernel, stream_tiles=stream_tiles),
        grid=(ntile + (1 if stream_tiles else 0),),
        in_specs=[row(D_MODEL)] + [_const_spec(c.shape) for c in consts],
        out_specs=out_specs,
        out_shape=out_shape,
        scratch_shapes=scratch,
        compiler_params=_params("arbitrary" if stream_tiles else "parallel"),
        name="ffn_proj",
    )(x, *consts)


def _out_ffn_kernel(x1_ref, gdn_ref, mla_ref, wog_ref, wom_ref, g2_ref, wg_ref, wu_ref, wd_ref, gf_ref,
                    y_ref, h_ref, acc_ref):
    x2 = (x1_ref[...] + jnp.dot(gdn_ref[...], wog_ref[...], preferred_element_type=F32)
          + jnp.dot(mla_ref[...], wom_ref[...], preferred_element_type=F32))
    x3 = _swiglu_half(x2, g2_ref, wg_ref, wu_ref, wd_ref, h_ref, acc_ref)
    y_ref[...] = _rms(x3, gf_ref[...])


def out_ffn(x1, gdn, mla, w, tm=512):
    n = x1.shape[0]
    tm = _row_tile(n, tm)
    row = lambda width: pl.BlockSpec((tm, width), lambda i: (i, 0))
    consts = (w['w_out_g'], w['w_out_m'], w['g2'], w['wg2'], w['wu2'], w['wd2'], w['gf'])
    return pl.pallas_call(
        _out_ffn_kernel,
        grid=(n // tm,),
        in_specs=[row(D_MODEL), row(H_GDN * GDN_DV), row(H_MLA * DV_MLA)] + [_const_spec(c.shape) for c in consts],
        out_specs=row(D_MODEL),
        out_shape=jax.ShapeDtypeStruct((n, D_MODEL), F32),
        scratch_shapes=[pltpu.VMEM((tm, D_MODEL), BF16), pltpu.VMEM((tm, D_MODEL), F32)],
        compiler_params=_params("parallel"),
        name="out_ffn",
    )(x1, gdn, mla, *consts)


def _cumsum_rows(x):
    n = x.shape[0]
    row = lax.broadcasted_iota(jnp.int32, x.shape, 0)
    shift = 1
    while shift < n:
        x = x + jnp.where(row >= shift, pltpu.roll(x, shift, 0), 0.0)
        shift *= 2
    return x


def _transpose_rows(x):
    length = x.shape[0]
    sq = jnp.concatenate([x, jnp.zeros((LANES - length, LANES), x.dtype)], axis=0)
    return sq.T[:, :length]


GDN_HIST = SUBLANES
GDN_PREP_COST = 4.0
GDN_PROMPT_ROWS = 2 * CHUNK


def _gdn_features(conv_scr, convw_ref, i, row0):
    base = GDN_HIST - (CONV_W - 1) + row0
    parts = []
    for h in range(CONV_DIM // GDN_DK):
        cols = slice(h * GDN_DK, (h + 1) * GDN_DK)
        x = conv_scr[i, pl.ds(base, CHUNK), cols] * convw_ref[0:1, cols]
        for j in range(1, CONV_W):
            x = x + conv_scr[i, pl.ds(base + j, CHUNK), cols] * convw_ref[j:j + 1, cols]
        x = jax.nn.silu(x)
        if h < 2 * H_GDN:
            x = x * lax.rsqrt(jnp.sum(x * x, axis=-1, keepdims=True) + L2_EPS)
        parts.append(x * (GDN_DK ** -0.5) if h < H_GDN else x)
    return jnp.concatenate(parts, axis=1)


def _gdn_block_stages(first_block, rows, pad_front, features, ab_ref, z_ref, write_o, m_scr,
                      alog_ref, dtb_ref, gn_ref):
    bg = ab_ref.shape[0]
    length = CHUNK
    row = lax.broadcasted_iota(jnp.int32, (length, LANES), 0)
    ri = lax.broadcasted_iota(jnp.int32, (length, length), 0)
    ci = lax.broadcasted_iota(jnp.int32, (length, length), 1)
    causal = ci <= ri
    strict = ci < ri

    def per_batch(i, row0):
        u = features(i, row0)
        ab = ab_ref[i, row0:row0 + length, :]
        g_all = -jnp.exp(alog_ref[...]) * jax.nn.softplus(ab + dtb_ref[...])
        beta_all = jax.nn.sigmoid(ab)
        if pad_front > row0:
            valid = jnp.logical_or(jnp.logical_not(first_block), row >= pad_front - row0)
            g_all = jnp.where(valid, g_all, 0.0)
            beta_all = jnp.where(valid, beta_all, 0.0)
        gc_all = _cumsum_rows(g_all)
        gc_t = _transpose_rows(gc_all)

        chains = []
        for h in range(H_GDN):
            q = u[:, h * GDN_DK:(h + 1) * GDN_DK]
            k = u[:, (H_GDN + h) * GDN_DK:(H_GDN + h + 1) * GDN_DK]
            v = u[:, 2 * H_GDN * GDN_DK + h * GDN_DV:2 * H_GDN * GDN_DK + (h + 1) * GDN_DV]
            gc = gc_all[:, AB_LANE + h:AB_LANE + h + 1]
            beta = beta_all[:, AB_LANE + H_GDN + h:AB_LANE + H_GDN + h + 1]
            gc_row = gc_t[AB_LANE + h:AB_LANE + h + 1, :]
            decay = jnp.where(causal, jnp.exp(jnp.where(causal, gc - gc_row, 0.0)), 0.0)
            kq = jnp.concatenate([k, q], axis=0).astype(BF16)
            chains.append(dict(i=i, h=h, row0=row0, k=k, v=v, gc=gc, beta=beta, decay=decay, kq=kq))
        return chains

    per_chunk = []
    for row0 in range(0, rows, length):
        per_chunk.append([])
        for i in range(bg):
            per_chunk[-1] += per_batch(i, row0)
            yield GDN_PREP_COST
    chains = [ch for chunk in per_chunk for ch in chunk]
    for ch in chains:
        ch['kk_qk'] = lax.dot_general(ch['kq'], ch['k'].astype(BF16), NT_DIMS, preferred_element_type=F32)
    yield 1.0
    for ch in chains:
        ch['a'] = jnp.where(strict, ch['beta'] * ch['kk_qk'][:length] * ch['decay'], 0.0)
        ch['qk_decay'] = ch['kk_qk'][length:] * ch['decay']
        ch['y'] = -ch['a']
    for ch in chains:
        ch['pw'] = _bdot(ch['a'], ch['a'])
    yield 1.0
    span = 2
    while span < length:
        span *= 2
        for ch in chains:
            ch['y'] = ch['y'] + ch['pw'] + _bdot(ch['y'], ch['pw'])
        yield 1.0
        if span < length:
            for ch in chains:
                ch['pw'] = _bdot(ch['pw'], ch['pw'])
            yield 1.0

    def emit_output(chunk):
        for ch in chunk:
            o = ch['eg'] * ch['kqm'][length:] + _bdot(ch['qk_decay'], ch['uu'])
            n = _rms(o, gn_ref[...])
            rws = slice(ch['row0'], ch['row0'] + length)
            cols = slice(ch['h'] * GDN_DV, (ch['h'] + 1) * GDN_DV)
            write_o(ch['i'], rws, cols, (n * jax.nn.silu(z_ref[ch['i'], rws, cols])).astype(BF16))

    pending = None
    for chunk in per_chunk:
        for ch in chunk:
            ch['m0'] = m_scr[ch['i'], ch['h']]
            ch['kqm'] = jnp.dot(ch['kq'], ch['m0'].astype(BF16), preferred_element_type=F32)
        yield 1.0
        if pending is not None:
            emit_output(pending)
            yield 1.0
        for ch in chunk:
            ch['eg'] = jnp.exp(ch['gc'])
            rhs = ch['beta'] * (ch['v'] - ch['eg'] * ch['kqm'][:length])
            ch['uu'] = rhs + _bdot(ch['y'], rhs)
        yield 1.0
        for ch in chunk:
            g_last = ch['gc'][length - 1:length, :]
            k_dec = ch['k'] * jnp.exp(g_last - ch['gc'])
            m_scr[ch['i'], ch['h']] = jnp.exp(g_last) * ch['m0'] + _bdot(_transpose_rows(k_dec), ch['uu'])
        yield 1.0
        pending = chunk
    emit_output(pending)
    yield 1.0


def _gdn_stage_block(qkv_ref, conv_scr):
    bg, rows, _ = qkv_ref.shape
    for i in range(bg):
        conv_scr[i, GDN_HIST:GDN_HIST + rows, :] = qkv_ref[i]


def _gdn_keep_history(conv_scr, rows):
    for i in range(conv_scr.shape[0]):
        conv_scr[i, 0:GDN_HIST, :] = conv_scr[i, rows:rows + GDN_HIST, :]


def _gdn_kernel(qkv_ref, z_ref, ab_ref, m0_ref, conv0_ref, convw_ref, alog_ref, dtb_ref, gn_ref,
                o_ref, mout_ref, conv_scr, m_scr, *, pad_front, prepared):
    c = pl.program_id(1)
    rows = qkv_ref.shape[1]

    @pl.when(c == 0)
    def _():
        m_scr[...] = m0_ref[...]
        conv_scr[:, 0:GDN_HIST, :] = conv0_ref[...]

    def write_o(i, rws, cols, val):
        o_ref[i, rws, cols] = val

    if prepared:
        features = lambda i, row0: qkv_ref[i, row0:row0 + CHUNK, :]
    else:
        _gdn_stage_block(qkv_ref, conv_scr)
        features = functools.partial(_gdn_features, conv_scr, convw_ref)
    _drain(_gdn_block_stages(c == 0, rows, pad_front, features, ab_ref, z_ref, write_o, m_scr,
                             alog_ref, dtb_ref, gn_ref))
    if not prepared:
        _gdn_keep_history(conv_scr, rows)

    @pl.when(c == pl.num_programs(1) - 1)
    def _():
        mout_ref[...] = m_scr[...]


def _gdn_in_specs(bg, rows, blk):
    nd = len(blk(0, 0))
    const = lambda shape: pl.BlockSpec(shape, lambda *_: (0,) * len(shape))
    return [
        pl.BlockSpec((bg, rows, CONV_DIM), blk),
        pl.BlockSpec((bg, rows, H_GDN * GDN_DV), blk),
        pl.BlockSpec((bg, rows, LANES), lambda *a: blk(*a)[:nd - 1] + (MLA_IN // LANES - 1,)),
        pl.BlockSpec((bg, H_GDN, GDN_DK, GDN_DV), lambda g, c: (g, 0, 0, 0)),
        pl.BlockSpec((bg, GDN_HIST, CONV_DIM), lambda g, c: (g, 0, 0)),
        const((CONV_W, CONV_DIM)),
        const((1, LANES)),
        const((1, LANES)),
        const((1, GDN_DV)),
    ]


def gdn(qkv, z, mla_in, m0, conv0, w, *, pad_front=0, bg=4, rows=CHUNK, prepared=False):
    b, t, _ = qkv.shape
    bg = min(bg, b)
    assert b % bg == 0 and t % rows == 0 and rows % CHUNK == 0
    blk = lambda g, c: (g, c, 0)
    return pl.pallas_call(
        functools.partial(_gdn_kernel, pad_front=pad_front, prepared=prepared),
        grid=(b // bg, t // rows),
        in_specs=_gdn_in_specs(bg, rows, blk),
        out_specs=[
            pl.BlockSpec((bg, rows, H_GDN * GDN_DV), blk),
            pl.BlockSpec((bg, H_GDN, GDN_DK, GDN_DV), lambda g, c: (g, 0, 0, 0)),
        ],
        out_shape=[jax.ShapeDtypeStruct((b, t, H_GDN * GDN_DV), BF16),
                   jax.ShapeDtypeStruct((b, H_GDN, GDN_DK, GDN_DV), F32)],
        scratch_shapes=[pltpu.VMEM((bg, GDN_HIST + (0 if prepared else rows), CONV_DIM), F32),
                        pltpu.VMEM((bg, H_GDN, GDN_DK, GDN_DV), F32)],
        compiler_params=_params("parallel", "arbitrary"),
        name="gdn",
    )(qkv, z, mla_in, m0, conv0, w['conv_w'], w['alog'], w['dtb'], w['gn'])


def _rope(t, cos, sin):
    lane = lax.broadcasted_iota(jnp.int32, t.shape, 1)
    half = DR // 2
    swapped = jnp.where(lane < half, pltpu.roll(t, LANES - half, 1), pltpu.roll(t, half, 1))
    return t * cos + swapped * sin


def _mla_prep_kernel(x_ref, cos_ref, sin_ref, gq_ref, gkv_ref, wuq_ref, wukv_ref, *out_refs, expand):
    q_ref, c_ref, kr_ref = out_refs[:3]
    x = x_ref[...]
    cos = cos_ref[...]
    sin = sin_ref[...]
    q = _bdot(_rms(x[:, :Q_LORA], gq_ref[...]), wuq_ref[...])
    for h in range(H_MLA):
        lo = h * QK_PAD
        q_ref[:, lo:lo + DN] = (q[:, lo:lo + DN] * Q_SCALE).astype(BF16)
        q_ref[:, lo + DN:lo + QK_PAD] = (_rope(q[:, lo + DN:lo + QK_PAD], cos, sin) * Q_SCALE).astype(BF16)
    c = _rms(x[:, Q_LORA:Q_LORA + KV_LORA], gkv_ref[...])
    c_ref[...] = c
    kr = _rope(x[:, Q_LORA + KV_LORA:], cos, sin)
    kr_ref[...] = kr
    if expand:
        k_ref, vt_ref = out_refs[3:]
        kv = _bdot(c, wukv_ref[...])
        for h in range(H_MLA):
            lo = h * (DN + DV_MLA)
            k_ref[:, h * QK_PAD:h * QK_PAD + DN] = kv[:, lo:lo + DN].astype(BF16)
            k_ref[:, h * QK_PAD + DN:(h + 1) * QK_PAD] = kr.astype(BF16)
            v = kv[:, lo + DN:lo + DN + DV_MLA]
            v_t = v.T if v.shape[0] % LANES == 0 else _transpose_rows(v)
            vt_ref[h * DV_MLA:(h + 1) * DV_MLA, :] = v_t.astype(BF16)


def mla_prep(mla_in, cos, sin, w, *, expand, tm=512):
    n = mla_in.shape[0]
    tm = _row_tile(min(n, cos.shape[0]), tm)
    nrep = cos.shape[0] // tm
    row = lambda width: pl.BlockSpec((tm, width), lambda i: (i, 0))
    tab = pl.BlockSpec((tm, LANES), lambda i: (i % nrep, 0))
    consts = (w['gq'], w['gkv'], w['w_uq'], w['w_ukv'])
    out_specs = [row(H_MLA * QK_PAD), row(KV_LORA), row(LANES)]
    out_shape = [jax.ShapeDtypeStruct((n, H_MLA * QK_PAD), BF16), jax.ShapeDtypeStruct((n, KV_LORA), F32),
                 jax.ShapeDtypeStruct((n, LANES), F32)]
    if expand:
        out_specs += [row(H_MLA * QK_PAD), pl.BlockSpec((H_MLA * DV_MLA, tm), lambda i: (0, i))]
        out_shape += [jax.ShapeDtypeStruct((n, H_MLA * QK_PAD), BF16), jax.ShapeDtypeStruct((H_MLA * DV_MLA, n), BF16)]
    return pl.pallas_call(
        functools.partial(_mla_prep_kernel, expand=expand),
        grid=(n // tm,),
        in_specs=[row(MLA_IN), tab, tab] + [_const_spec(c.shape) for c in consts],
        out_specs=out_specs,
        out_shape=out_shape,
        compiler_params=_params("parallel"),
        name="mla_prep_kv" if expand else "mla_prep",
    )(mla_in, cos, sin, *consts)


def _mla_prompt_kernel(q_ref, k_ref, vt_ref, km_ref, vmt_ref, o_ref, s_scr, m_scr, l_scr, acc_scr):
    qi = pl.program_id(1)
    tq = q_ref.shape[0]
    heads = range(H_MLA)

    def scores(off, h):
        return lax.dot_general(k_ref[pl.ds(off, tq), h * QK_PAD:(h + 1) * QK_PAD],
                               q_ref[:, h * QK_PAD:(h + 1) * QK_PAD], NT_DIMS, preferred_element_type=F32)

    def consume(h, off, masked):
        s_t = s_scr[h]
        if masked:
            key_chunk = lax.broadcasted_iota(jnp.int32, (tq, tq), 0) // CHUNK
            qry_chunk = lax.broadcasted_iota(jnp.int32, (tq, tq), 1) // CHUNK
            s_t = jnp.where(key_chunk <= qry_chunk, s_t, -jnp.inf)
        m = m_scr[h]
        m_new = jnp.maximum(m, jnp.max(s_t, axis=0, keepdims=True))
        alpha = jnp.exp2(m - m_new)
        p_t = jnp.exp2(s_t - m_new)
        m_scr[h] = m_new
        l_scr[h] = alpha * l_scr[h] + jnp.sum(p_t, axis=0, keepdims=True)
        acc_scr[h] = alpha * acc_scr[h] + jnp.dot(vt_ref[h * DV_MLA:(h + 1) * DV_MLA, pl.ds(off, tq)],
                                                  p_t.astype(BF16), preferred_element_type=F32)

    meta_s = [lax.dot_general(km_ref[:, h * QK_PAD:(h + 1) * QK_PAD], q_ref[:, h * QK_PAD:(h + 1) * QK_PAD],
                              NT_DIMS, preferred_element_type=F32) for h in heads]
    for h in heads:
        s_scr[h] = scores(0, h)
    meta_p = []
    for h in heads:
        m = jnp.max(meta_s[h], axis=0, keepdims=True)
        p_t = jnp.exp2(meta_s[h] - m)
        m_scr[h] = m
        l_scr[h] = jnp.sum(p_t, axis=0, keepdims=True)
        meta_p.append(p_t.astype(BF16))
    for h in heads:
        acc_scr[h] = jnp.dot(vmt_ref[h * DV_MLA:(h + 1) * DV_MLA, :], meta_p[h], preferred_element_type=F32)

    def full_tile(t):
        off = pl.multiple_of(t * tq, tq)
        nxt = pl.multiple_of((t + 1) * tq, tq)
        for h in heads:
            s_next = scores(nxt, h)
            consume(h, off, False)
            s_scr[h] = s_next

    def tile_pair(t2, carry):
        full_tile(2 * t2)
        full_tile(2 * t2 + 1)
        return carry

    lax.fori_loop(0, qi // 2, tile_pair, 0)

    @pl.when(qi % 2 == 1)
    def _():
        full_tile(qi - 1)

    for h in heads:
        consume(h, pl.multiple_of(qi * tq, tq), True)
    for h in heads:
        o_ref[:, h * DV_MLA:(h + 1) * DV_MLA] = (acc_scr[h] / l_scr[h]).T.astype(o_ref.dtype)


def mla_prompt(q, k, v_t, k_meta, v_meta_t, tq=512):
    b, s, _ = q.shape
    tq = _row_tile(s, tq)
    assert tq % LANES == 0
    whole = lambda shape: pl.BlockSpec(shape, lambda bi, i: (0, 0))
    return pl.pallas_call(
        _mla_prompt_kernel,
        grid=(b, s // tq),
        in_specs=[
            pl.BlockSpec((None, tq, H_MLA * QK_PAD), lambda bi, i: (bi, i, 0)),
            pl.BlockSpec((None, s, H_MLA * QK_PAD), lambda bi, i: (bi, 0, 0), pipeline_mode=pl.Buffered(1)),
            pl.BlockSpec((H_MLA * DV_MLA, s), lambda bi, i: (0, bi), pipeline_mode=pl.Buffered(1)),
            whole(k_meta.shape),
            whole(v_meta_t.shape),
        ],
        out_specs=pl.BlockSpec((None, tq, H_MLA * DV_MLA), lambda bi, i: (bi, i, 0)),
        out_shape=jax.ShapeDtypeStruct((b, s, H_MLA * DV_MLA), BF16),
        scratch_shapes=[pltpu.VMEM((H_MLA, tq, tq), F32), pltpu.VMEM((H_MLA, 1, tq), F32),
                        pltpu.VMEM((H_MLA, 1, tq), F32), pltpu.VMEM((H_MLA, DV_MLA, tq), F32)],
        compiler_params=_params("parallel", "arbitrary"),
        name="mla_prompt",
    )(q, k, v_t, k_meta, v_meta_t)


def _mla_sample_kernel(q_ref, c_ref, kr_ref, cc_ref, ckr_ref, wukv_ref, o_ref):
    nreq, t, _ = q_ref.shape
    w = wukv_ref[...]
    reqs = []
    for r in range(nreq):
        q = q_ref[r]
        qa = jnp.concatenate(
            [lax.dot_general(q[:, h * QK_PAD:h * QK_PAD + DN], w[:, h * (DN + DV_MLA):h * (DN + DV_MLA) + DN],
                             NT_DIMS, preferred_element_type=F32) for h in range(H_MLA)], axis=0).astype(BF16)
        qr = jnp.concatenate([q[:, h * QK_PAD + DN:(h + 1) * QK_PAD] for h in range(H_MLA)], axis=0)
        reqs.append(dict(qa=qa, qr=qr, cache_c=cc_ref[r].astype(BF16), own_c=c_ref[r].astype(BF16)))
    for r, rq in enumerate(reqs):
        rq['s_cache'] = (lax.dot_general(rq['qa'], rq['cache_c'], NT_DIMS, preferred_element_type=F32)
                         + lax.dot_general(rq['qr'][:, :DR], ckr_ref[r].astype(BF16), NT_DIMS,
                                           preferred_element_type=F32))
        rq['s_own'] = (lax.dot_general(rq['qa'], rq['own_c'], NT_DIMS, preferred_element_type=F32)
                       + lax.dot_general(rq['qr'], kr_ref[r].astype(BF16), NT_DIMS, preferred_element_type=F32))
    for rq in reqs:
        m = jnp.maximum(jnp.max(rq['s_cache'], axis=-1, keepdims=True), jnp.max(rq['s_own'], axis=-1, keepdims=True))
        p_cache = jnp.exp2(rq['s_cache'] - m)
        p_own = jnp.exp2(rq['s_own'] - m)
        rq['l'] = jnp.sum(p_cache, axis=-1, keepdims=True) + jnp.sum(p_own, axis=-1, keepdims=True)
        rq['p_cache'], rq['p_own'] = p_cache.astype(BF16), p_own.astype(BF16)
    for rq in reqs:
        rq['pc'] = (jnp.dot(rq['p_cache'], rq['cache_c'], preferred_element_type=F32)
                    + jnp.dot(rq['p_own'], rq['own_c'], preferred_element_type=F32)) / rq['l']
    for r, rq in enumerate(reqs):
        for h in range(H_MLA):
            lo = h * (DN + DV_MLA) + DN
            o_ref[r, :, h * DV_MLA:(h + 1) * DV_MLA] = _bdot(rq['pc'][h * t:(h + 1) * t],
                                                             w[:, lo:lo + DV_MLA]).astype(o_ref.dtype)


def mla_sample(q, c, kr, cache_c, cache_kr, w_ukv, group=2):
    b, t, _ = q.shape
    p = cache_c.shape[1]
    group = min(group, b)
    assert b % group == 0
    per_b = lambda rows, width: pl.BlockSpec((group, rows, width), lambda bi: (bi, 0, 0))
    return pl.pallas_call(
        _mla_sample_kernel,
        grid=(b // group,),
        in_specs=[per_b(t, H_MLA * QK_PAD), per_b(t, KV_LORA), per_b(t, LANES), per_b(p, KV_LORA), per_b(p, DR),
                  _const_spec(w_ukv.shape)],
        out_specs=per_b(t, H_MLA * DV_MLA),
        out_shape=jax.ShapeDtypeStruct((b, t, H_MLA * DV_MLA), BF16),
        compiler_params=_params("parallel"),
        name="mla_sample",
    )(q, c, kr, cache_c, cache_kr, w_ukv)


def _rope_tables(first, count):
    inv = ROPE_BASE ** (-np.arange(0, DR, 2, dtype=np.float64) / DR)
    ang = (first + np.arange(count, dtype=np.float64))[:, None] * inv[None, :]
    cos, sin = np.cos(ang), np.sin(ang)
    pad = np.zeros((count, LANES - DR))
    return (jnp.asarray(np.concatenate([cos, cos, pad], axis=1), F32),
            jnp.asarray(np.concatenate([-sin, sin, pad], axis=1), F32))


def _prepare_weights(ffn1_norm, ffn1_wg, ffn1_wu, ffn1_wd, mix_norm, w_in, conv_w, a_log, dt_bias, gdn_norm,
                     q_norm, kv_norm, w_uq, w_ukv, w_out, ffn2_norm, ffn2_wg, ffn2_wu, ffn2_wd, final_norm):
    w = {}
    w['g1'], w['gm'], w['g2'] = ffn1_norm[0][None], mix_norm[0][None], ffn2_norm[0][None]
    w['gf'] = final_norm[None]
    assert D_FF % FF_CHUNK == 0
    w['wg1'], w['wu1'], w['wd1'] = (m[0].astype(BF16) for m in (ffn1_wg, ffn1_wu, ffn1_wd))
    w['wg2'], w['wu2'], w['wd2'] = (m[0].astype(BF16) for m in (ffn2_wg, ffn2_wu, ffn2_wd))
    o_z = CONV_DIM
    o_a = o_z + H_GDN * GDN_DV
    o_b = o_a + H_GDN
    o_cq = o_b + H_GDN
    o_kr = o_cq + Q_LORA + KV_LORA
    wi = w_in[0]
    w['w_qkv'] = wi[:, :o_z].astype(BF16)
    w['w_z'] = wi[:, o_z:o_a].astype(BF16)
    tail_pad = jnp.zeros((D_MODEL, LANES - DR - 2 * H_GDN), wi.dtype)
    w['w_mla'] = jnp.concatenate([wi[:, o_cq:o_kr], wi[:, o_kr:o_kr + DR], wi[:, o_a:o_cq], tail_pad], axis=1).astype(BF16)
    w['conv_w'] = conv_w[0]
    lane_vec = lambda v: jnp.zeros((1, LANES), F32).at[0, AB_LANE:AB_LANE + H_GDN].set(v.astype(F32))
    w['alog'], w['dtb'] = lane_vec(a_log[0]), lane_vec(dt_bias[0])
    w['gn'] = gdn_norm[0][None]
    w['gq'], w['gkv'] = q_norm[0][None], kv_norm[0][None]
    uq = w_uq[0].reshape(Q_LORA, H_MLA, DN + DR)
    uq = jnp.concatenate([uq, jnp.zeros((Q_LORA, H_MLA, QK_PAD - DN - DR), uq.dtype)], axis=-1)
    w['w_uq'] = uq.reshape(Q_LORA, H_MLA * QK_PAD).astype(BF16)
    w['w_ukv'] = w_ukv[0].astype(BF16)
    w['w_out_g'] = w_out[0][:H_GDN * GDN_DV].astype(BF16)
    w['w_out_m'] = w_out[0][H_GDN * GDN_DV:].astype(BF16)
    return w


def kernel(x_prompt, x_sample, cache_mla_ckv, cache_mla_krope, state_gdn, state_conv, meta, ffn1_norm, ffn1_wg,
           ffn1_wu, ffn1_wd, mix_norm, w_in, conv_w, a_log, dt_bias, gdn_norm, q_norm, kv_norm, w_uq, w_ukv, w_out,
           ffn2_norm, ffn2_wg, ffn2_wu, ffn2_wd, final_norm):
    assert ffn1_wg.shape[0] == 1, "one layer: the meta rows are not carried past the mixer"
    bsz, s_len, _ = x_prompt.shape
    dbs, d_seq, _ = x_sample.shape
    past = cache_mla_ckv.shape[2]
    w = _prepare_weights(ffn1_norm, ffn1_wg, ffn1_wu, ffn1_wd, mix_norm, w_in, conv_w, a_log, dt_bias, gdn_norm,
                         q_norm, kv_norm, w_uq, w_ukv, w_out, ffn2_norm, ffn2_wg, ffn2_wu, ffn2_wd, final_norm)

    front = CHUNK - N_META
    _, qkv_m, _, mla_m = ffn_proj(meta.astype(F32), w)
    meta_qkv = jnp.pad(qkv_m, ((front, 0), (0, 0)))[None]
    x1_p, feat_p, z_p, mla_p, tail_p = ffn_proj(x_prompt.reshape(bsz * s_len, D_MODEL), w,
                                                 hist=meta_qkv[0, CHUNK - GDN_HIST:], stream_len=s_len)
    x1_s, qkv_s, z_s, mla_s = ffn_proj(x_sample.reshape(dbs * d_seq, D_MODEL), w)

    _, m_meta = gdn(meta_qkv, jnp.zeros((1, CHUNK, H_GDN * GDN_DV), F32), jnp.pad(mla_m, ((front, 0), (0, 0)))[None],
                    jnp.zeros((1, H_GDN, GDN_DK, GDN_DV), F32), jnp.zeros((1, GDN_HIST, CONV_DIM), F32), w,
                    pad_front=front)
    m0_p = jnp.broadcast_to(m_meta, (bsz,) + m_meta.shape[1:])
    gdn_p, m_p = gdn(feat_p.reshape(bsz, s_len, CONV_DIM), z_p.reshape(bsz, s_len, -1),
                     mla_p.reshape(bsz, s_len, MLA_IN), m0_p, jnp.zeros((bsz, GDN_HIST, CONV_DIM), F32), w,
                     rows=GDN_PROMPT_ROWS, prepared=True)
    conv_p = tail_p.reshape(bsz, -1, GDN_HIST, CONV_DIM)[:, -1, GDN_HIST - (CONV_W - 1):]
    qkv_s3 = qkv_s.reshape(dbs, d_seq, CONV_DIM)
    conv0_s = jnp.pad(state_conv[0].astype(F32), ((0, 0), (GDN_HIST - (CONV_W - 1), 0), (0, 0)))
    gdn_s, m_s = gdn(qkv_s3, z_s.reshape(dbs, d_seq, -1), mla_s.reshape(dbs, d_seq, MLA_IN),
                     state_gdn[0].astype(F32), conv0_s, w)

    cos_m, sin_m = _rope_tables(0, N_META)
    cos_p, sin_p = _rope_tables(N_META, s_len)
    cos_s, sin_s = _rope_tables(past, d_seq)
    q_p, c_p, kr_p, k_p, vt_p = mla_prep(mla_p, cos_p, sin_p, w, expand=True)
    _, c_m, kr_m, k_m, vt_m = mla_prep(mla_m, cos_m, sin_m, w, expand=True)
    q_s, c_s, kr_s = mla_prep(mla_s, cos_s, sin_s, w, expand=False)
    mla_o_p = mla_prompt(q_p.reshape(bsz, s_len, -1), k_p.reshape(bsz, s_len, -1), vt_p, k_m, vt_m)
    mla_o_s = mla_sample(q_s.reshape(dbs, d_seq, -1), c_s.reshape(dbs, d_seq, -1), kr_s.reshape(dbs, d_seq, -1),
                         cache_mla_ckv[0].astype(F32), cache_mla_krope[0].astype(F32), w['w_ukv'])

    y_p = out_ffn(x1_p, gdn_p.reshape(bsz * s_len, -1), mla_o_p.reshape(bsz * s_len, -1), w)
    y_s = out_ffn(x1_s, gdn_s.reshape(dbs * d_seq, -1), mla_o_s.reshape(dbs * d_seq, -1), w)

    with_meta = lambda m_rows, rows: jnp.concatenate(
        [jnp.broadcast_to(m_rows[None], (bsz,) + m_rows.shape), rows.reshape(bsz, s_len, -1)], axis=1)
    return (y_p.reshape(bsz, s_len, D_MODEL), y_s.reshape(dbs, d_seq, D_MODEL),
            with_meta(c_m, c_p)[None], with_meta(kr_m[:, :DR], kr_p[:, :DR])[None],
            m_p[None], conv_p[None],
            c_s.reshape(dbs, d_seq, KV_LORA)[None], kr_s[:, :DR].reshape(dbs, d_seq, DR)[None],
            m_s[None], qkv_s3[:, d_seq - (CONV_W - 1):][None])
```

```python
import functools

import jax
import jax.numpy as jnp
import numpy as np
from jax import lax
from jax.experimental import pallas as pl
from jax.experimental.pallas import tpu as pltpu

F32 = jnp.float32
BF16 = jnp.bfloat16

D_MODEL = 1024
CHUNK = 64
N_META = 16
H_GDN = 4
GDN_DK = 128
GDN_DV = 128
CONV_W = 4
CONV_DIM = H_GDN * (2 * GDN_DK + GDN_DV)
H_MLA = 4
Q_LORA = 384
KV_LORA = 256
DN = 128
DR = 64
DV_MLA = 128
ROPE_BASE = 10000.0
SM_SCALE = (DN + DR) ** -0.5
LOG2_E = 1.4426950408889634
Q_SCALE = SM_SCALE * LOG2_E
D_FF = 2816
EPS = 1e-6
L2_EPS = 1e-6

LANES = 128
SUBLANES = 8
FF_CHUNK = 256
MLA_IN = Q_LORA + KV_LORA + LANES
QK_PAD = 2 * LANES
AB_LANE = DR
VMEM_LIMIT = 56 * 1024 * 1024

NT_DIMS = (((1,), (1,)), ((), ()))


def _rms(x, g):
    return x * lax.rsqrt(jnp.mean(x * x, axis=-1, keepdims=True) + EPS) * g


def _bdot(a, b):
    return jnp.dot(a.astype(BF16), b.astype(BF16), preferred_element_type=F32)


def _bdot_nt(a, b):
    return lax.dot_general(a.astype(BF16), b.astype(BF16), NT_DIMS, preferred_element_type=F32)


def _const_spec(shape):
    nd = len(shape)
    return pl.BlockSpec(shape, lambda *_: (0,) * nd, pipeline_mode=pl.Buffered(1))


def _params(*sem):
    return pltpu.CompilerParams(dimension_semantics=sem, vmem_limit_bytes=VMEM_LIMIT)


def _drain(*stage_gens):
    live = [[g, 0.0, float(t)] for g, t in (sg if isinstance(sg, tuple) else (sg, 1.0) for sg in stage_gens)]
    while live:
        entry = min(live, key=lambda e: e[1] / e[2])
        try:
            entry[1] += next(entry[0]) or 0.0
        except StopIteration:
            live.remove(entry)


def _swiglu_half_stages(x, g_ref, wg_ref, wu_ref, wd_ref, h_ref, acc_ref, result):
    h_ref[...] = _rms(x, g_ref[...]).astype(BF16)
    nf = wg_ref.shape[1] // FF_CHUNK

    def gate_up(f):
        cols = slice(f * FF_CHUNK, (f + 1) * FF_CHUNK)
        gate = jnp.dot(h_ref[...], wg_ref[:, cols], preferred_element_type=F32)
        up = jnp.dot(h_ref[...], wu_ref[:, cols], preferred_element_type=F32)
        return gate, up

    nxt = gate_up(0)
    yield 1.0
    for f in range(nf):
        gate, up = nxt
        if f + 1 < nf:
            nxt = gate_up(f + 1)
            yield 1.0
        act = (jax.nn.silu(gate) * up).astype(BF16)
        down = jnp.dot(act, wd_ref[f * FF_CHUNK:(f + 1) * FF_CHUNK, :], preferred_element_type=F32)
        if f == 0:
            acc_ref[...] = down
        else:
            acc_ref[...] += down
        yield 1.0
    result.append(x + 0.5 * acc_ref[...])


def _swiglu_half(x, g_ref, wg_ref, wu_ref, wd_ref, h_ref, acc_ref):
    result = []
    _drain(_swiglu_half_stages(x, g_ref, wg_ref, wu_ref, wd_ref, h_ref, acc_ref, result))
    return result[0]


def _ffn_proj_kernel(x_ref, g1_ref, wg_ref, wu_ref, wd_ref, gm_ref, wqkv_ref, wz_ref, wmla_ref, *rest,
                     stream_tiles):
    if not stream_tiles:
        x1_ref, qkv_ref, z_ref, mla_ref, h_ref, acc_ref = rest
        x1 = _swiglu_half(x_ref[...], g1_ref, wg_ref, wu_ref, wd_ref, h_ref, acc_ref)
    else:
        hist_ref, convw_ref, x1_ref, qkv_ref, z_ref, mla_ref, tail_ref, h_ref, acc_ref, conv_scr = rest
        i = pl.program_id(0)
        tm = x_ref.shape[0]

        @pl.when(i == 0)
        def _():
            conv_scr[...] = jnp.zeros_like(conv_scr)

        @pl.when((i + stream_tiles - 1) % stream_tiles == 0)
        def _():
            conv_scr[0, 0:GDN_HIST, :] = hist_ref[...]

        def feature_stages():
            for row0 in range(0, tm, CHUNK):
                qkv_ref[row0:row0 + CHUNK, :] = _gdn_features(conv_scr, convw_ref, 0, row0)
                yield 1.0
            conv_scr[0, 0:GDN_HIST, :] = conv_scr[0, tm:tm + GDN_HIST, :]

        result = []
        _drain((_swiglu_half_stages(x_ref[...], g1_ref, wg_ref, wu_ref, wd_ref, h_ref, acc_ref, result),
                2 * (D_FF // FF_CHUNK)), (feature_stages(), tm // CHUNK))
        x1 = result[0]
    x1_ref[...] = x1
    h_ref[...] = _rms(x1, gm_ref[...]).astype(BF16)
    qkv = jnp.dot(h_ref[...], wqkv_ref[...], preferred_element_type=F32)
    z_ref[...] = jnp.dot(h_ref[...], wz_ref[...], preferred_element_type=F32)
    mla_ref[...] = jnp.dot(h_ref[...], wmla_ref[...], preferred_element_type=F32)
    if stream_tiles:
        conv_scr[0, GDN_HIST:GDN_HIST + tm, :] = qkv
        tail_ref[0] = qkv[tm - GDN_HIST:, :]
    else:
        qkv_ref[...] = qkv


def _row_tile(n, want):
    t = min(want, n)
    assert n % t == 0, (n, t)
    return t


def ffn_proj(x, w, hist=None, stream_len=None, tm=512):
    n = x.shape[0]
    tm = _row_tile(n, tm)
    ntile = n // tm
    stream_tiles = 0
    if hist is not None:
        assert stream_len % tm == 0 and tm % CHUNK == 0
        stream_tiles = stream_len // tm
    tile = (lambda i: jnp.minimum(i, ntile - 1)) if stream_tiles else (lambda i: i)
    row = lambda width: pl.BlockSpec((tm, width), lambda i: (tile(i), 0))
    consts = (w['g1'], w['wg1'], w['wu1'], w['wd1'], w['gm'], w['w_qkv'], w['w_z'], w['w_mla'])
    out_specs = [row(D_MODEL), row(CONV_DIM), row(H_GDN * GDN_DV), row(MLA_IN)]
    out_shape = [jax.ShapeDtypeStruct((n, D_MODEL), F32), jax.ShapeDtypeStruct((n, CONV_DIM), F32),
                 jax.ShapeDtypeStruct((n, H_GDN * GDN_DV), F32), jax.ShapeDtypeStruct((n, MLA_IN), F32)]
    scratch = [pltpu.VMEM((tm, D_MODEL), BF16), pltpu.VMEM((tm, D_MODEL), F32)]
    if stream_tiles:
        consts += (hist, w['conv_w'])
        out_specs[1] = pl.BlockSpec((tm, CONV_DIM), lambda i: (jnp.maximum(i - 1, 0), 0))
        out_specs.append(pl.BlockSpec((1, GDN_HIST, CONV_DIM), lambda i: (tile(i), 0, 0)))
        out_shape.append(jax.ShapeDtypeStruct((ntile, GDN_HIST, CONV_DIM), F32))
        scratch.append(pltpu.VMEM((1, GDN_HIST + tm, CONV_DIM), F32))
    return pl.pallas_call(
        functools.partial(_ffn_proj_kernel, stream_tiles=stream_tiles),
        grid=(ntile + (1 if stream_tiles else 0),),
        in_specs=[row(D_MODEL)] + [_const_spec(c.shape) for c in consts],
        out_specs=out_specs,
        out_shape=out_shape,
        scratch_shapes=scratch,
        compiler_params=_params("arbitrary" if stream_tiles else "parallel"),
        name="ffn_proj",
    )(x, *consts)


def _out_ffn_kernel(x1_ref, gdn_ref, mla_ref, wog_ref, wom_ref, g2_ref, wg_ref, wu_ref, wd_ref, gf_ref,
                    y_ref, h_ref, acc_ref):
    x2 = (x1_ref[...] + jnp.dot(gdn_ref[...], wog_ref[...], preferred_element_type=F32)
          + jnp.dot(mla_ref[...], wom_ref[...], preferred_element_type=F32))
    x3 = _swiglu_half(x2, g2_ref, wg_ref, wu_ref, wd_ref, h_ref, acc_ref)
    y_ref[...] = _rms(x3, gf_ref[...])


def out_ffn(x1, gdn, mla, w, tm=512):
    n = x1.shape[0]
    tm = _row_tile(n, tm)
    row = lambda width: pl.BlockSpec((tm, width), lambda i: (i, 0))
    consts = (w['w_out_g'], w['w_out_m'], w['g2'], w['wg2'], w['wu2'], w['wd2'], w['gf'])
    return pl.pallas_call(
        _out_ffn_kernel,
        grid=(n // tm,),
        in_specs=[row(D_MODEL), row(H_GDN * GDN_DV), row(H_MLA * DV_MLA)] + [_const_spec(c.shape) for c in consts],
        out_specs=row(D_MODEL),
        out_shape=jax.ShapeDtypeStruct((n, D_MODEL), F32),
        scratch_shapes=[pltpu.VMEM((tm, D_MODEL), BF16), pltpu.VMEM((tm, D_MODEL), F32)],
        compiler_params=_params("parallel"),
        name="out_ffn",
    )(x1, gdn, mla, *consts)


def _cumsum_rows(x):
    n = x.shape[0]
    row = lax.broadcasted_iota(jnp.int32, x.shape, 0)
    shift = 1
    while shift < n:
        x = x + jnp.where(row >= shift, pltpu.roll(x, shift, 0), 0.0)
        shift *= 2
    return x


def _transpose_rows(x):
    length = x.shape[0]
    sq = jnp.concatenate([x, jnp.zeros((LANES - length, LANES), x.dtype)], axis=0)
    return sq.T[:, :length]


GDN_HIST = SUBLANES
GDN_PREP_COST = 4.0
GDN_PROMPT_ROWS = 2 * CHUNK
GDN_SAMPLE_GROUP = 8


def _gdn_features(conv_scr, convw_ref, i, row0):
    base = GDN_HIST - (CONV_W - 1) + row0
    parts = []
    for h in range(CONV_DIM // GDN_DK):
        cols = slice(h * GDN_DK, (h + 1) * GDN_DK)
        x = conv_scr[i, pl.ds(base, CHUNK), cols] * convw_ref[0:1, cols]
        for j in range(1, CONV_W):
            x = x + conv_scr[i, pl.ds(base + j, CHUNK), cols] * convw_ref[j:j + 1, cols]
        x = jax.nn.silu(x)
        if h < 2 * H_GDN:
            x = x * lax.rsqrt(jnp.sum(x * x, axis=-1, keepdims=True) + L2_EPS)
        parts.append(x * (GDN_DK ** -0.5) if h < H_GDN else x)
    return jnp.concatenate(parts, axis=1)


def _gdn_block_stages(first_block, rows, pad_front, features, ab_ref, z_ref, write_o, m_scr,
                      alog_ref, dtb_ref, gn_ref):
    bg = ab_ref.shape[0]
    length = CHUNK
    row = lax.broadcasted_iota(jnp.int32, (length, LANES), 0)
    ri = lax.broadcasted_iota(jnp.int32, (length, length), 0)
    ci = lax.broadcasted_iota(jnp.int32, (length, length), 1)
    causal = ci <= ri
    strict = ci < ri

    def per_batch(i, row0):
        u = features(i, row0)
        ab = ab_ref[i, row0:row0 + length, :]
        g_all = -jnp.exp(alog_ref[...]) * jax.nn.softplus(ab + dtb_ref[...])
        beta_all = jax.nn.sigmoid(ab)
        if pad_front > row0:
            valid = jnp.logical_or(jnp.logical_not(first_block), row >= pad_front - row0)
            g_all = jnp.where(valid, g_all, 0.0)
            beta_all = jnp.where(valid, beta_all, 0.0)
        gc_all = _cumsum_rows(g_all)
        gc_t = _transpose_rows(gc_all)

        chains = []
        for h in range(H_GDN):
            q = u[:, h * GDN_DK:(h + 1) * GDN_DK]
            k = u[:, (H_GDN + h) * GDN_DK:(H_GDN + h + 1) * GDN_DK]
            v = u[:, 2 * H_GDN * GDN_DK + h * GDN_DV:2 * H_GDN * GDN_DK + (h + 1) * GDN_DV]
            gc = gc_all[:, AB_LANE + h:AB_LANE + h + 1]
            beta = beta_all[:, AB_LANE + H_GDN + h:AB_LANE + H_GDN + h + 1]
            gc_row = gc_t[AB_LANE + h:AB_LANE + h + 1, :]
            decay = jnp.where(causal, jnp.exp(jnp.where(causal, gc - gc_row, 0.0)), 0.0)
            kq = jnp.concatenate([k, q], axis=0).astype(BF16)
            chains.append(dict(i=i, h=h, row0=row0, k=k, v=v, gc=gc, beta=beta, decay=decay, kq=kq))
        return chains

    per_chunk = []
    for row0 in range(0, rows, length):
        per_chunk.append([])
        for i in range(bg):
            per_chunk[-1] += per_batch(i, row0)
            yield GDN_PREP_COST
    chains = [ch for chunk in per_chunk for ch in chunk]
    for ch in chains:
        ch['kk_qk'] = lax.dot_general(ch['kq'], ch['k'].astype(BF16), NT_DIMS, preferred_element_type=F32)
    yield 1.0
    for ch in chains:
        ch['a'] = jnp.where(strict, ch['beta'] * ch['kk_qk'][:length] * ch['decay'], 0.0)
        ch['qk_decay'] = ch['kk_qk'][length:] * ch['decay']
        ch['y'] = -ch['a']
    for ch in chains:
        ch['pw'] = _bdot(ch['a'], ch['a'])
    yield 1.0
    span = 2
    while span < length:
        span *= 2
        for ch in chains:
            ch['y'] = ch['y'] + ch['pw'] + _bdot(ch['y'], ch['pw'])
        yield 1.0
        if span < length:
            for ch in chains:
                ch['pw'] = _bdot(ch['pw'], ch['pw'])
            yield 1.0

    def emit_output(chunk):
        for ch in chunk:
            o = ch['eg'] * ch['kqm'][length:] + _bdot(ch['qk_decay'], ch['uu'])
            n = _rms(o, gn_ref[...])
            rws = slice(ch['row0'], ch['row0'] + length)
            cols = slice(ch['h'] * GDN_DV, (ch['h'] + 1) * GDN_DV)
            write_o(ch['i'], rws, cols, (n * jax.nn.silu(z_ref[ch['i'], rws, cols])).astype(BF16))

    pending = None
    for chunk in per_chunk:
        for ch in chunk:
            ch['m0'] = m_scr[ch['i'], ch['h']]
            ch['kqm'] = jnp.dot(ch['kq'], ch['m0'].astype(BF16), preferred_element_type=F32)
        yield 1.0
        if pending is not None:
            emit_output(pending)
            yield 1.0
        for ch in chunk:
            ch['eg'] = jnp.exp(ch['gc'])
            rhs = ch['beta'] * (ch['v'] - ch['eg'] * ch['kqm'][:length])
            ch['uu'] = rhs + _bdot(ch['y'], rhs)
        yield 1.0
        for ch in chunk:
            g_last = ch['gc'][length - 1:length, :]
            k_dec = ch['k'] * jnp.exp(g_last - ch['gc'])
            m_scr[ch['i'], ch['h']] = jnp.exp(g_last) * ch['m0'] + _bdot(_transpose_rows(k_dec), ch['uu'])
        yield 1.0
        pending = chunk
    emit_output(pending)
    yield 1.0


def _gdn_stage_block(qkv_ref, conv_scr):
    bg, rows, _ = qkv_ref.shape
    for i in range(bg):
        conv_scr[i, GDN_HIST:GDN_HIST + rows, :] = qkv_ref[i]


def _gdn_keep_history(conv_scr, rows):
    for i in range(conv_scr.shape[0]):
        conv_scr[i, 0:GDN_HIST, :] = conv_scr[i, rows:rows + GDN_HIST, :]


def _gdn_kernel(qkv_ref, z_ref, ab_ref, m0_ref, conv0_ref, convw_ref, alog_ref, dtb_ref, gn_ref,
                o_ref, mout_ref, conv_scr, m_scr, *, pad_front, prepared):
    c = pl.program_id(1)
    rows = qkv_ref.shape[1]

    @pl.when(c == 0)
    def _():
        m_scr[...] = m0_ref[...]
        conv_scr[:, 0:GDN_HIST, :] = conv0_ref[...]

    def write_o(i, rws, cols, val):
        o_ref[i, rws, cols] = val

    if prepared:
        features = lambda i, row0: qkv_ref[i, row0:row0 + CHUNK, :]
    else:
        _gdn_stage_block(qkv_ref, conv_scr)
        features = functools.partial(_gdn_features, conv_scr, convw_ref)
    _drain(_gdn_block_stages(c == 0, rows, pad_front, features, ab_ref, z_ref, write_o, m_scr,
                             alog_ref, dtb_ref, gn_ref))
    if not prepared:
        _gdn_keep_history(conv_scr, rows)

    @pl.when(c == pl.num_programs(1) - 1)
    def _():
        mout_ref[...] = m_scr[...]


def _gdn_in_specs(bg, rows, blk):
    nd = len(blk(0, 0))
    const = lambda shape: pl.BlockSpec(shape, lambda *_: (0,) * len(shape))
    return [
        pl.BlockSpec((bg, rows, CONV_DIM), blk),
        pl.BlockSpec((bg, rows, H_GDN * GDN_DV), blk),
        pl.BlockSpec((bg, rows, LANES), lambda *a: blk(*a)[:nd - 1] + (MLA_IN // LANES - 1,)),
        pl.BlockSpec((bg, H_GDN, GDN_DK, GDN_DV), lambda g, c: (g, 0, 0, 0)),
        pl.BlockSpec((bg, GDN_HIST, CONV_DIM), lambda g, c: (g, 0, 0)),
        const((CONV_W, CONV_DIM)),
        const((1, LANES)),
        const((1, LANES)),
        const((1, GDN_DV)),
    ]


def gdn(qkv, z, mla_in, m0, conv0, w, *, pad_front=0, bg=4, rows=CHUNK, prepared=False):
    b, t, _ = qkv.shape
    bg = min(bg, b)
    assert b % bg == 0 and t % rows == 0 and rows % CHUNK == 0
    blk = lambda g, c: (g, c, 0)
    return pl.pallas_call(
        functools.partial(_gdn_kernel, pad_front=pad_front, prepared=prepared),
        grid=(b // bg, t // rows),
        in_specs=_gdn_in_specs(bg, rows, blk),
        out_specs=[
            pl.BlockSpec((bg, rows, H_GDN * GDN_DV), blk),
            pl.BlockSpec((bg, H_GDN, GDN_DK, GDN_DV), lambda g, c: (g, 0, 0, 0)),
        ],
        out_shape=[jax.ShapeDtypeStruct((b, t, H_GDN * GDN_DV), BF16),
                   jax.ShapeDtypeStruct((b, H_GDN, GDN_DK, GDN_DV), F32)],
        scratch_shapes=[pltpu.VMEM((bg, GDN_HIST + (0 if prepared else rows), CONV_DIM), F32),
                        pltpu.VMEM((bg, H_GDN, GDN_DK, GDN_DV), F32)],
        compiler_params=_params("parallel", "arbitrary"),
        name="gdn",
    )(qkv, z, mla_in, m0, conv0, w['conv_w'], w['alog'], w['dtb'], w['gn'])


def _rope(t, cos, sin):
    lane = lax.broadcasted_iota(jnp.int32, t.shape, 1)
    half = DR // 2
    swapped = jnp.where(lane < half, pltpu.roll(t, LANES - half, 1), pltpu.roll(t, half, 1))
    return t * cos + swapped * sin


def _mla_prep_kernel(x_ref, cos_ref, sin_ref, gq_ref, gkv_ref, wuq_ref, wukv_ref, *out_refs, expand):
    q_ref, c_ref, kr_ref = out_refs[:3]
    x = x_ref[...]
    cos = cos_ref[...]
    sin = sin_ref[...]
    q = _bdot(_rms(x[:, :Q_LORA], gq_ref[...]), wuq_ref[...])
    for h in range(H_MLA):
        lo = h * QK_PAD
        q_ref[:, lo:lo + DN] = (q[:, lo:lo + DN] * Q_SCALE).astype(BF16)
        q_ref[:, lo + DN:lo + QK_PAD] = (_rope(q[:, lo + DN:lo + QK_PAD], cos, sin) * Q_SCALE).astype(BF16)
    c = _rms(x[:, Q_LORA:Q_LORA + KV_LORA], gkv_ref[...])
    c_ref[...] = c
    kr = _rope(x[:, Q_LORA + KV_LORA:], cos, sin)
    if not expand:
        kr_ref[...] = kr
    else:
        kr_ref[...] = (kr.T if kr.shape[0] % LANES == 0 else _transpose_rows(kr))[:DR]
        k_ref, vt_ref = out_refs[3:]
        kv = _bdot(c, wukv_ref[...])
        for h in range(H_MLA):
            lo = h * (DN + DV_MLA)
            k_ref[:, h * QK_PAD:h * QK_PAD + DN] = kv[:, lo:lo + DN].astype(BF16)
            k_ref[:, h * QK_PAD + DN:(h + 1) * QK_PAD] = kr.astype(BF16)
            v = kv[:, lo + DN:lo + DN + DV_MLA]
            v_t = v.T if v.shape[0] % LANES == 0 else _transpose_rows(v)
            vt_ref[h * DV_MLA:(h + 1) * DV_MLA, :] = v_t.astype(BF16)


def mla_prep(mla_in, cos, sin, w, *, expand, tm=512):
    n = mla_in.shape[0]
    tm = _row_tile(min(n, cos.shape[0]), tm)
    nrep = cos.shape[0] // tm
    row = lambda width: pl.BlockSpec((tm, width), lambda i: (i, 0))
    tab = pl.BlockSpec((tm, LANES), lambda i: (i % nrep, 0))
    consts = (w['gq'], w['gkv'], w['w_uq'], w['w_ukv'])
    out_specs = [row(H_MLA * QK_PAD), row(KV_LORA), row(LANES)]
    out_shape = [jax.ShapeDtypeStruct((n, H_MLA * QK_PAD), BF16), jax.ShapeDtypeStruct((n, KV_LORA), F32),
                 jax.ShapeDtypeStruct((n, LANES), F32)]
    if expand:
        out_specs[2] = pl.BlockSpec((DR, tm), lambda i: (0, i))
        out_shape[2] = jax.ShapeDtypeStruct((DR, n), F32)
        out_specs += [row(H_MLA * QK_PAD), pl.BlockSpec((H_MLA * DV_MLA, tm), lambda i: (0, i))]
        out_shape += [jax.ShapeDtypeStruct((n, H_MLA * QK_PAD), BF16), jax.ShapeDtypeStruct((H_MLA * DV_MLA, n), BF16)]
    return pl.pallas_call(
        functools.partial(_mla_prep_kernel, expand=expand),
        grid=(n // tm,),
        in_specs=[row(MLA_IN), tab, tab] + [_const_spec(c.shape) for c in consts],
        out_specs=out_specs,
        out_shape=out_shape,
        compiler_params=_params("parallel"),
        name="mla_prep_kv" if expand else "mla_prep",
    )(mla_in, cos, sin, *consts)


def _mla_prompt_kernel(q_ref, k_ref, vt_ref, km_ref, vmt_ref, o_ref, s_scr, m_scr, l_scr, acc_scr):
    qi = pl.program_id(1)
    tq = q_ref.shape[0]
    heads = range(H_MLA)

    def scores(off, h):
        return lax.dot_general(k_ref[pl.ds(off, tq), h * QK_PAD:(h + 1) * QK_PAD],
                               q_ref[:, h * QK_PAD:(h + 1) * QK_PAD], NT_DIMS, preferred_element_type=F32)

    def consume(h, off, masked):
        s_t = s_scr[h]
        if masked:
            key_chunk = lax.broadcasted_iota(jnp.int32, (tq, tq), 0) // CHUNK
            qry_chunk = lax.broadcasted_iota(jnp.int32, (tq, tq), 1) // CHUNK
            s_t = jnp.where(key_chunk <= qry_chunk, s_t, -jnp.inf)
        m = m_scr[h]
        m_new = jnp.maximum(m, jnp.max(s_t, axis=0, keepdims=True))
        alpha = jnp.exp2(m - m_new)
        p_t = jnp.exp2(s_t - m_new)
        m_scr[h] = m_new
        l_scr[h] = alpha * l_scr[h] + jnp.sum(p_t, axis=0, keepdims=True)
        acc_scr[h] = alpha * acc_scr[h] + jnp.dot(vt_ref[h * DV_MLA:(h + 1) * DV_MLA, pl.ds(off, tq)],
                                                  p_t.astype(BF16), preferred_element_type=F32)

    meta_s = [lax.dot_general(km_ref[:, h * QK_PAD:(h + 1) * QK_PAD], q_ref[:, h * QK_PAD:(h + 1) * QK_PAD],
                              NT_DIMS, preferred_element_type=F32) for h in heads]
    for h in heads:
        s_scr[h] = scores(0, h)
    meta_p = []
    for h in heads:
        m = jnp.max(meta_s[h], axis=0, keepdims=True)
        p_t = jnp.exp2(meta_s[h] - m)
        m_scr[h] = m
        l_scr[h] = jnp.sum(p_t, axis=0, keepdims=True)
        meta_p.append(p_t.astype(BF16))
    for h in heads:
        acc_scr[h] = jnp.dot(vmt_ref[h * DV_MLA:(h + 1) * DV_MLA, :], meta_p[h], preferred_element_type=F32)

    def full_tile(t):
        off = pl.multiple_of(t * tq, tq)
        nxt = pl.multiple_of((t + 1) * tq, tq)
        for h in heads:
            s_next = scores(nxt, h)
            consume(h, off, False)
            s_scr[h] = s_next

    def tile_pair(t2, carry):
        full_tile(2 * t2)
        full_tile(2 * t2 + 1)
        return carry

    lax.fori_loop(0, qi // 2, tile_pair, 0)

    @pl.when(qi % 2 == 1)
    def _():
        full_tile(qi - 1)

    for h in heads:
        consume(h, pl.multiple_of(qi * tq, tq), True)
    for h in heads:
        o_ref[:, h * DV_MLA:(h + 1) * DV_MLA] = (acc_scr[h] / l_scr[h]).T.astype(o_ref.dtype)


def mla_prompt(q, k, v_t, k_meta, v_meta_t, tq=512):
    b, s, _ = q.shape
    tq = _row_tile(s, tq)
    assert tq % LANES == 0
    whole = lambda shape: pl.BlockSpec(shape, lambda bi, i: (0, 0))
    return pl.pallas_call(
        _mla_prompt_kernel,
        grid=(b, s // tq),
        in_specs=[
            pl.BlockSpec((None, tq, H_MLA * QK_PAD), lambda bi, i: (bi, i, 0)),
            pl.BlockSpec((None, s, H_MLA * QK_PAD), lambda bi, i: (bi, 0, 0), pipeline_mode=pl.Buffered(1)),
            pl.BlockSpec((H_MLA * DV_MLA, s), lambda bi, i: (0, bi), pipeline_mode=pl.Buffered(1)),
            whole(k_meta.shape),
            whole(v_meta_t.shape),
        ],
        out_specs=pl.BlockSpec((None, tq, H_MLA * DV_MLA), lambda bi, i: (bi, i, 0)),
        out_shape=jax.ShapeDtypeStruct((b, s, H_MLA * DV_MLA), BF16),
        scratch_shapes=[pltpu.VMEM((H_MLA, tq, tq), F32), pltpu.VMEM((H_MLA, 1, tq), F32),
                        pltpu.VMEM((H_MLA, 1, tq), F32), pltpu.VMEM((H_MLA, DV_MLA, tq), F32)],
        compiler_params=_params("parallel", "arbitrary"),
        name="mla_prompt",
    )(q, k, v_t, k_meta, v_meta_t)


def _mla_sample_kernel(q_ref, c_ref, kr_ref, cc_ref, ckr_ref, wukv_ref, o_ref):
    nreq, t, _ = q_ref.shape
    w = wukv_ref[...]
    reqs = []
    for r in range(nreq):
        q = q_ref[r]
        qa = jnp.concatenate(
            [lax.dot_general(q[:, h * QK_PAD:h * QK_PAD + DN], w[:, h * (DN + DV_MLA):h * (DN + DV_MLA) + DN],
                             NT_DIMS, preferred_element_type=F32) for h in range(H_MLA)], axis=0).astype(BF16)
        qr = jnp.concatenate([q[:, h * QK_PAD + DN:(h + 1) * QK_PAD] for h in range(H_MLA)], axis=0)
        reqs.append(dict(qa=qa, qr=qr, cache_c=cc_ref[r].astype(BF16), own_c=c_ref[r].astype(BF16)))
    for r, rq in enumerate(reqs):
        rq['s_cache'] = (lax.dot_general(rq['qa'], rq['cache_c'], NT_DIMS, preferred_element_type=F32)
                         + jnp.dot(rq['qr'][:, :DR], ckr_ref[r].astype(BF16), preferred_element_type=F32))
        rq['s_own'] = (lax.dot_general(rq['qa'], rq['own_c'], NT_DIMS, preferred_element_type=F32)
                       + lax.dot_general(rq['qr'], kr_ref[r].astype(BF16), NT_DIMS, preferred_element_type=F32))
    for rq in reqs:
        m = jnp.maximum(jnp.max(rq['s_cache'], axis=-1, keepdims=True), jnp.max(rq['s_own'], axis=-1, keepdims=True))
        p_cache = jnp.exp2(rq['s_cache'] - m)
        p_own = jnp.exp2(rq['s_own'] - m)
        rq['l'] = jnp.sum(p_cache, axis=-1, keepdims=True) + jnp.sum(p_own, axis=-1, keepdims=True)
        rq['p_cache'], rq['p_own'] = p_cache.astype(BF16), p_own.astype(BF16)
    for rq in reqs:
        rq['pc'] = (jnp.dot(rq['p_cache'], rq['cache_c'], preferred_element_type=F32)
                    + jnp.dot(rq['p_own'], rq['own_c'], preferred_element_type=F32)) / rq['l']
    for r, rq in enumerate(reqs):
        for h in range(H_MLA):
            lo = h * (DN + DV_MLA) + DN
            o_ref[r, :, h * DV_MLA:(h + 1) * DV_MLA] = _bdot(rq['pc'][h * t:(h + 1) * t],
                                                             w[:, lo:lo + DV_MLA]).astype(o_ref.dtype)


def mla_sample(q, c, kr, cache_c, cache_kr, w_ukv, group=2):
    b, t, _ = q.shape
    p = cache_c.shape[1]
    group = min(group, b)
    assert b % group == 0
    per_b = lambda rows, width: pl.BlockSpec((group, rows, width), lambda bi: (bi, 0, 0))
    return pl.pallas_call(
        _mla_sample_kernel,
        grid=(b // group,),
        in_specs=[per_b(t, H_MLA * QK_PAD), per_b(t, KV_LORA), per_b(t, LANES), per_b(p, KV_LORA), per_b(DR, p),
                  _const_spec(w_ukv.shape)],
        out_specs=per_b(t, H_MLA * DV_MLA),
        out_shape=jax.ShapeDtypeStruct((b, t, H_MLA * DV_MLA), BF16),
        compiler_params=_params("parallel"),
        name="mla_sample",
    )(q, c, kr, cache_c, cache_kr, w_ukv)


def _rope_tables(first, count):
    inv = ROPE_BASE ** (-np.arange(0, DR, 2, dtype=np.float64) / DR)
    ang = (first + np.arange(count, dtype=np.float64))[:, None] * inv[None, :]
    cos, sin = np.cos(ang), np.sin(ang)
    pad = np.zeros((count, LANES - DR))
    return (jnp.asarray(np.concatenate([cos, cos, pad], axis=1), F32),
            jnp.asarray(np.concatenate([-sin, sin, pad], axis=1), F32))


def _prepare_weights(ffn1_norm, ffn1_wg, ffn1_wu, ffn1_wd, mix_norm, w_in, conv_w, a_log, dt_bias, gdn_norm,
                     q_norm, kv_norm, w_uq, w_ukv, w_out, ffn2_norm, ffn2_wg, ffn2_wu, ffn2_wd, final_norm):
    w = {}
    w['g1'], w['gm'], w['g2'] = ffn1_norm[0][None], mix_norm[0][None], ffn2_norm[0][None]
    w['gf'] = final_norm[None]
    assert D_FF % FF_CHUNK == 0
    w['wg1'], w['wu1'], w['wd1'] = (m[0].astype(BF16) for m in (ffn1_wg, ffn1_wu, ffn1_wd))
    w['wg2'], w['wu2'], w['wd2'] = (m[0].astype(BF16) for m in (ffn2_wg, ffn2_wu, ffn2_wd))
    o_z = CONV_DIM
    o_a = o_z + H_GDN * GDN_DV
    o_b = o_a + H_GDN
    o_cq = o_b + H_GDN
    o_kr = o_cq + Q_LORA + KV_LORA
    wi = w_in[0]
    w['w_qkv'] = wi[:, :o_z].astype(BF16)
    w['w_z'] = wi[:, o_z:o_a].astype(BF16)
    tail_pad = jnp.zeros((D_MODEL, LANES - DR - 2 * H_GDN), wi.dtype)
    w['w_mla'] = jnp.concatenate([wi[:, o_cq:o_kr], wi[:, o_kr:o_kr + DR], wi[:, o_a:o_cq], tail_pad], axis=1).astype(BF16)
    w['conv_w'] = conv_w[0]
    lane_vec = lambda v: jnp.zeros((1, LANES), F32).at[0, AB_LANE:AB_LANE + H_GDN].set(v.astype(F32))
    w['alog'], w['dtb'] = lane_vec(a_log[0]), lane_vec(dt_bias[0])
    w['gn'] = gdn_norm[0][None]
    w['gq'], w['gkv'] = q_norm[0][None], kv_norm[0][None]
    uq = w_uq[0].reshape(Q_LORA, H_MLA, DN + DR)
    uq = jnp.concatenate([uq, jnp.zeros((Q_LORA, H_MLA, QK_PAD - DN - DR), uq.dtype)], axis=-1)
    w['w_uq'] = uq.reshape(Q_LORA, H_MLA * QK_PAD).astype(BF16)
    w['w_ukv'] = w_ukv[0].astype(BF16)
    w['w_out_g'] = w_out[0][:H_GDN * GDN_DV].astype(BF16)
    w['w_out_m'] = w_out[0][H_GDN * GDN_DV:].astype(BF16)
    return w


def kernel(x_prompt, x_sample, cache_mla_ckv, cache_mla_krope, state_gdn, state_conv, meta, ffn1_norm, ffn1_wg,
           ffn1_wu, ffn1_wd, mix_norm, w_in, conv_w, a_log, dt_bias, gdn_norm, q_norm, kv_norm, w_uq, w_ukv, w_out,
           ffn2_norm, ffn2_wg, ffn2_wu, ffn2_wd, final_norm):
    assert ffn1_wg.shape[0] == 1, "one layer: the meta rows are not carried past the mixer"
    bsz, s_len, _ = x_prompt.shape
    dbs, d_seq, _ = x_sample.shape
    past = cache_mla_ckv.shape[2]
    w = _prepare_weights(ffn1_norm, ffn1_wg, ffn1_wu, ffn1_wd, mix_norm, w_in, conv_w, a_log, dt_bias, gdn_norm,
                         q_norm, kv_norm, w_uq, w_ukv, w_out, ffn2_norm, ffn2_wg, ffn2_wu, ffn2_wd, final_norm)

    front = CHUNK - N_META
    _, qkv_m, _, mla_m = ffn_proj(meta.astype(F32), w)
    meta_qkv = jnp.pad(qkv_m, ((front, 0), (0, 0)))[None]
    x1_p, feat_p, z_p, mla_p, tail_p = ffn_proj(x_prompt.reshape(bsz * s_len, D_MODEL), w,
                                                 hist=meta_qkv[0, CHUNK - GDN_HIST:], stream_len=s_len)
    x1_s, qkv_s, z_s, mla_s = ffn_proj(x_sample.reshape(dbs * d_seq, D_MODEL), w)

    _, m_meta = gdn(meta_qkv, jnp.zeros((1, CHUNK, H_GDN * GDN_DV), F32), jnp.pad(mla_m, ((front, 0), (0, 0)))[None],
                    jnp.zeros((1, H_GDN, GDN_DK, GDN_DV), F32), jnp.zeros((1, GDN_HIST, CONV_DIM), F32), w,
                    pad_front=front)
    m0_p = jnp.broadcast_to(m_meta, (bsz,) + m_meta.shape[1:])
    gdn_p, m_p = gdn(feat_p.reshape(bsz, s_len, CONV_DIM), z_p.reshape(bsz, s_len, -1),
                     mla_p.reshape(bsz, s_len, MLA_IN), m0_p, jnp.zeros((bsz, GDN_HIST, CONV_DIM), F32), w,
                     rows=GDN_PROMPT_ROWS, prepared=True)
    conv_p = tail_p.reshape(bsz, -1, GDN_HIST, CONV_DIM)[:, -1, GDN_HIST - (CONV_W - 1):]
    qkv_s3 = qkv_s.reshape(dbs, d_seq, CONV_DIM)
    conv0_s = jnp.pad(state_conv[0].astype(F32), ((0, 0), (GDN_HIST - (CONV_W - 1), 0), (0, 0)))
    gdn_s, m_s = gdn(qkv_s3, z_s.reshape(dbs, d_seq, -1), mla_s.reshape(dbs, d_seq, MLA_IN),
                     state_gdn[0].astype(F32), conv0_s, w, bg=GDN_SAMPLE_GROUP)

    cos_m, sin_m = _rope_tables(0, N_META)
    cos_p, sin_p = _rope_tables(N_META, s_len)
    cos_s, sin_s = _rope_tables(past, d_seq)
    q_p, c_p, kr_p, k_p, vt_p = mla_prep(mla_p, cos_p, sin_p, w, expand=True)
    _, c_m, kr_m, k_m, vt_m = mla_prep(mla_m, cos_m, sin_m, w, expand=True)
    q_s, c_s, kr_s = mla_prep(mla_s, cos_s, sin_s, w, expand=False)
    mla_o_p = mla_prompt(q_p.reshape(bsz, s_len, -1), k_p.reshape(bsz, s_len, -1), vt_p, k_m, vt_m)
    mla_o_s = mla_sample(q_s.reshape(dbs, d_seq, -1), c_s.reshape(dbs, d_seq, -1), kr_s.reshape(dbs, d_seq, -1),
                         cache_mla_ckv[0].astype(F32), jnp.swapaxes(cache_mla_krope[0].astype(F32), 1, 2), w['w_ukv'])

    y_p = out_ffn(x1_p, gdn_p.reshape(bsz * s_len, -1), mla_o_p.reshape(bsz * s_len, -1), w)
    y_s = out_ffn(x1_s, gdn_s.reshape(dbs * d_seq, -1), mla_o_s.reshape(dbs * d_seq, -1), w)

    with_meta = lambda m_rows, rows: jnp.concatenate(
        [jnp.broadcast_to(m_rows[None], (bsz,) + m_rows.shape), rows.reshape(bsz, s_len, -1)], axis=1)
    krope_t = jnp.concatenate([jnp.broadcast_to(kr_m[None], (bsz, DR, N_META)),
                               kr_p.reshape(DR, bsz, s_len).transpose(1, 0, 2)], axis=2)
    return (y_p.reshape(bsz, s_len, D_MODEL), y_s.reshape(dbs, d_seq, D_MODEL),
            with_meta(c_m, c_p)[None], jnp.swapaxes(krope_t, 1, 2)[None],
            m_p[None], conv_p[None],
            c_s.reshape(dbs, d_seq, KV_LORA)[None], kr_s[:, :DR].reshape(dbs, d_seq, DR)[None],
            m_s[None], qkv_s3[:, d_seq - (CONV_W - 1):][None])
```

```python
import functools

import jax
import jax.numpy as jnp
import numpy as np
from jax import lax
from jax.experimental import pallas as pl
from jax.experimental.pallas import tpu as pltpu

F32 = jnp.float32
BF16 = jnp.bfloat16

D_MODEL = 1024
CHUNK = 64
N_META = 16
H_GDN = 4
GDN_DK = 128
GDN_DV = 128
CONV_W = 4
CONV_DIM = H_GDN * (2 * GDN_DK + GDN_DV)
H_MLA = 4
Q_LORA = 384
KV_LORA = 256
DN = 128
DR = 64
DV_MLA = 128
ROPE_BASE = 10000.0
SM_SCALE = (DN + DR) ** -0.5
LOG2_E = 1.4426950408889634
Q_SCALE = SM_SCALE * LOG2_E
D_FF = 2816
EPS = 1e-6
L2_EPS = 1e-6

LANES = 128
SUBLANES = 8
FF_CHUNK = 256
MLA_IN = Q_LORA + KV_LORA + LANES
QK_PAD = 2 * LANES
AB_LANE = DR
VMEM_LIMIT = 56 * 1024 * 1024

NT_DIMS = (((1,), (1,)), ((), ()))


def _rms(x, g):
    return x * lax.rsqrt(jnp.mean(x * x, axis=-1, keepdims=True) + EPS) * g


def _bdot(a, b):
    return jnp.dot(a.astype(BF16), b.astype(BF16), preferred_element_type=F32)


def _bdot_nt(a, b):
    return lax.dot_general(a.astype(BF16), b.astype(BF16), NT_DIMS, preferred_element_type=F32)


def _const_spec(shape):
    nd = len(shape)
    return pl.BlockSpec(shape, lambda *_: (0,) * nd, pipeline_mode=pl.Buffered(1))


def _params(*sem):
    return pltpu.CompilerParams(dimension_semantics=sem, vmem_limit_bytes=VMEM_LIMIT)


def _drain(*stage_gens):
    live = [[g, 0.0, float(t)] for g, t in (sg if isinstance(sg, tuple) else (sg, 1.0) for sg in stage_gens)]
    while live:
        entry = min(live, key=lambda e: e[1] / e[2])
        try:
            entry[1] += next(entry[0]) or 0.0
        except StopIteration:
            live.remove(entry)


def _swiglu_half_stages(x, g_ref, wg_ref, wu_ref, wd_ref, h_ref, acc_ref, result):
    h_ref[...] = _rms(x, g_ref[...]).astype(BF16)
    nf = wg_ref.shape[1] // FF_CHUNK

    def gate_up(f):
        cols = slice(f * FF_CHUNK, (f + 1) * FF_CHUNK)
        gate = jnp.dot(h_ref[...], wg_ref[:, cols], preferred_element_type=F32)
        up = jnp.dot(h_ref[...], wu_ref[:, cols], preferred_element_type=F32)
        return gate, up

    nxt = gate_up(0)
    yield 1.0
    for f in range(nf):
        gate, up = nxt
        if f + 1 < nf:
            nxt = gate_up(f + 1)
            yield 1.0
        act = (jax.nn.silu(gate) * up).astype(BF16)
        down = jnp.dot(act, wd_ref[f * FF_CHUNK:(f + 1) * FF_CHUNK, :], preferred_element_type=F32)
        if f == 0:
            acc_ref[...] = down
        else:
            acc_ref[...] += down
        yield 1.0
    result.append(x + 0.5 * acc_ref[...])


def _swiglu_half(x, g_ref, wg_ref, wu_ref, wd_ref, h_ref, acc_ref):
    result = []
    _drain(_swiglu_half_stages(x, g_ref, wg_ref, wu_ref, wd_ref, h_ref, acc_ref, result))
    return result[0]


def _ffn_proj_kernel(x_ref, g1_ref, wg_ref, wu_ref, wd_ref, gm_ref, wqkv_ref, wz_ref, wmla_ref, *rest,
                     stream_tiles):
    if not stream_tiles:
        x1_ref, qkv_ref, z_ref, mla_ref, h_ref, acc_ref = rest
        x1 = _swiglu_half(x_ref[...], g1_ref, wg_ref, wu_ref, wd_ref, h_ref, acc_ref)
    else:
        hist_ref, convw_ref, x1_ref, qkv_ref, z_ref, mla_ref, tail_ref, h_ref, acc_ref, conv_scr = rest
        i = pl.program_id(0)
        tm = x_ref.shape[0]

        @pl.when(i == 0)
        def _():
            conv_scr[...] = jnp.zeros_like(conv_scr)

        @pl.when((i + stream_tiles - 1) % stream_tiles == 0)
        def _():
            conv_scr[0, 0:GDN_HIST, :] = hist_ref[...]

        def feature_stages():
            for row0 in range(0, tm, CHUNK):
                qkv_ref[row0:row0 + CHUNK, :] = _gdn_features(conv_scr, convw_ref, 0, row0)
                yield 1.0
            conv_scr[0, 0:GDN_HIST, :] = conv_scr[0, tm:tm + GDN_HIST, :]

        result = []
        _drain((_swiglu_half_stages(x_ref[...], g1_ref, wg_ref, wu_ref, wd_ref, h_ref, acc_ref, result),
                2 * (D_FF // FF_CHUNK)), (feature_stages(), tm // CHUNK))
        x1 = result[0]
    x1_ref[...] = x1
    h_ref[...] = _rms(x1, gm_ref[...]).astype(BF16)
    qkv = jnp.dot(h_ref[...], wqkv_ref[...], preferred_element_type=F32)
    z_ref[...] = jnp.dot(h_ref[...], wz_ref[...], preferred_element_type=F32)
    mla_ref[...] = jnp.dot(h_ref[...], wmla_ref[...], preferred_element_type=F32)
    if stream_tiles:
        conv_scr[0, GDN_HIST:GDN_HIST + tm, :] = qkv
        tail_ref[0] = qkv[tm - GDN_HIST:, :]
    else:
        qkv_ref[...] = qkv


def _row_tile(n, want):
    t = min(want, n)
    assert n % t == 0, (n, t)
    return t


def ffn_proj(x, w, hist=None, stream_len=None, tm=512):
    n = x.shape[0]
    tm = _row_tile(n, tm)
    ntile = n // tm
    stream_tiles = 0
    if hist is not None:
        assert stream_len % tm == 0 and tm % CHUNK == 0
        stream_tiles = stream_len // tm
    tile = (lambda i: jnp.minimum(i, ntile - 1)) if stream_tiles else (lambda i: i)
    row = lambda width: pl.BlockSpec((tm, width), lambda i: (tile(i), 0))
    consts = (w['g1'], w['wg1'], w['wu1'], w['wd1'], w['gm'], w['w_qkv'], w['w_z'], w['w_mla'])
    out_specs = [row(D_MODEL), row(CONV_DIM), row(H_GDN * GDN_DV), row(MLA_IN)]
    out_shape = [jax.ShapeDtypeStruct((n, D_MODEL), F32), jax.ShapeDtypeStruct((n, CONV_DIM), F32),
                 jax.ShapeDtypeStruct((n, H_GDN * GDN_DV), F32), jax.ShapeDtypeStruct((n, MLA_IN), F32)]
    scratch = [pltpu.VMEM((tm, D_MODEL), BF16), pltpu.VMEM((tm, D_MODEL), F32)]
    if stream_tiles:
        consts += (hist, w['conv_w'])
        out_specs[1] = pl.BlockSpec((tm, CONV_DIM), lambda i: (jnp.maximum(i - 1, 0), 0))
        out_specs.append(pl.BlockSpec((1, GDN_HIST, CONV_DIM), lambda i: (tile(i), 0, 0)))
        out_shape.append(jax.ShapeDtypeStruct((ntile, GDN_HIST, CONV_DIM), F32))
        scratch.append(pltpu.VMEM((1, GDN_HIST + tm, CONV_DIM), F32))
    return pl.pallas_call(
        functools.partial(_ffn_proj_kernel, stream_tiles=stream_tiles),
        grid=(ntile + (1 if stream_tiles else 0),),
        in_specs=[row(D_MODEL)] + [_const_spec(c.shape) for c in consts],
        out_specs=out_specs,
        out_shape=out_shape,
        scratch_shapes=scratch,
        compiler_params=_params("arbitrary" if stream_tiles else "parallel"),
        name="ffn_proj",
    )(x, *consts)


def _out_ffn_kernel(x1_ref, gdn_ref, mla_ref, wog_ref, wom_ref, g2_ref, wg_ref, wu_ref, wd_ref, gf_ref,
                    y_ref, h_ref, acc_ref):
    x2 = (x1_ref[...] + jnp.dot(gdn_ref[...], wog_ref[...], preferred_element_type=F32)
          + jnp.dot(mla_ref[...], wom_ref[...], preferred_element_type=F32))
    x3 = _swiglu_half(x2, g2_ref, wg_ref, wu_ref, wd_ref, h_ref, acc_ref)
    y_ref[...] = _rms(x3, gf_ref[...])


def out_ffn(x1, gdn, mla, w, tm=512):
    n = x1.shape[0]
    tm = _row_tile(n, tm)
    row = lambda width: pl.BlockSpec((tm, width), lambda i: (i, 0))
    consts = (w['w_out_g'], w['w_out_m'], w['g2'], w['wg2'], w['wu2'], w['wd2'], w['gf'])
    return pl.pallas_call(
        _out_ffn_kernel,
        grid=(n // tm,),
        in_specs=[row(D_MODEL), row(H_GDN * GDN_DV), row(H_MLA * DV_MLA)] + [_const_spec(c.shape) for c in consts],
        out_specs=row(D_MODEL),
        out_shape=jax.ShapeDtypeStruct((n, D_MODEL), F32),
        scratch_shapes=[pltpu.VMEM((tm, D_MODEL), BF16), pltpu.VMEM((tm, D_MODEL), F32)],
        compiler_params=_params("parallel"),
        name="out_ffn",
    )(x1, gdn, mla, *consts)


def _cumsum_rows(x):
    n = x.shape[0]
    row = lax.broadcasted_iota(jnp.int32, x.shape, 0)
    shift = 1
    while shift < n:
        x = x + jnp.where(row >= shift, pltpu.roll(x, shift, 0), 0.0)
        shift *= 2
    return x


def _transpose_rows(x):
    length = x.shape[0]
    sq = jnp.concatenate([x, jnp.zeros((LANES - length, LANES), x.dtype)], axis=0)
    return sq.T[:, :length]


GDN_HIST = SUBLANES
GDN_PREP_COST = 4.0
GDN_PROMPT_ROWS = 2 * CHUNK
GDN_SAMPLE_GROUP = 8


def _gdn_features(conv_scr, convw_ref, i, row0):
    base = GDN_HIST - (CONV_W - 1) + row0
    parts = []
    for h in range(CONV_DIM // GDN_DK):
        cols = slice(h * GDN_DK, (h + 1) * GDN_DK)
        x = conv_scr[i, pl.ds(base, CHUNK), cols] * convw_ref[0:1, cols]
        for j in range(1, CONV_W):
            x = x + conv_scr[i, pl.ds(base + j, CHUNK), cols] * convw_ref[j:j + 1, cols]
        x = jax.nn.silu(x)
        if h < 2 * H_GDN:
            x = x * lax.rsqrt(jnp.sum(x * x, axis=-1, keepdims=True) + L2_EPS)
        parts.append(x * (GDN_DK ** -0.5) if h < H_GDN else x)
    return jnp.concatenate(parts, axis=1)


def _gdn_block_stages(first_block, rows, pad_front, features, ab_ref, z_ref, write_o, m_scr,
                      alog_ref, dtb_ref, gn_ref):
    bg = ab_ref.shape[0]
    length = CHUNK
    heads = H_GDN
    width = heads * length
    row = lax.broadcasted_iota(jnp.int32, (length, LANES), 0)
    lane = lax.broadcasted_iota(jnp.int32, (length, width), 1)
    lane_head = lane // length
    lane_col = lane % length
    row_p = lax.broadcasted_iota(jnp.int32, (length, width), 0)
    causal = lane_col <= row_p
    strict = lane_col < row_p

    def spread(cols):
        out = jnp.broadcast_to(cols[-1], (length, width))
        for h in range(heads - 2, -1, -1):
            out = jnp.where(lane_head == h, cols[h], out)
        return out

    def pick(mats):
        out = mats[-1]
        for h in range(heads - 2, -1, -1):
            out = jnp.where(lane_head == h, mats[h], out)
        return out

    def diag_blocks(p):
        return jnp.concatenate([jnp.where(lane_head == h, p, 0.0) for h in range(heads)], axis=0).astype(BF16)

    def diag_wide(mats):
        zero = jnp.zeros_like(mats[0])
        return jnp.concatenate(
            [jnp.concatenate([mats[h] if g == h else zero for g in range(heads)], axis=1) for h in range(heads)],
            axis=0).astype(BF16)

    def per_batch(i, row0):
        u = features(i, row0)
        ab = ab_ref[i, row0:row0 + length, :]
        g_all = -jnp.exp(alog_ref[...]) * jax.nn.softplus(ab + dtb_ref[...])
        beta_all = jax.nn.sigmoid(ab)
        if pad_front > row0:
            valid = jnp.logical_or(jnp.logical_not(first_block), row >= pad_front - row0)
            g_all = jnp.where(valid, g_all, 0.0)
            beta_all = jnp.where(valid, beta_all, 0.0)
        gc_all = _cumsum_rows(g_all)
        gc_rows = jnp.concatenate([gc_all, jnp.zeros((LANES - length, LANES), F32)], axis=0).T
        gc_rows = gc_rows[AB_LANE:AB_LANE + SUBLANES, :]
        gc_rows_hi = pltpu.roll(gc_rows, length, 1)
        gc_row = jnp.concatenate([gc_rows[h:h + 1] + gc_rows_hi[h + 1:h + 2] for h in range(0, heads, 2)], axis=1)

        hd = []
        for h in range(heads):
            q = u[:, h * GDN_DK:(h + 1) * GDN_DK]
            k = u[:, (heads + h) * GDN_DK:(heads + h + 1) * GDN_DK]
            v = u[:, 2 * heads * GDN_DK + h * GDN_DV:2 * heads * GDN_DK + (h + 1) * GDN_DV]
            gc = gc_all[:, AB_LANE + h:AB_LANE + h + 1]
            beta = beta_all[:, AB_LANE + heads + h:AB_LANE + heads + h + 1]
            kq = jnp.concatenate([k, q], axis=0).astype(BF16)
            hd.append(dict(h=h, k=k, v=v, gc=gc, beta=beta, kq=kq))
        gc_col = spread([c['gc'] for c in hd])
        decay = jnp.where(causal, jnp.exp(jnp.where(causal, gc_col - gc_row, 0.0)), 0.0)
        k_all = jnp.concatenate([c['k'] for c in hd], axis=0).astype(BF16)
        return dict(i=i, row0=row0, heads=hd, decay=decay, beta=spread([c['beta'] for c in hd]), k_all=k_all)

    per_chunk = []
    for row0 in range(0, rows, length):
        per_chunk.append([])
        for i in range(bg):
            per_chunk[-1].append(per_batch(i, row0))
            yield GDN_PREP_COST
    groups = [g for chunk in per_chunk for g in chunk]
    for g in groups:
        scores = [lax.dot_general(c['kq'], g['k_all'], NT_DIMS, preferred_element_type=F32) for c in g['heads']]
        g['kk'] = pick([s[:length] for s in scores])
        g['qk'] = pick([s[length:] for s in scores])
    yield 1.0
    for g in groups:
        g['a'] = jnp.where(strict, g['beta'] * g['kk'] * g['decay'], 0.0)
        g['qk_decay'] = g['qk'] * g['decay']
        g['y'] = -g['a']
    for g in groups:
        g['pw'] = jnp.dot(g['a'].astype(BF16), diag_blocks(g['a']), preferred_element_type=F32)
    yield 1.0
    span = 2
    while span < length:
        span *= 2
        for g in groups:
            blocks = diag_blocks(g['pw'])
            if span < length:
                prod = jnp.dot(jnp.concatenate([g['y'], g['pw']], axis=0).astype(BF16), blocks,
                               preferred_element_type=F32)
                g['y'] = g['y'] + g['pw'] + prod[:length]
                g['pw'] = prod[length:]
            else:
                g['y'] = g['y'] + g['pw'] + jnp.dot(g['y'].astype(BF16), blocks, preferred_element_type=F32)
        yield 1.0

    def emit_output(chunk):
        for g in chunk:
            prod = jnp.dot(g['qk_decay'].astype(BF16), diag_wide([c['uu'] for c in g['heads']]),
                           preferred_element_type=F32)
            rws = slice(g['row0'], g['row0'] + length)
            for c in g['heads']:
                cols = slice(c['h'] * GDN_DV, (c['h'] + 1) * GDN_DV)
                o = c['eg'] * c['kqm'][length:] + prod[:, cols]
                n = _rms(o, gn_ref[...])
                write_o(g['i'], rws, cols, (n * jax.nn.silu(z_ref[g['i'], rws, cols])).astype(BF16))

    pending = None
    for chunk in per_chunk:
        for g in chunk:
            for c in g['heads']:
                c['m0'] = m_scr[g['i'], c['h']]
                c['kqm'] = jnp.dot(c['kq'], c['m0'].astype(BF16), preferred_element_type=F32)
        yield 1.0
        if pending is not None:
            emit_output(pending)
            yield 1.0
        for g in chunk:
            for c in g['heads']:
                c['eg'] = jnp.exp(c['gc'])
                c['rhs'] = c['beta'] * (c['v'] - c['eg'] * c['kqm'][:length])
            prod = jnp.dot(g['y'].astype(BF16), diag_wide([c['rhs'] for c in g['heads']]),
                           preferred_element_type=F32)
            for c in g['heads']:
                c['uu'] = c['rhs'] + prod[:, c['h'] * GDN_DV:(c['h'] + 1) * GDN_DV]
        yield 1.0
        for g in chunk:
            for c in g['heads']:
                g_last = c['gc'][length - 1:length, :]
                k_dec = c['k'] * jnp.exp(g_last - c['gc'])
                m_scr[g['i'], c['h']] = jnp.exp(g_last) * c['m0'] + _bdot(_transpose_rows(k_dec), c['uu'])
        yield 1.0
        pending = chunk
    emit_output(pending)
    yield 1.0


def _gdn_stage_block(qkv_ref, conv_scr):
    bg, rows, _ = qkv_ref.shape
    for i in range(bg):
        conv_scr[i, GDN_HIST:GDN_HIST + rows, :] = qkv_ref[i]


def _gdn_keep_history(conv_scr, rows):
    for i in range(conv_scr.shape[0]):
        conv_scr[i, 0:GDN_HIST, :] = conv_scr[i, rows:rows + GDN_HIST, :]


def _gdn_kernel(qkv_ref, z_ref, ab_ref, m0_ref, conv0_ref, convw_ref, alog_ref, dtb_ref, gn_ref,
                o_ref, mout_ref, conv_scr, m_scr, *, pad_front, prepared):
    c = pl.program_id(1)
    rows = qkv_ref.shape[1]

    @pl.when(c == 0)
    def _():
        m_scr[...] = m0_ref[...]
        conv_scr[:, 0:GDN_HIST, :] = conv0_ref[...]

    def write_o(i, rws, cols, val):
        o_ref[i, rws, cols] = val

    if prepared:
        features = lambda i, row0: qkv_ref[i, row0:row0 + CHUNK, :]
    else:
        _gdn_stage_block(qkv_ref, conv_scr)
        features = functools.partial(_gdn_features, conv_scr, convw_ref)
    _drain(_gdn_block_stages(c == 0, rows, pad_front, features, ab_ref, z_ref, write_o, m_scr,
                             alog_ref, dtb_ref, gn_ref))
    if not prepared:
        _gdn_keep_history(conv_scr, rows)

    @pl.when(c == pl.num_programs(1) - 1)
    def _():
        mout_ref[...] = m_scr[...]


def _gdn_in_specs(bg, rows, blk):
    nd = len(blk(0, 0))
    const = lambda shape: pl.BlockSpec(shape, lambda *_: (0,) * len(shape))
    return [
        pl.BlockSpec((bg, rows, CONV_DIM), blk),
        pl.BlockSpec((bg, rows, H_GDN * GDN_DV), blk),
        pl.BlockSpec((bg, rows, LANES), lambda *a: blk(*a)[:nd - 1] + (MLA_IN // LANES - 1,)),
        pl.BlockSpec((bg, H_GDN, GDN_DK, GDN_DV), lambda g, c: (g, 0, 0, 0)),
        pl.BlockSpec((bg, GDN_HIST, CONV_DIM), lambda g, c: (g, 0, 0)),
        const((CONV_W, CONV_DIM)),
        const((1, LANES)),
        const((1, LANES)),
        const((1, GDN_DV)),
    ]


def gdn(qkv, z, mla_in, m0, conv0, w, *, pad_front=0, bg=4, rows=CHUNK, prepared=False):
    b, t, _ = qkv.shape
    bg = min(bg, b)
    assert b % bg == 0 and t % rows == 0 and rows % CHUNK == 0
    blk = lambda g, c: (g, c, 0)
    return pl.pallas_call(
        functools.partial(_gdn_kernel, pad_front=pad_front, prepared=prepared),
        grid=(b // bg, t // rows),
        in_specs=_gdn_in_specs(bg, rows, blk),
        out_specs=[
            pl.BlockSpec((bg, rows, H_GDN * GDN_DV), blk),
            pl.BlockSpec((bg, H_GDN, GDN_DK, GDN_DV), lambda g, c: (g, 0, 0, 0)),
        ],
        out_shape=[jax.ShapeDtypeStruct((b, t, H_GDN * GDN_DV), BF16),
                   jax.ShapeDtypeStruct((b, H_GDN, GDN_DK, GDN_DV), F32)],
        scratch_shapes=[pltpu.VMEM((bg, GDN_HIST + (0 if prepared else rows), CONV_DIM), F32),
                        pltpu.VMEM((bg, H_GDN, GDN_DK, GDN_DV), F32)],
        compiler_params=_params("parallel", "arbitrary"),
        name="gdn",
    )(qkv, z, mla_in, m0, conv0, w['conv_w'], w['alog'], w['dtb'], w['gn'])


def _rope(t, cos, sin):
    lane = lax.broadcasted_iota(jnp.int32, t.shape, 1)
    half = DR // 2
    swapped = jnp.where(lane < half, pltpu.roll(t, LANES - half, 1), pltpu.roll(t, half, 1))
    return t * cos + swapped * sin


def _mla_prep_kernel(x_ref, cos_ref, sin_ref, gq_ref, gkv_ref, wuq_ref, wukv_ref, *out_refs, expand):
    q_ref, c_ref, kr_ref = out_refs[:3]
    x = x_ref[...]
    cos = cos_ref[...]
    sin = sin_ref[...]
    q = _bdot(_rms(x[:, :Q_LORA], gq_ref[...]), wuq_ref[...])
    for h in range(H_MLA):
        lo = h * QK_PAD
        q_ref[:, lo:lo + DN] = (q[:, lo:lo + DN] * Q_SCALE).astype(BF16)
        q_ref[:, lo + DN:lo + QK_PAD] = (_rope(q[:, lo + DN:lo + QK_PAD], cos, sin) * Q_SCALE).astype(BF16)
    c = _rms(x[:, Q_LORA:Q_LORA + KV_LORA], gkv_ref[...])
    c_ref[...] = c
    kr = _rope(x[:, Q_LORA + KV_LORA:], cos, sin)
    if not expand:
        kr_ref[...] = kr
    else:
        kr_ref[...] = (kr.T if kr.shape[0] % LANES == 0 else _transpose_rows(kr))[:DR]
        k_ref, vt_ref = out_refs[3:]
        kv = _bdot(c, wukv_ref[...])
        for h in range(H_MLA):
            lo = h * (DN + DV_MLA)
            k_ref[:, h * QK_PAD:h * QK_PAD + DN] = kv[:, lo:lo + DN].astype(BF16)
            k_ref[:, h * QK_PAD + DN:(h + 1) * QK_PAD] = kr.astype(BF16)
            v = kv[:, lo + DN:lo + DN + DV_MLA]
            v_t = v.T if v.shape[0] % LANES == 0 else _transpose_rows(v)
            vt_ref[h * DV_MLA:(h + 1) * DV_MLA, :] = v_t.astype(BF16)


def mla_prep(mla_in, cos, sin, w, *, expand, tm=512):
    n = mla_in.shape[0]
    tm = _row_tile(min(n, cos.shape[0]), tm)
    nrep = cos.shape[0] // tm
    row = lambda width: pl.BlockSpec((tm, width), lambda i: (i, 0))
    tab = pl.BlockSpec((tm, LANES), lambda i: (i % nrep, 0))
    consts = (w['gq'], w['gkv'], w['w_uq'], w['w_ukv'])
    out_specs = [row(H_MLA * QK_PAD), row(KV_LORA), row(LANES)]
    out_shape = [jax.ShapeDtypeStruct((n, H_MLA * QK_PAD), BF16), jax.ShapeDtypeStruct((n, KV_LORA), F32),
                 jax.ShapeDtypeStruct((n, LANES), F32)]
    if expand:
        out_specs[2] = pl.BlockSpec((DR, tm), lambda i: (0, i))
        out_shape[2] = jax.ShapeDtypeStruct((DR, n), F32)
        out_specs += [row(H_MLA * QK_PAD), pl.BlockSpec((H_MLA * DV_MLA, tm), lambda i: (0, i))]
        out_shape += [jax.ShapeDtypeStruct((n, H_MLA * QK_PAD), BF16), jax.ShapeDtypeStruct((H_MLA * DV_MLA, n), BF16)]
    return pl.pallas_call(
        functools.partial(_mla_prep_kernel, expand=expand),
        grid=(n // tm,),
        in_specs=[row(MLA_IN), tab, tab] + [_const_spec(c.shape) for c in consts],
        out_specs=out_specs,
        out_shape=out_shape,
        compiler_params=_params("parallel"),
        name="mla_prep_kv" if expand else "mla_prep",
    )(mla_in, cos, sin, *consts)


def _mla_prompt_kernel(q_ref, k_ref, vt_ref, km_ref, vmt_ref, o_ref, s_scr, m_scr, l_scr, acc_scr):
    qi = pl.program_id(1)
    tq = q_ref.shape[0]
    heads = range(H_MLA)

    def scores(off, h):
        return lax.dot_general(k_ref[pl.ds(off, tq), h * QK_PAD:(h + 1) * QK_PAD],
                               q_ref[:, h * QK_PAD:(h + 1) * QK_PAD], NT_DIMS, preferred_element_type=F32)

    def consume(h, off, masked):
        s_t = s_scr[h]
        if masked:
            key_chunk = lax.broadcasted_iota(jnp.int32, (tq, tq), 0) // CHUNK
            qry_chunk = lax.broadcasted_iota(jnp.int32, (tq, tq), 1) // CHUNK
            s_t = jnp.where(key_chunk <= qry_chunk, s_t, -jnp.inf)
        m = m_scr[h]
        m_new = jnp.maximum(m, jnp.max(s_t, axis=0, keepdims=True))
        alpha = jnp.exp2(m - m_new)
        p_t = jnp.exp2(s_t - m_new)
        m_scr[h] = m_new
        l_scr[h] = alpha * l_scr[h] + jnp.sum(p_t, axis=0, keepdims=True)
        acc_scr[h] = alpha * acc_scr[h] + jnp.dot(vt_ref[h * DV_MLA:(h + 1) * DV_MLA, pl.ds(off, tq)],
                                                  p_t.astype(BF16), preferred_element_type=F32)

    meta_s = [lax.dot_general(km_ref[:, h * QK_PAD:(h + 1) * QK_PAD], q_ref[:, h * QK_PAD:(h + 1) * QK_PAD],
                              NT_DIMS, preferred_element_type=F32) for h in heads]
    for h in heads:
        s_scr[h] = scores(0, h)
    meta_p = []
    for h in heads:
        m = jnp.max(meta_s[h], axis=0, keepdims=True)
        p_t = jnp.exp2(meta_s[h] - m)
        m_scr[h] = m
        l_scr[h] = jnp.sum(p_t, axis=0, keepdims=True)
        meta_p.append(p_t.astype(BF16))
    for h in heads:
        acc_scr[h] = jnp.dot(vmt_ref[h * DV_MLA:(h + 1) * DV_MLA, :], meta_p[h], preferred_element_type=F32)

    def full_tile(t):
        off = pl.multiple_of(t * tq, tq)
        nxt = pl.multiple_of((t + 1) * tq, tq)
        for h in heads:
            s_next = scores(nxt, h)
            consume(h, off, False)
            s_scr[h] = s_next

    def tile_pair(t2, carry):
        full_tile(2 * t2)
        full_tile(2 * t2 + 1)
        return carry

    lax.fori_loop(0, qi // 2, tile_pair, 0)

    @pl.when(qi % 2 == 1)
    def _():
        full_tile(qi - 1)

    for h in heads:
        consume(h, pl.multiple_of(qi * tq, tq), True)
    for h in heads:
        o_ref[:, h * DV_MLA:(h + 1) * DV_MLA] = (acc_scr[h] / l_scr[h]).T.astype(o_ref.dtype)


def mla_prompt(q, k, v_t, k_meta, v_meta_t, tq=512):
    b, s, _ = q.shape
    tq = _row_tile(s, tq)
    assert tq % LANES == 0
    whole = lambda shape: pl.BlockSpec(shape, lambda bi, i: (0, 0))
    return pl.pallas_call(
        _mla_prompt_kernel,
        grid=(b, s // tq),
        in_specs=[
            pl.BlockSpec((None, tq, H_MLA * QK_PAD), lambda bi, i: (bi, i, 0)),
            pl.BlockSpec((None, s, H_MLA * QK_PAD), lambda bi, i: (bi, 0, 0), pipeline_mode=pl.Buffered(1)),
            pl.BlockSpec((H_MLA * DV_MLA, s), lambda bi, i: (0, bi), pipeline_mode=pl.Buffered(1)),
            whole(k_meta.shape),
            whole(v_meta_t.shape),
        ],
        out_specs=pl.BlockSpec((None, tq, H_MLA * DV_MLA), lambda bi, i: (bi, i, 0)),
        out_shape=jax.ShapeDtypeStruct((b, s, H_MLA * DV_MLA), BF16),
        scratch_shapes=[pltpu.VMEM((H_MLA, tq, tq), F32), pltpu.VMEM((H_MLA, 1, tq), F32),
                        pltpu.VMEM((H_MLA, 1, tq), F32), pltpu.VMEM((H_MLA, DV_MLA, tq), F32)],
        compiler_params=_params("parallel", "arbitrary"),
        name="mla_prompt",
    )(q, k, v_t, k_meta, v_meta_t)


def _mla_sample_kernel(q_ref, c_ref, kr_ref, cc_ref, ckr_ref, wukv_ref, o_ref):
    nreq, t, _ = q_ref.shape
    w = wukv_ref[...]
    reqs = []
    for r in range(nreq):
        q = q_ref[r]
        qa = jnp.concatenate(
            [lax.dot_general(q[:, h * QK_PAD:h * QK_PAD + DN], w[:, h * (DN + DV_MLA):h * (DN + DV_MLA) + DN],
                             NT_DIMS, preferred_element_type=F32) for h in range(H_MLA)], axis=0).astype(BF16)
        qr = jnp.concatenate([q[:, h * QK_PAD + DN:(h + 1) * QK_PAD] for h in range(H_MLA)], axis=0)
        reqs.append(dict(qa=qa, qr=qr, cache_c=cc_ref[r].astype(BF16), own_c=c_ref[r].astype(BF16)))
    for r, rq in enumerate(reqs):
        rq['s_cache'] = (lax.dot_general(rq['qa'], rq['cache_c'], NT_DIMS, preferred_element_type=F32)
                         + jnp.dot(rq['qr'][:, :DR], ckr_ref[r].astype(BF16), preferred_element_type=F32))
        rq['s_own'] = (lax.dot_general(rq['qa'], rq['own_c'], NT_DIMS, preferred_element_type=F32)
                       + lax.dot_general(rq['qr'], kr_ref[r].astype(BF16), NT_DIMS, preferred_element_type=F32))
    for rq in reqs:
        m = jnp.maximum(jnp.max(rq['s_cache'], axis=-1, keepdims=True), jnp.max(rq['s_own'], axis=-1, keepdims=True))
        p_cache = jnp.exp2(rq['s_cache'] - m)
        p_own = jnp.exp2(rq['s_own'] - m)
        rq['l'] = jnp.sum(p_cache, axis=-1, keepdims=True) + jnp.sum(p_own, axis=-1, keepdims=True)
        rq['p_cache'], rq['p_own'] = p_cache.astype(BF16), p_own.astype(BF16)
    for rq in reqs:
        rq['pc'] = (jnp.dot(rq['p_cache'], rq['cache_c'], preferred_element_type=F32)
                    + jnp.dot(rq['p_own'], rq['own_c'], preferred_element_type=F32)) / rq['l']
    for r, rq in enumerate(reqs):
        for h in range(H_MLA):
            lo = h * (DN + DV_MLA) + DN
            o_ref[r, :, h * DV_MLA:(h + 1) * DV_MLA] = _bdot(rq['pc'][h * t:(h + 1) * t],
                                                             w[:, lo:lo + DV_MLA]).astype(o_ref.dtype)


def mla_sample(q, c, kr, cache_c, cache_kr, w_ukv, group=2):
    b, t, _ = q.shape
    p = cache_c.shape[1]
    group = min(group, b)
    assert b % group == 0
    per_b = lambda rows, width: pl.BlockSpec((group, rows, width), lambda bi: (bi, 0, 0))
    return pl.pallas_call(
        _mla_sample_kernel,
        grid=(b // group,),
        in_specs=[per_b(t, H_MLA * QK_PAD), per_b(t, KV_LORA), per_b(t, LANES), per_b(p, KV_LORA), per_b(DR, p),
                  _const_spec(w_ukv.shape)],
        out_specs=per_b(t, H_MLA * DV_MLA),
        out_shape=jax.ShapeDtypeStruct((b, t, H_MLA * DV_MLA), BF16),
        compiler_params=_params("parallel"),
        name="mla_sample",
    )(q, c, kr, cache_c, cache_kr, w_ukv)


def _rope_tables(first, count):
    inv = ROPE_BASE ** (-np.arange(0, DR, 2, dtype=np.float64) / DR)
    ang = (first + np.arange(count, dtype=np.float64))[:, None] * inv[None, :]
    cos, sin = np.cos(ang), np.sin(ang)
    pad = np.zeros((count, LANES - DR))
    return (jnp.asarray(np.concatenate([cos, cos, pad], axis=1), F32),
            jnp.asarray(np.concatenate([-sin, sin, pad], axis=1), F32))


def _prepare_weights(ffn1_norm, ffn1_wg, ffn1_wu, ffn1_wd, mix_norm, w_in, conv_w, a_log, dt_bias, gdn_norm,
                     q_norm, kv_norm, w_uq, w_ukv, w_out, ffn2_norm, ffn2_wg, ffn2_wu, ffn2_wd, final_norm):
    w = {}
    w['g1'], w['gm'], w['g2'] = ffn1_norm[0][None], mix_norm[0][None], ffn2_norm[0][None]
    w['gf'] = final_norm[None]
    assert D_FF % FF_CHUNK == 0
    w['wg1'], w['wu1'], w['wd1'] = (m[0].astype(BF16) for m in (ffn1_wg, ffn1_wu, ffn1_wd))
    w['wg2'], w['wu2'], w['wd2'] = (m[0].astype(BF16) for m in (ffn2_wg, ffn2_wu, ffn2_wd))
    o_z = CONV_DIM
    o_a = o_z + H_GDN * GDN_DV
    o_b = o_a + H_GDN
    o_cq = o_b + H_GDN
    o_kr = o_cq + Q_LORA + KV_LORA
    wi = w_in[0]
    w['w_qkv'] = wi[:, :o_z].astype(BF16)
    w['w_z'] = wi[:, o_z:o_a].astype(BF16)
    tail_pad = jnp.zeros((D_MODEL, LANES - DR - 2 * H_GDN), wi.dtype)
    w['w_mla'] = jnp.concatenate([wi[:, o_cq:o_kr], wi[:, o_kr:o_kr + DR], wi[:, o_a:o_cq], tail_pad], axis=1).astype(BF16)
    w['conv_w'] = conv_w[0]
    lane_vec = lambda v: jnp.zeros((1, LANES), F32).at[0, AB_LANE:AB_LANE + H_GDN].set(v.astype(F32))
    w['alog'], w['dtb'] = lane_vec(a_log[0]), lane_vec(dt_bias[0])
    w['gn'] = gdn_norm[0][None]
    w['gq'], w['gkv'] = q_norm[0][None], kv_norm[0][None]
    uq = w_uq[0].reshape(Q_LORA, H_MLA, DN + DR)
    uq = jnp.concatenate([uq, jnp.zeros((Q_LORA, H_MLA, QK_PAD - DN - DR), uq.dtype)], axis=-1)
    w['w_uq'] = uq.reshape(Q_LORA, H_MLA * QK_PAD).astype(BF16)
    w['w_ukv'] = w_ukv[0].astype(BF16)
    w['w_out_g'] = w_out[0][:H_GDN * GDN_DV].astype(BF16)
    w['w_out_m'] = w_out[0][H_GDN * GDN_DV:].astype(BF16)
    return w


def kernel(x_prompt, x_sample, cache_mla_ckv, cache_mla_krope, state_gdn, state_conv, meta, ffn1_norm, ffn1_wg,
           ffn1_wu, ffn1_wd, mix_norm, w_in, conv_w, a_log, dt_bias, gdn_norm, q_norm, kv_norm, w_uq, w_ukv, w_out,
           ffn2_norm, ffn2_wg, ffn2_wu, ffn2_wd, final_norm):
    assert ffn1_wg.shape[0] == 1, "one layer: the meta rows are not carried past the mixer"
    bsz, s_len, _ = x_prompt.shape
    dbs, d_seq, _ = x_sample.shape
    past = cache_mla_ckv.shape[2]
    w = _prepare_weights(ffn1_norm, ffn1_wg, ffn1_wu, ffn1_wd, mix_norm, w_in, conv_w, a_log, dt_bias, gdn_norm,
                         q_norm, kv_norm, w_uq, w_ukv, w_out, ffn2_norm, ffn2_wg, ffn2_wu, ffn2_wd, final_norm)

    front = CHUNK - N_META
    _, qkv_m, _, mla_m = ffn_proj(meta.astype(F32), w)
    meta_qkv = jnp.pad(qkv_m, ((front, 0), (0, 0)))[None]
    x1_p, feat_p, z_p, mla_p, tail_p = ffn_proj(x_prompt.reshape(bsz * s_len, D_MODEL), w,
                                                 hist=meta_qkv[0, CHUNK - GDN_HIST:], stream_len=s_len)
    x1_s, qkv_s, z_s, mla_s = ffn_proj(x_sample.reshape(dbs * d_seq, D_MODEL), w)

    _, m_meta = gdn(meta_qkv, jnp.zeros((1, CHUNK, H_GDN * GDN_DV), F32), jnp.pad(mla_m, ((front, 0), (0, 0)))[None],
                    jnp.zeros((1, H_GDN, GDN_DK, GDN_DV), F32), jnp.zeros((1, GDN_HIST, CONV_DIM), F32), w,
                    pad_front=front)
    m0_p = jnp.broadcast_to(m_meta, (bsz,) + m_meta.shape[1:])
    gdn_p, m_p = gdn(feat_p.reshape(bsz, s_len, CONV_DIM), z_p.reshape(bsz, s_len, -1),
                     mla_p.reshape(bsz, s_len, MLA_IN), m0_p, jnp.zeros((bsz, GDN_HIST, CONV_DIM), F32), w,
                     rows=GDN_PROMPT_ROWS, prepared=True)
    conv_p = tail_p.reshape(bsz, -1, GDN_HIST, CONV_DIM)[:, -1, GDN_HIST - (CONV_W - 1):]
    qkv_s3 = qkv_s.reshape(dbs, d_seq, CONV_DIM)
    conv0_s = jnp.pad(state_conv[0].astype(F32), ((0, 0), (GDN_HIST - (CONV_W - 1), 0), (0, 0)))
    gdn_s, m_s = gdn(qkv_s3, z_s.reshape(dbs, d_seq, -1), mla_s.reshape(dbs, d_seq, MLA_IN),
                     state_gdn[0].astype(F32), conv0_s, w, bg=GDN_SAMPLE_GROUP)

    cos_m, sin_m = _rope_tables(0, N_META)
    cos_p, sin_p = _rope_tables(N_META, s_len)
    cos_s, sin_s = _rope_tables(past, d_seq)
    q_p, c_p, kr_p, k_p, vt_p = mla_prep(mla_p, cos_p, sin_p, w, expand=True)
    _, c_m, kr_m, k_m, vt_m = mla_prep(mla_m, cos_m, sin_m, w, expand=True)
    q_s, c_s, kr_s = mla_prep(mla_s, cos_s, sin_s, w, expand=False)
    mla_o_p = mla_prompt(q_p.reshape(bsz, s_len, -1), k_p.reshape(bsz, s_len, -1), vt_p, k_m, vt_m)
    mla_o_s = mla_sample(q_s.reshape(dbs, d_seq, -1), c_s.reshape(dbs, d_seq, -1), kr_s.reshape(dbs, d_seq, -1),
                         cache_mla_ckv[0].astype(F32), jnp.swapaxes(cache_mla_krope[0].astype(F32), 1, 2), w['w_ukv'])

    y_p = out_ffn(x1_p, gdn_p.reshape(bsz * s_len, -1), mla_o_p.reshape(bsz * s_len, -1), w)
    y_s = out_ffn(x1_s, gdn_s.reshape(dbs * d_seq, -1), mla_o_s.reshape(dbs * d_seq, -1), w)

    with_meta = lambda m_rows, rows: jnp.concatenate(
        [jnp.broadcast_to(m_rows[None], (bsz,) + m_rows.shape), rows.reshape(bsz, s_len, -1)], axis=1)
    krope_t = jnp.concatenate([jnp.broadcast_to(kr_m[None], (bsz, DR, N_META)),
                               kr_p.reshape(DR, bsz, s_len).transpose(1, 0, 2)], axis=2)
    return (y_p.reshape(bsz, s_len, D_MODEL), y_s.reshape(dbs, d_seq, D_MODEL),
            with_meta(c_m, c_p)[None], jnp.swapaxes(krope_t, 1, 2)[None],
            m_p[None], conv_p[None],
            c_s.reshape(dbs, d_seq, KV_LORA)[None], kr_s[:, :DR].reshape(dbs, d_seq, DR)[None],
            m_s[None], qkv_s3[:, d_seq - (CONV_W - 1):][None])
```

```python
import functools

import jax
import jax.numpy as jnp
import numpy as np
from jax import lax
from jax.experimental import pallas as pl
from jax.experimental.pallas import tpu as pltpu

F32 = jnp.float32
BF16 = jnp.bfloat16

D_MODEL = 1024
CHUNK = 64
N_META = 16
H_GDN = 4
GDN_DK = 128
GDN_DV = 128
CONV_W = 4
CONV_DIM = H_GDN * (2 * GDN_DK + GDN_DV)
H_MLA = 4
Q_LORA = 384
KV_LORA = 256
DN = 128
DR = 64
DV_MLA = 128
ROPE_BASE = 10000.0
SM_SCALE = (DN + DR) ** -0.5
LOG2_E = 1.4426950408889634
Q_SCALE = SM_SCALE * LOG2_E
D_FF = 2816
EPS = 1e-6
L2_EPS = 1e-6

LANES = 128
SUBLANES = 8
FF_CHUNK = 256
MLA_IN = Q_LORA + KV_LORA + LANES
QK_PAD = 2 * LANES
AB_LANE = DR
VMEM_LIMIT = 56 * 1024 * 1024

NT_DIMS = (((1,), (1,)), ((), ()))


def _rms(x, g):
    return x * lax.rsqrt(jnp.mean(x * x, axis=-1, keepdims=True) + EPS) * g


def _bdot(a, b):
    return jnp.dot(a.astype(BF16), b.astype(BF16), preferred_element_type=F32)


def _bdot_nt(a, b):
    return lax.dot_general(a.astype(BF16), b.astype(BF16), NT_DIMS, preferred_element_type=F32)


def _const_spec(shape):
    nd = len(shape)
    return pl.BlockSpec(shape, lambda *_: (0,) * nd, pipeline_mode=pl.Buffered(1))


def _params(*sem):
    return pltpu.CompilerParams(dimension_semantics=sem, vmem_limit_bytes=VMEM_LIMIT)


def _drain(*stage_gens):
    live = [[g, 0.0, float(t)] for g, t in (sg if isinstance(sg, tuple) else (sg, 1.0) for sg in stage_gens)]
    while live:
        entry = min(live, key=lambda e: e[1] / e[2])
        try:
            entry[1] += next(entry[0]) or 0.0
        except StopIteration:
            live.remove(entry)


def _swiglu_half_stages(x, g_ref, wg_ref, wu_ref, wd_ref, h_ref, acc_ref, result):
    h_ref[...] = _rms(x, g_ref[...]).astype(BF16)
    nf = wg_ref.shape[1] // FF_CHUNK

    def gate_up(f):
        cols = slice(f * FF_CHUNK, (f + 1) * FF_CHUNK)
        gate = jnp.dot(h_ref[...], wg_ref[:, cols], preferred_element_type=F32)
        up = jnp.dot(h_ref[...], wu_ref[:, cols], preferred_element_type=F32)
        return gate, up

    nxt = gate_up(0)
    yield 1.0
    for f in range(nf):
        gate, up = nxt
        if f + 1 < nf:
            nxt = gate_up(f + 1)
            yield 1.0
        act = (jax.nn.silu(gate) * up).astype(BF16)
        down = jnp.dot(act, wd_ref[f * FF_CHUNK:(f + 1) * FF_CHUNK, :], preferred_element_type=F32)
        if f == 0:
            acc_ref[...] = down
        else:
            acc_ref[...] += down
        yield 1.0
    result.append(x + 0.5 * acc_ref[...])


def _swiglu_half(x, g_ref, wg_ref, wu_ref, wd_ref, h_ref, acc_ref):
    result = []
    _drain(_swiglu_half_stages(x, g_ref, wg_ref, wu_ref, wd_ref, h_ref, acc_ref, result))
    return result[0]


def _ffn_proj_kernel(x_ref, g1_ref, wg_ref, wu_ref, wd_ref, gm_ref, wqkv_ref, wz_ref, wmla_ref, *rest,
                     stream_tiles):
    if not stream_tiles:
        x1_ref, qkv_ref, z_ref, mla_ref, h_ref, acc_ref = rest
        x1 = _swiglu_half(x_ref[...], g1_ref, wg_ref, wu_ref, wd_ref, h_ref, acc_ref)
    else:
        hist_ref, convw_ref, x1_ref, qkv_ref, z_ref, mla_ref, tail_ref, h_ref, acc_ref, conv_scr = rest
        i = pl.program_id(0)
        tm = x_ref.shape[0]

        @pl.when(i == 0)
        def _():
            conv_scr[...] = jnp.zeros_like(conv_scr)

        @pl.when((i + stream_tiles - 1) % stream_tiles == 0)
        def _():
            conv_scr[0, 0:GDN_HIST, :] = hist_ref[...]

        def feature_stages():
            for row0 in range(0, tm, CHUNK):
                qkv_ref[row0:row0 + CHUNK, :] = _gdn_features(conv_scr, convw_ref, 0, row0)
                yield 1.0
            conv_scr[0, 0:GDN_HIST, :] = conv_scr[0, tm:tm + GDN_HIST, :]

        result = []
        _drain((_swiglu_half_stages(x_ref[...], g1_ref, wg_ref, wu_ref, wd_ref, h_ref, acc_ref, result),
                2 * (D_FF // FF_CHUNK)), (feature_stages(), tm // CHUNK))
        x1 = result[0]
    x1_ref[...] = x1
    h_ref[...] = _rms(x1, gm_ref[...]).astype(BF16)
    qkv = jnp.dot(h_ref[...], wqkv_ref[...], preferred_element_type=F32)
    z_ref[...] = jnp.dot(h_ref[...], wz_ref[...], preferred_element_type=F32)
    mla_ref[...] = jnp.dot(h_ref[...], wmla_ref[...], preferred_element_type=F32)
    if stream_tiles:
        conv_scr[0, GDN_HIST:GDN_HIST + tm, :] = qkv
        tail_ref[0] = qkv[tm - GDN_HIST:, :]
    else:
        qkv_ref[...] = qkv


def _row_tile(n, want):
    t = min(want, n)
    assert n % t == 0, (n, t)
    return t


def ffn_proj(x, w, hist=None, stream_len=None, tm=512):
    n = x.shape[0]
    tm = _row_tile(n, tm)
    ntile = n // tm
    stream_tiles = 0
    if hist is not None:
        assert stream_len % tm == 0 and tm % CHUNK == 0
        stream_tiles = stream_len // tm
    tile = (lambda i: jnp.minimum(i, ntile - 1)) if stream_tiles else (lambda i: i)
    row = lambda width: pl.BlockSpec((tm, width), lambda i: (tile(i), 0))
    consts = (w['g1'], w['wg1'], w['wu1'], w['wd1'], w['gm'], w['w_qkv'], w['w_z'], w['w_mla'])
    out_specs = [row(D_MODEL), row(CONV_DIM), row(H_GDN * GDN_DV), row(MLA_IN)]
    out_shape = [jax.ShapeDtypeStruct((n, D_MODEL), F32), jax.ShapeDtypeStruct((n, CONV_DIM), F32),
                 jax.ShapeDtypeStruct((n, H_GDN * GDN_DV), F32), jax.ShapeDtypeStruct((n, MLA_IN), F32)]
    scratch = [pltpu.VMEM((tm, D_MODEL), BF16), pltpu.VMEM((tm, D_MODEL), F32)]
    if stream_tiles:
        consts += (hist, w['conv_w'])
        out_specs[1] = pl.BlockSpec((tm, CONV_DIM), lambda i: (jnp.maximum(i - 1, 0), 0))
        out_specs.append(pl.BlockSpec((1, GDN_HIST, CONV_DIM), lambda i: (tile(i), 0, 0)))
        out_shape.append(jax.ShapeDtypeStruct((ntile, GDN_HIST, CONV_DIM), F32))
        scratch.append(pltpu.VMEM((1, GDN_HIST + tm, CONV_DIM), F32))
    return pl.pallas_call(
        functools.partial(_ffn_proj_kernel, stream_tiles=stream_tiles),
        grid=(ntile + (1 if stream_tiles else 0),),
        in_specs=[row(D_MODEL)] + [_const_spec(c.shape) for c in consts],
        out_specs=out_specs,
        out_shape=out_shape,
        scratch_shapes=scratch,
        compiler_params=_params("arbitrary" if stream_tiles else "parallel"),
        name="ffn_proj",
    )(x, *consts)


def _out_ffn_kernel(x1_ref, gdn_ref, mla_ref, wog_ref, wom_ref, g2_ref, wg_ref, wu_ref, wd_ref, gf_ref,
                    y_ref, h_ref, acc_ref):
    x2 = (x1_ref[...] + jnp.dot(gdn_ref[...], wog_ref[...], preferred_element_type=F32)
          + jnp.dot(mla_ref[...], wom_ref[...], preferred_element_type=F32))
    x3 = _swiglu_half(x2, g2_ref, wg_ref, wu_ref, wd_ref, h_ref, acc_ref)
    y_ref[...] = _rms(x3, gf_ref[...])


def out_ffn(x1, gdn, mla, w, tm=512):
    n = x1.shape[0]
    tm = _row_tile(n, tm)
    row = lambda width: pl.BlockSpec((tm, width), lambda i: (i, 0))
    consts = (w['w_out_g'], w['w_out_m'], w['g2'], w['wg2'], w['wu2'], w['wd2'], w['gf'])
    return pl.pallas_call(
        _out_ffn_kernel,
        grid=(n // tm,),
        in_specs=[row(D_MODEL), row(H_GDN * GDN_DV), row(H_MLA * DV_MLA)] + [_const_spec(c.shape) for c in consts],
        out_specs=row(D_MODEL),
        out_shape=jax.ShapeDtypeStruct((n, D_MODEL), F32),
        scratch_shapes=[pltpu.VMEM((tm, D_MODEL), BF16), pltpu.VMEM((tm, D_MODEL), F32)],
        compiler_params=_params("parallel"),
        name="out_ffn",
    )(x1, gdn, mla, *consts)


def _cumsum_rows(x):
    n = x.shape[0]
    row = lax.broadcasted_iota(jnp.int32, x.shape, 0)
    shift = 1
    while shift < n:
        x = x + jnp.where(row >= shift, pltpu.roll(x, shift, 0), 0.0)
        shift *= 2
    return x


def _transpose_rows(x):
    length = x.shape[0]
    sq = jnp.concatenate([x, jnp.zeros((LANES - length, LANES), x.dtype)], axis=0)
    return sq.T[:, :length]


GDN_HIST = SUBLANES
GDN_PREP_COST = 4.0
GDN_PROMPT_ROWS = 2 * CHUNK
GDN_SAMPLE_GROUP = 8


def _gdn_features(conv_scr, convw_ref, i, row0):
    base = GDN_HIST - (CONV_W - 1) + row0
    parts = []
    for h in range(CONV_DIM // GDN_DK):
        cols = slice(h * GDN_DK, (h + 1) * GDN_DK)
        x = conv_scr[i, pl.ds(base, CHUNK), cols] * convw_ref[0:1, cols]
        for j in range(1, CONV_W):
            x = x + conv_scr[i, pl.ds(base + j, CHUNK), cols] * convw_ref[j:j + 1, cols]
        x = jax.nn.silu(x)
        if h < 2 * H_GDN:
            x = x * lax.rsqrt(jnp.sum(x * x, axis=-1, keepdims=True) + L2_EPS)
        parts.append(x * (GDN_DK ** -0.5) if h < H_GDN else x)
    return jnp.concatenate(parts, axis=1)


def _gdn_block_stages(first_block, rows, pad_front, features, ab_ref, z_ref, write_o, m_scr,
                      alog_ref, dtb_ref, gn_ref):
    bg = ab_ref.shape[0]
    length = CHUNK
    heads = H_GDN
    width = heads * length
    row = lax.broadcasted_iota(jnp.int32, (length, LANES), 0)
    lane = lax.broadcasted_iota(jnp.int32, (length, width), 1)
    lane_head = lane // length
    lane_col = lane % length
    row_p = lax.broadcasted_iota(jnp.int32, (length, width), 0)
    causal = lane_col <= row_p
    strict = lane_col < row_p

    def spread(cols):
        out = jnp.broadcast_to(cols[-1], (length, width))
        for h in range(heads - 2, -1, -1):
            out = jnp.where(lane_head == h, cols[h], out)
        return out

    def pick(mats):
        out = mats[-1]
        for h in range(heads - 2, -1, -1):
            out = jnp.where(lane_head == h, mats[h], out)
        return out

    def diag_blocks(p):
        return jnp.concatenate([jnp.where(lane_head == h, p, 0.0) for h in range(heads)], axis=0).astype(BF16)

    def diag_wide(mats):
        zero = jnp.zeros_like(mats[0])
        return jnp.concatenate(
            [jnp.concatenate([mats[h] if g == h else zero for g in range(heads)], axis=1) for h in range(heads)],
            axis=0).astype(BF16)

    def per_batch(i, row0):
        u = features(i, row0)
        ab = ab_ref[i, row0:row0 + length, :]
        g_all = -jnp.exp(alog_ref[...]) * jax.nn.softplus(ab + dtb_ref[...])
        beta_all = jax.nn.sigmoid(ab)
        if pad_front > row0:
            valid = jnp.logical_or(jnp.logical_not(first_block), row >= pad_front - row0)
            g_all = jnp.where(valid, g_all, 0.0)
            beta_all = jnp.where(valid, beta_all, 0.0)
        gc_all = _cumsum_rows(g_all)
        gc_rows = jnp.concatenate([gc_all, jnp.zeros((LANES - length, LANES), F32)], axis=0).T
        gc_rows = gc_rows[AB_LANE:AB_LANE + SUBLANES, :]
        gc_rows_hi = pltpu.roll(gc_rows, length, 1)
        gc_row = jnp.concatenate([gc_rows[h:h + 1] + gc_rows_hi[h + 1:h + 2] for h in range(0, heads, 2)], axis=1)

        hd = []
        for h in range(heads):
            q = u[:, h * GDN_DK:(h + 1) * GDN_DK]
            k = u[:, (heads + h) * GDN_DK:(heads + h + 1) * GDN_DK]
            v = u[:, 2 * heads * GDN_DK + h * GDN_DV:2 * heads * GDN_DK + (h + 1) * GDN_DV]
            gc = gc_all[:, AB_LANE + h:AB_LANE + h + 1]
            beta = beta_all[:, AB_LANE + heads + h:AB_LANE + heads + h + 1]
            kq = jnp.concatenate([k, q], axis=0).astype(BF16)
            hd.append(dict(h=h, k=k, v=v, gc=gc, beta=beta, kq=kq))
        gc_col = spread([c['gc'] for c in hd])
        decay = jnp.where(causal, jnp.exp(jnp.where(causal, gc_col - gc_row, 0.0)), 0.0)
        k_all = jnp.concatenate([c['k'] for c in hd], axis=0).astype(BF16)
        return dict(i=i, row0=row0, heads=hd, decay=decay, beta=spread([c['beta'] for c in hd]), k_all=k_all)

    per_chunk = []
    for row0 in range(0, rows, length):
        per_chunk.append([])
        for i in range(bg):
            per_chunk[-1].append(per_batch(i, row0))
            yield GDN_PREP_COST
    groups = [g for chunk in per_chunk for g in chunk]
    for g in groups:
        scores = [lax.dot_general(c['kq'], g['k_all'], NT_DIMS, preferred_element_type=F32) for c in g['heads']]
        g['kk'] = pick([s[:length] for s in scores])
        g['qk'] = pick([s[length:] for s in scores])
    yield 1.0
    for g in groups:
        g['a'] = jnp.where(strict, g['beta'] * g['kk'] * g['decay'], 0.0)
        g['qk_decay'] = g['qk'] * g['decay']
        g['y'] = -g['a']
    for g in groups:
        g['pw'] = jnp.dot(g['a'].astype(BF16), diag_blocks(g['a']), preferred_element_type=F32)
    yield 1.0
    span = 2
    while span < length:
        span *= 2
        for g in groups:
            blocks = diag_blocks(g['pw'])
            if span < length:
                prod = jnp.dot(jnp.concatenate([g['y'], g['pw']], axis=0).astype(BF16), blocks,
                               preferred_element_type=F32)
                g['y'] = g['y'] + g['pw'] + prod[:length]
                g['pw'] = prod[length:]
            else:
                g['y'] = g['y'] + g['pw'] + jnp.dot(g['y'].astype(BF16), blocks, preferred_element_type=F32)
        yield 1.0

    def emit_output(chunk):
        for g in chunk:
            prod = jnp.dot(g['qk_decay'].astype(BF16), diag_wide([c['uu'] for c in g['heads']]),
                           preferred_element_type=F32)
            rws = slice(g['row0'], g['row0'] + length)
            for c in g['heads']:
                cols = slice(c['h'] * GDN_DV, (c['h'] + 1) * GDN_DV)
                o = c['eg'] * c['kqm'][length:] + prod[:, cols]
                n = _rms(o, gn_ref[...])
                write_o(g['i'], rws, cols, (n * jax.nn.silu(z_ref[g['i'], rws, cols])).astype(BF16))

    pending = None
    for chunk in per_chunk:
        for g in chunk:
            for c in g['heads']:
                c['m0'] = m_scr[g['i'], c['h']]
                c['kqm'] = jnp.dot(c['kq'], c['m0'].astype(BF16), preferred_element_type=F32)
        yield 1.0
        if pending is not None:
            emit_output(pending)
            yield 1.0
        for g in chunk:
            for c in g['heads']:
                c['eg'] = jnp.exp(c['gc'])
                c['rhs'] = c['beta'] * (c['v'] - c['eg'] * c['kqm'][:length])
            prod = jnp.dot(g['y'].astype(BF16), diag_wide([c['rhs'] for c in g['heads']]),
                           preferred_element_type=F32)
            for c in g['heads']:
                c['uu'] = c['rhs'] + prod[:, c['h'] * GDN_DV:(c['h'] + 1) * GDN_DV]
        yield 1.0
        for g in chunk:
            for c in g['heads']:
                g_last = c['gc'][length - 1:length, :]
                k_dec = c['k'] * jnp.exp(g_last - c['gc'])
                m_scr[g['i'], c['h']] = jnp.exp(g_last) * c['m0'] + _bdot(_transpose_rows(k_dec), c['uu'])
        yield 1.0
        pending = chunk
    emit_output(pending)
    yield 1.0


def _gdn_stage_block(qkv_ref, conv_scr):
    bg, rows, _ = qkv_ref.shape
    for i in range(bg):
        conv_scr[i, GDN_HIST:GDN_HIST + rows, :] = qkv_ref[i]


def _gdn_keep_history(conv_scr, rows):
    for i in range(conv_scr.shape[0]):
        conv_scr[i, 0:GDN_HIST, :] = conv_scr[i, rows:rows + GDN_HIST, :]


def _gdn_kernel(qkv_ref, z_ref, ab_ref, m0_ref, conv0_ref, convw_ref, alog_ref, dtb_ref, gn_ref,
                o_ref, mout_ref, conv_scr, m_scr, *, pad_front, prepared):
    c = pl.program_id(1)
    rows = qkv_ref.shape[1]

    @pl.when(c == 0)
    def _():
        m_scr[...] = m0_ref[...]
        conv_scr[:, 0:GDN_HIST, :] = conv0_ref[...]

    def write_o(i, rws, cols, val):
        o_ref[i, rws, cols] = val

    if prepared:
        features = lambda i, row0: qkv_ref[i, row0:row0 + CHUNK, :]
    else:
        _gdn_stage_block(qkv_ref, conv_scr)
        features = functools.partial(_gdn_features, conv_scr, convw_ref)
    _drain(_gdn_block_stages(c == 0, rows, pad_front, features, ab_ref, z_ref, write_o, m_scr,
                             alog_ref, dtb_ref, gn_ref))
    if not prepared:
        _gdn_keep_history(conv_scr, rows)

    @pl.when(c == pl.num_programs(1) - 1)
    def _():
        mout_ref[...] = m_scr[...]


def _gdn_in_specs(bg, rows, blk):
    nd = len(blk(0, 0))
    const = lambda shape: pl.BlockSpec(shape, lambda *_: (0,) * len(shape))
    return [
        pl.BlockSpec((bg, rows, CONV_DIM), blk),
        pl.BlockSpec((bg, rows, H_GDN * GDN_DV), blk),
        pl.BlockSpec((bg, rows, LANES), lambda *a: blk(*a)[:nd - 1] + (MLA_IN // LANES - 1,)),
        pl.BlockSpec((bg, H_GDN, GDN_DK, GDN_DV), lambda g, c: (g, 0, 0, 0)),
        pl.BlockSpec((bg, GDN_HIST, CONV_DIM), lambda g, c: (g, 0, 0)),
        const((CONV_W, CONV_DIM)),
        const((1, LANES)),
        const((1, LANES)),
        const((1, GDN_DV)),
    ]


def gdn(qkv, z, mla_in, m0, conv0, w, *, pad_front=0, bg=4, rows=CHUNK, prepared=False):
    b, t, _ = qkv.shape
    bg = min(bg, b)
    assert b % bg == 0 and t % rows == 0 and rows % CHUNK == 0
    blk = lambda g, c: (g, c, 0)
    return pl.pallas_call(
        functools.partial(_gdn_kernel, pad_front=pad_front, prepared=prepared),
        grid=(b // bg, t // rows),
        in_specs=_gdn_in_specs(bg, rows, blk),
        out_specs=[
            pl.BlockSpec((bg, rows, H_GDN * GDN_DV), blk),
            pl.BlockSpec((bg, H_GDN, GDN_DK, GDN_DV), lambda g, c: (g, 0, 0, 0)),
        ],
        out_shape=[jax.ShapeDtypeStruct((b, t, H_GDN * GDN_DV), BF16),
                   jax.ShapeDtypeStruct((b, H_GDN, GDN_DK, GDN_DV), F32)],
        scratch_shapes=[pltpu.VMEM((bg, GDN_HIST + (0 if prepared else rows), CONV_DIM), F32),
                        pltpu.VMEM((bg, H_GDN, GDN_DK, GDN_DV), F32)],
        compiler_params=_params("parallel", "arbitrary"),
        name="gdn",
    )(qkv, z, mla_in, m0, conv0, w['conv_w'], w['alog'], w['dtb'], w['gn'])


def _rope(t, cos, sin):
    lane = lax.broadcasted_iota(jnp.int32, t.shape, 1)
    half = DR // 2
    swapped = jnp.where(lane < half, pltpu.roll(t, LANES - half, 1), pltpu.roll(t, half, 1))
    return t * cos + swapped * sin


def _mla_prep_kernel(x_ref, cos_ref, sin_ref, gq_ref, gkv_ref, wuq_ref, wukv_ref, *out_refs, expand):
    q_ref, c_ref, kr_ref = out_refs[:3]
    x = x_ref[...]
    cos = cos_ref[...]
    sin = sin_ref[...]
    q = _bdot(_rms(x[:, :Q_LORA], gq_ref[...]), wuq_ref[...])
    for h in range(H_MLA):
        lo = h * QK_PAD
        q_ref[:, lo:lo + DN] = (q[:, lo:lo + DN] * Q_SCALE).astype(BF16)
        q_ref[:, lo + DN:lo + QK_PAD] = (_rope(q[:, lo + DN:lo + QK_PAD], cos, sin) * Q_SCALE).astype(BF16)
    c = _rms(x[:, Q_LORA:Q_LORA + KV_LORA], gkv_ref[...])
    c_ref[...] = c
    kr = _rope(x[:, Q_LORA + KV_LORA:], cos, sin)
    if not expand:
        kr_ref[...] = kr
    else:
        kr_ref[...] = (kr.T if kr.shape[0] % LANES == 0 else _transpose_rows(kr))[:DR]
        k_ref, vt_ref = out_refs[3:]
        kv = _bdot(c, wukv_ref[...])
        for h in range(H_MLA):
            lo = h * (DN + DV_MLA)
            k_ref[:, h * QK_PAD:h * QK_PAD + DN] = kv[:, lo:lo + DN].astype(BF16)
            k_ref[:, h * QK_PAD + DN:(h + 1) * QK_PAD] = kr.astype(BF16)
            v = kv[:, lo + DN:lo + DN + DV_MLA]
            v_t = v.T if v.shape[0] % LANES == 0 else _transpose_rows(v)
            vt_ref[h * DV_MLA:(h + 1) * DV_MLA, :] = v_t.astype(BF16)


MLA_PREP_TILE = 512


def mla_prep(mla_in, cos, sin, w, *, expand, tm=MLA_PREP_TILE):
    n = mla_in.shape[0]
    tm = _row_tile(min(n, cos.shape[0]), tm)
    nrep = cos.shape[0] // tm
    row = lambda width: pl.BlockSpec((tm, width), lambda i: (i, 0))
    tab = pl.BlockSpec((tm, LANES), lambda i: (i % nrep, 0))
    consts = (w['gq'], w['gkv'], w['w_uq'], w['w_ukv'])
    out_specs = [row(H_MLA * QK_PAD), row(KV_LORA), row(LANES)]
    out_shape = [jax.ShapeDtypeStruct((n, H_MLA * QK_PAD), BF16), jax.ShapeDtypeStruct((n, KV_LORA), F32),
                 jax.ShapeDtypeStruct((n, LANES), F32)]
    if expand:
        out_specs[2] = pl.BlockSpec((DR, tm), lambda i: (0, i))
        out_shape[2] = jax.ShapeDtypeStruct((DR, n), F32)
        out_specs += [row(H_MLA * QK_PAD), pl.BlockSpec((H_MLA * DV_MLA, tm), lambda i: (0, i))]
        out_shape += [jax.ShapeDtypeStruct((n, H_MLA * QK_PAD), BF16), jax.ShapeDtypeStruct((H_MLA * DV_MLA, n), BF16)]
    return pl.pallas_call(
        functools.partial(_mla_prep_kernel, expand=expand),
        grid=(n // tm,),
        in_specs=[row(MLA_IN), tab, tab] + [_const_spec(c.shape) for c in consts],
        out_specs=out_specs,
        out_shape=out_shape,
        compiler_params=_params("parallel"),
        name="mla_prep_kv" if expand else "mla_prep",
    )(mla_in, cos, sin, *consts)


def _mla_prompt_kernel(q_ref, k_ref, vt_ref, km_ref, vmt_ref, o_ref, s_scr, m_scr, l_scr, acc_scr):
    qi = pl.program_id(1)
    tq = q_ref.shape[0]
    heads = range(H_MLA)

    def scores(off, h):
        return lax.dot_general(k_ref[pl.ds(off, tq), h * QK_PAD:(h + 1) * QK_PAD],
                               q_ref[:, h * QK_PAD:(h + 1) * QK_PAD], NT_DIMS, preferred_element_type=F32)

    def consume(h, off, masked):
        s_t = s_scr[h]
        if masked:
            key_chunk = lax.broadcasted_iota(jnp.int32, (tq, tq), 0) // CHUNK
            qry_chunk = lax.broadcasted_iota(jnp.int32, (tq, tq), 1) // CHUNK
            s_t = jnp.where(key_chunk <= qry_chunk, s_t, -jnp.inf)
        m = m_scr[h]
        m_new = jnp.maximum(m, jnp.max(s_t, axis=0, keepdims=True))
        alpha = jnp.exp2(m - m_new)
        p_t = jnp.exp2(s_t - m_new)
        m_scr[h] = m_new
        l_scr[h] = alpha * l_scr[h] + jnp.sum(p_t, axis=0, keepdims=True)
        acc_scr[h] = alpha * acc_scr[h] + jnp.dot(vt_ref[h * DV_MLA:(h + 1) * DV_MLA, pl.ds(off, tq)],
                                                  p_t.astype(BF16), preferred_element_type=F32)

    meta_s = [lax.dot_general(km_ref[:, h * QK_PAD:(h + 1) * QK_PAD], q_ref[:, h * QK_PAD:(h + 1) * QK_PAD],
                              NT_DIMS, preferred_element_type=F32) for h in heads]
    for h in heads:
        s_scr[h] = scores(0, h)
    meta_p = []
    for h in heads:
        m = jnp.max(meta_s[h], axis=0, keepdims=True)
        p_t = jnp.exp2(meta_s[h] - m)
        m_scr[h] = m
        l_scr[h] = jnp.sum(p_t, axis=0, keepdims=True)
        meta_p.append(p_t.astype(BF16))
    for h in heads:
        acc_scr[h] = jnp.dot(vmt_ref[h * DV_MLA:(h + 1) * DV_MLA, :], meta_p[h], preferred_element_type=F32)

    def full_tile(t):
        off = pl.multiple_of(t * tq, tq)
        nxt = pl.multiple_of((t + 1) * tq, tq)
        for h in heads:
            s_next = scores(nxt, h)
            consume(h, off, False)
            s_scr[h] = s_next

    def tile_pair(t2, carry):
        full_tile(2 * t2)
        full_tile(2 * t2 + 1)
        return carry

    lax.fori_loop(0, qi // 2, tile_pair, 0)

    @pl.when(qi % 2 == 1)
    def _():
        full_tile(qi - 1)

    for h in heads:
        consume(h, pl.multiple_of(qi * tq, tq), True)
    for h in heads:
        o_ref[:, h * DV_MLA:(h + 1) * DV_MLA] = (acc_scr[h] / l_scr[h]).T.astype(o_ref.dtype)


def mla_prompt(q, k, v_t, k_meta, v_meta_t, tq=512):
    b, s, _ = q.shape
    tq = _row_tile(s, tq)
    assert tq % LANES == 0
    whole = lambda shape: pl.BlockSpec(shape, lambda bi, i: (0, 0))
    return pl.pallas_call(
        _mla_prompt_kernel,
        grid=(b, s // tq),
        in_specs=[
            pl.BlockSpec((None, tq, H_MLA * QK_PAD), lambda bi, i: (bi, i, 0)),
            pl.BlockSpec((None, s, H_MLA * QK_PAD), lambda bi, i: (bi, 0, 0), pipeline_mode=pl.Buffered(1)),
            pl.BlockSpec((H_MLA * DV_MLA, s), lambda bi, i: (0, bi)),
            whole(k_meta.shape),
            whole(v_meta_t.shape),
        ],
        out_specs=pl.BlockSpec((None, tq, H_MLA * DV_MLA), lambda bi, i: (bi, i, 0)),
        out_shape=jax.ShapeDtypeStruct((b, s, H_MLA * DV_MLA), BF16),
        scratch_shapes=[pltpu.VMEM((H_MLA, tq, tq), F32), pltpu.VMEM((H_MLA, 1, tq), F32),
                        pltpu.VMEM((H_MLA, 1, tq), F32), pltpu.VMEM((H_MLA, DV_MLA, tq), F32)],
        compiler_params=_params("parallel", "arbitrary"),
        name="mla_prompt",
    )(q, k, v_t, k_meta, v_meta_t)


def _mla_sample_kernel(q_ref, c_ref, kr_ref, cc_ref, ckr_ref, wukv_ref, o_ref):
    nreq, t, _ = q_ref.shape
    w = wukv_ref[...]
    reqs = []
    for r in range(nreq):
        q = q_ref[r]
        qa = jnp.concatenate(
            [lax.dot_general(q[:, h * QK_PAD:h * QK_PAD + DN], w[:, h * (DN + DV_MLA):h * (DN + DV_MLA) + DN],
                             NT_DIMS, preferred_element_type=F32) for h in range(H_MLA)], axis=0).astype(BF16)
        qr = jnp.concatenate([q[:, h * QK_PAD + DN:(h + 1) * QK_PAD] for h in range(H_MLA)], axis=0)
        reqs.append(dict(qa=qa, qr=qr, cache_c=cc_ref[r].astype(BF16), own_c=c_ref[r].astype(BF16)))
    for r, rq in enumerate(reqs):
        rq['s_cache'] = (lax.dot_general(rq['qa'], rq['cache_c'], NT_DIMS, preferred_element_type=F32)
                         + jnp.dot(rq['qr'][:, :DR], ckr_ref[r].astype(BF16), preferred_element_type=F32))
        rq['s_own'] = (lax.dot_general(rq['qa'], rq['own_c'], NT_DIMS, preferred_element_type=F32)
                       + lax.dot_general(rq['qr'], kr_ref[r].astype(BF16), NT_DIMS, preferred_element_type=F32))
    for rq in reqs:
        m = jnp.maximum(jnp.max(rq['s_cache'], axis=-1, keepdims=True), jnp.max(rq['s_own'], axis=-1, keepdims=True))
        p_cache = jnp.exp2(rq['s_cache'] - m)
        p_own = jnp.exp2(rq['s_own'] - m)
        rq['l'] = jnp.sum(p_cache, axis=-1, keepdims=True) + jnp.sum(p_own, axis=-1, keepdims=True)
        rq['p_cache'], rq['p_own'] = p_cache.astype(BF16), p_own.astype(BF16)
    for rq in reqs:
        rq['pc'] = (jnp.dot(rq['p_cache'], rq['cache_c'], preferred_element_type=F32)
                    + jnp.dot(rq['p_own'], rq['own_c'], preferred_element_type=F32)) / rq['l']
    for r, rq in enumerate(reqs):
        for h in range(H_MLA):
            lo = h * (DN + DV_MLA) + DN
            o_ref[r, :, h * DV_MLA:(h + 1) * DV_MLA] = _bdot(rq['pc'][h * t:(h + 1) * t],
                                                             w[:, lo:lo + DV_MLA]).astype(o_ref.dtype)


def mla_sample(q, c, kr, cache_c, cache_kr, w_ukv, group=4):
    b, t, _ = q.shape
    p = cache_c.shape[1]
    group = min(group, b)
    assert b % group == 0
    per_b = lambda rows, width: pl.BlockSpec((group, rows, width), lambda bi: (bi, 0, 0))
    return pl.pallas_call(
        _mla_sample_kernel,
        grid=(b // group,),
        in_specs=[per_b(t, H_MLA * QK_PAD), per_b(t, KV_LORA), per_b(t, LANES), per_b(p, KV_LORA), per_b(DR, p),
                  _const_spec(w_ukv.shape)],
        out_specs=per_b(t, H_MLA * DV_MLA),
        out_shape=jax.ShapeDtypeStruct((b, t, H_MLA * DV_MLA), BF16),
        compiler_params=_params("parallel"),
        name="mla_sample",
    )(q, c, kr, cache_c, cache_kr, w_ukv)


def _rope_tables(first, count, repeat=1):
    inv = ROPE_BASE ** (-np.arange(0, DR, 2, dtype=np.float64) / DR)
    ang = (first + np.arange(count, dtype=np.float64))[:, None] * inv[None, :]
    cos, sin = np.cos(ang), np.sin(ang)
    pad = np.zeros((count, LANES - DR))
    table = lambda parts: jnp.asarray(np.tile(np.concatenate(parts, axis=1), (repeat, 1)), F32)
    return table([cos, cos, pad]), table([-sin, sin, pad])


def _prepare_weights(ffn1_norm, ffn1_wg, ffn1_wu, ffn1_wd, mix_norm, w_in, conv_w, a_log, dt_bias, gdn_norm,
                     q_norm, kv_norm, w_uq, w_ukv, w_out, ffn2_norm, ffn2_wg, ffn2_wu, ffn2_wd, final_norm):
    w = {}
    w['g1'], w['gm'], w['g2'] = ffn1_norm[0][None], mix_norm[0][None], ffn2_norm[0][None]
    w['gf'] = final_norm[None]
    assert D_FF % FF_CHUNK == 0
    w['wg1'], w['wu1'], w['wd1'] = (m[0].astype(BF16) for m in (ffn1_wg, ffn1_wu, ffn1_wd))
    w['wg2'], w['wu2'], w['wd2'] = (m[0].astype(BF16) for m in (ffn2_wg, ffn2_wu, ffn2_wd))
    o_z = CONV_DIM
    o_a = o_z + H_GDN * GDN_DV
    o_b = o_a + H_GDN
    o_cq = o_b + H_GDN
    o_kr = o_cq + Q_LORA + KV_LORA
    wi = w_in[0]
    w['w_qkv'] = wi[:, :o_z].astype(BF16)
    w['w_z'] = wi[:, o_z:o_a].astype(BF16)
    tail_pad = jnp.zeros((D_MODEL, LANES - DR - 2 * H_GDN), wi.dtype)
    w['w_mla'] = jnp.concatenate([wi[:, o_cq:o_kr], wi[:, o_kr:o_kr + DR], wi[:, o_a:o_cq], tail_pad], axis=1).astype(BF16)
    w['conv_w'] = conv_w[0]
    lane_vec = lambda v: jnp.zeros((1, LANES), F32).at[0, AB_LANE:AB_LANE + H_GDN].set(v.astype(F32))
    w['alog'], w['dtb'] = lane_vec(a_log[0]), lane_vec(dt_bias[0])
    w['gn'] = gdn_norm[0][None]
    w['gq'], w['gkv'] = q_norm[0][None], kv_norm[0][None]
    uq = w_uq[0].reshape(Q_LORA, H_MLA, DN + DR)
    uq = jnp.concatenate([uq, jnp.zeros((Q_LORA, H_MLA, QK_PAD - DN - DR), uq.dtype)], axis=-1)
    w['w_uq'] = uq.reshape(Q_LORA, H_MLA * QK_PAD).astype(BF16)
    w['w_ukv'] = w_ukv[0].astype(BF16)
    w['w_out_g'] = w_out[0][:H_GDN * GDN_DV].astype(BF16)
    w['w_out_m'] = w_out[0][H_GDN * GDN_DV:].astype(BF16)
    return w


def kernel(x_prompt, x_sample, cache_mla_ckv, cache_mla_krope, state_gdn, state_conv, meta, ffn1_norm, ffn1_wg,
           ffn1_wu, ffn1_wd, mix_norm, w_in, conv_w, a_log, dt_bias, gdn_norm, q_norm, kv_norm, w_uq, w_ukv, w_out,
           ffn2_norm, ffn2_wg, ffn2_wu, ffn2_wd, final_norm):
    assert ffn1_wg.shape[0] == 1, "one layer: the meta rows are not carried past the mixer"
    bsz, s_len, _ = x_prompt.shape
    dbs, d_seq, _ = x_sample.shape
    past = cache_mla_ckv.shape[2]
    w = _prepare_weights(ffn1_norm, ffn1_wg, ffn1_wu, ffn1_wd, mix_norm, w_in, conv_w, a_log, dt_bias, gdn_norm,
                         q_norm, kv_norm, w_uq, w_ukv, w_out, ffn2_norm, ffn2_wg, ffn2_wu, ffn2_wd, final_norm)

    front = CHUNK - N_META
    _, qkv_m, _, mla_m = ffn_proj(meta.astype(F32), w)
    meta_qkv = jnp.pad(qkv_m, ((front, 0), (0, 0)))[None]
    x1_p, feat_p, z_p, mla_p, tail_p = ffn_proj(x_prompt.reshape(bsz * s_len, D_MODEL), w,
                                                 hist=meta_qkv[0, CHUNK - GDN_HIST:], stream_len=s_len)
    x1_s, qkv_s, z_s, mla_s = ffn_proj(x_sample.reshape(dbs * d_seq, D_MODEL), w)

    _, m_meta = gdn(meta_qkv, jnp.zeros((1, CHUNK, H_GDN * GDN_DV), F32), jnp.pad(mla_m, ((front, 0), (0, 0)))[None],
                    jnp.zeros((1, H_GDN, GDN_DK, GDN_DV), F32), jnp.zeros((1, GDN_HIST, CONV_DIM), F32), w,
                    pad_front=front)
    m0_p = jnp.broadcast_to(m_meta, (bsz,) + m_meta.shape[1:])
    gdn_p, m_p = gdn(feat_p.reshape(bsz, s_len, CONV_DIM), z_p.reshape(bsz, s_len, -1),
                     mla_p.reshape(bsz, s_len, MLA_IN), m0_p, jnp.zeros((bsz, GDN_HIST, CONV_DIM), F32), w,
                     rows=GDN_PROMPT_ROWS, prepared=True)
    conv_p = tail_p.reshape(bsz, -1, GDN_HIST, CONV_DIM)[:, -1, GDN_HIST - (CONV_W - 1):]
    qkv_s3 = qkv_s.reshape(dbs, d_seq, CONV_DIM)
    conv0_s = jnp.pad(state_conv[0].astype(F32), ((0, 0), (GDN_HIST - (CONV_W - 1), 0), (0, 0)))
    gdn_s, m_s = gdn(qkv_s3, z_s.reshape(dbs, d_seq, -1), mla_s.reshape(dbs, d_seq, MLA_IN),
                     state_gdn[0].astype(F32), conv0_s, w, bg=GDN_SAMPLE_GROUP)

    cos_m, sin_m = _rope_tables(0, N_META)
    cos_p, sin_p = _rope_tables(N_META, s_len)
    cos_s, sin_s = _rope_tables(past, d_seq, repeat=min(dbs, MLA_PREP_TILE // d_seq))
    q_p, c_p, kr_p, k_p, vt_p = mla_prep(mla_p, cos_p, sin_p, w, expand=True)
    _, c_m, kr_m, k_m, vt_m = mla_prep(mla_m, cos_m, sin_m, w, expand=True)
    q_s, c_s, kr_s = mla_prep(mla_s, cos_s, sin_s, w, expand=False)
    mla_o_p = mla_prompt(q_p.reshape(bsz, s_len, -1), k_p.reshape(bsz, s_len, -1), vt_p, k_m, vt_m)
    mla_o_s = mla_sample(q_s.reshape(dbs, d_seq, -1), c_s.reshape(dbs, d_seq, -1), kr_s.reshape(dbs, d_seq, -1),
                         cache_mla_ckv[0].astype(F32), jnp.swapaxes(cache_mla_krope[0].astype(F32), 1, 2), w['w_ukv'])

    y_p = out_ffn(x1_p, gdn_p.reshape(bsz * s_len, -1), mla_o_p.reshape(bsz * s_len, -1), w)
    y_s = out_ffn(x1_s, gdn_s.reshape(dbs * d_seq, -1), mla_o_s.reshape(dbs * d_seq, -1), w)

    with_meta = lambda m_rows, rows: jnp.concatenate(
        [jnp.broadcast_to(m_rows[None], (bsz,) + m_rows.shape), rows.reshape(bsz, s_len, -1)], axis=1)
    krope_t = jnp.concatenate([jnp.broadcast_to(kr_m[None], (bsz, DR, N_META)),
                               kr_p.reshape(DR, bsz, s_len).transpose(1, 0, 2)], axis=2)
    return (y_p.reshape(bsz, s_len, D_MODEL), y_s.reshape(dbs, d_seq, D_MODEL),
            with_meta(c_m, c_p)[None], jnp.swapaxes(krope_t, 1, 2)[None],
            m_p[None], conv_p[None],
            c_s.reshape(dbs, d_seq, KV_LORA)[None], kr_s[:, :DR].reshape(dbs, d_seq, DR)[None],
            m_s[None], qkv_s3[:, d_seq - (CONV_W - 1):][None])
```

```python
import functools

import jax
import jax.numpy as jnp
import numpy as np
from jax import lax
from jax.experimental import pallas as pl
from jax.experimental.pallas import tpu as pltpu

F32 = jnp.float32
BF16 = jnp.bfloat16

D_MODEL = 1024
CHUNK = 64
N_META = 16
H_GDN = 4
GDN_DK = 128
GDN_DV = 128
CONV_W = 4
CONV_DIM = H_GDN * (2 * GDN_DK + GDN_DV)
H_MLA = 4
Q_LORA = 384
KV_LORA = 256
DN = 128
DR = 64
DV_MLA = 128
ROPE_BASE = 10000.0
SM_SCALE = (DN + DR) ** -0.5
LOG2_E = 1.4426950408889634
Q_SCALE = SM_SCALE * LOG2_E
D_FF = 2816
EPS = 1e-6
L2_EPS = 1e-6

LANES = 128
SUBLANES = 8
FF_CHUNK = 256
MLA_IN = Q_LORA + KV_LORA + LANES
QK_PAD = 2 * LANES
AB_LANE = DR
VMEM_LIMIT = 56 * 1024 * 1024

NT_DIMS = (((1,), (1,)), ((), ()))


def _rms(x, g):
    return x * lax.rsqrt(jnp.mean(x * x, axis=-1, keepdims=True) + EPS) * g


def _bdot(a, b):
    return jnp.dot(a.astype(BF16), b.astype(BF16), preferred_element_type=F32)


def _bdot_nt(a, b):
    return lax.dot_general(a.astype(BF16), b.astype(BF16), NT_DIMS, preferred_element_type=F32)


def _const_spec(shape):
    nd = len(shape)
    return pl.BlockSpec(shape, lambda *_: (0,) * nd, pipeline_mode=pl.Buffered(1))


def _params(*sem):
    return pltpu.CompilerParams(dimension_semantics=sem, vmem_limit_bytes=VMEM_LIMIT)


def _drain(*stage_gens):
    live = [[g, 0.0, float(t)] for g, t in (sg if isinstance(sg, tuple) else (sg, 1.0) for sg in stage_gens)]
    while live:
        entry = min(live, key=lambda e: e[1] / e[2])
        try:
            entry[1] += next(entry[0]) or 0.0
        except StopIteration:
            live.remove(entry)


def _swiglu_half_stages(x, g_ref, wg_ref, wu_ref, wd_ref, h_ref, acc_ref, result):
    h_ref[...] = _rms(x, g_ref[...]).astype(BF16)
    nf = wg_ref.shape[1] // FF_CHUNK

    def gate_up(f):
        cols = slice(f * FF_CHUNK, (f + 1) * FF_CHUNK)
        gate = jnp.dot(h_ref[...], wg_ref[:, cols], preferred_element_type=F32)
        up = jnp.dot(h_ref[...], wu_ref[:, cols], preferred_element_type=F32)
        return gate, up

    nxt = gate_up(0)
    yield 1.0
    for f in range(nf):
        gate, up = nxt
        if f + 1 < nf:
            nxt = gate_up(f + 1)
            yield 1.0
        act = (jax.nn.silu(gate) * up).astype(BF16)
        down = jnp.dot(act, wd_ref[f * FF_CHUNK:(f + 1) * FF_CHUNK, :], preferred_element_type=F32)
        if f == 0:
            acc_ref[...] = down
        else:
            acc_ref[...] += down
        yield 1.0
    result.append(x + 0.5 * acc_ref[...])


def _swiglu_half(x, g_ref, wg_ref, wu_ref, wd_ref, h_ref, acc_ref):
    result = []
    _drain(_swiglu_half_stages(x, g_ref, wg_ref, wu_ref, wd_ref, h_ref, acc_ref, result))
    return result[0]


def _ffn_proj_kernel(x_ref, g1_ref, wg_ref, wu_ref, wd_ref, gm_ref, wqkv_ref, wz_ref, wmla_ref, *rest,
                     stream_tiles):
    if not stream_tiles:
        x1_ref, qkv_ref, z_ref, mla_ref, h_ref, acc_ref = rest
        x1 = _swiglu_half(x_ref[...], g1_ref, wg_ref, wu_ref, wd_ref, h_ref, acc_ref)
    else:
        hist_ref, convw_ref, x1_ref, qkv_ref, z_ref, mla_ref, tail_ref, h_ref, acc_ref, conv_scr = rest
        i = pl.program_id(0)
        tm = x_ref.shape[0]

        @pl.when(i == 0)
        def _():
            conv_scr[...] = jnp.zeros_like(conv_scr)

        @pl.when((i + stream_tiles - 1) % stream_tiles == 0)
        def _():
            conv_scr[0, 0:GDN_HIST, :] = hist_ref[...]

        def feature_stages():
            for row0 in range(0, tm, CHUNK):
                qkv_ref[row0:row0 + CHUNK, :] = _gdn_features(conv_scr, convw_ref, 0, row0)
                yield 1.0
            conv_scr[0, 0:GDN_HIST, :] = conv_scr[0, tm:tm + GDN_HIST, :]

        result = []
        _drain((_swiglu_half_stages(x_ref[...], g1_ref, wg_ref, wu_ref, wd_ref, h_ref, acc_ref, result),
                2 * (D_FF // FF_CHUNK)), (feature_stages(), tm // CHUNK))
        x1 = result[0]
    x1_ref[...] = x1
    h_ref[...] = _rms(x1, gm_ref[...]).astype(BF16)
    qkv = jnp.dot(h_ref[...], wqkv_ref[...], preferred_element_type=F32)
    z_ref[...] = jnp.dot(h_ref[...], wz_ref[...], preferred_element_type=F32)
    mla_ref[...] = jnp.dot(h_ref[...], wmla_ref[...], preferred_element_type=F32)
    if stream_tiles:
        conv_scr[0, GDN_HIST:GDN_HIST + tm, :] = qkv
        tail_ref[0] = qkv[tm - GDN_HIST:, :]
    else:
        qkv_ref[...] = qkv


def _row_tile(n, want):
    t = min(want, n)
    assert n % t == 0, (n, t)
    return t


def ffn_proj(x, w, hist=None, stream_len=None, tm=512):
    n = x.shape[0]
    tm = _row_tile(n, tm)
    ntile = n // tm
    stream_tiles = 0
    if hist is not None:
        assert stream_len % tm == 0 and tm % CHUNK == 0
        stream_tiles = stream_len // tm
    tile = (lambda i: jnp.minimum(i, ntile - 1)) if stream_tiles else (lambda i: i)
    row = lambda width: pl.BlockSpec((tm, width), lambda i: (tile(i), 0))
    consts = (w['g1'], w['wg1'], w['wu1'], w['wd1'], w['gm'], w['w_qkv'], w['w_z'], w['w_mla'])
    out_specs = [row(D_MODEL), row(CONV_DIM), row(H_GDN * GDN_DV), row(MLA_IN)]
    out_shape = [jax.ShapeDtypeStruct((n, D_MODEL), F32), jax.ShapeDtypeStruct((n, CONV_DIM), F32),
                 jax.ShapeDtypeStruct((n, H_GDN * GDN_DV), F32), jax.ShapeDtypeStruct((n, MLA_IN), F32)]
    scratch = [pltpu.VMEM((tm, D_MODEL), BF16), pltpu.VMEM((tm, D_MODEL), F32)]
    if stream_tiles:
        consts += (hist, w['conv_w'])
        out_specs[1] = pl.BlockSpec((tm, CONV_DIM), lambda i: (jnp.maximum(i - 1, 0), 0))
        out_specs.append(pl.BlockSpec((1, GDN_HIST, CONV_DIM), lambda i: (tile(i), 0, 0)))
        out_shape.append(jax.ShapeDtypeStruct((ntile, GDN_HIST, CONV_DIM), F32))
        scratch.append(pltpu.VMEM((1, GDN_HIST + tm, CONV_DIM), F32))
    return pl.pallas_call(
        functools.partial(_ffn_proj_kernel, stream_tiles=stream_tiles),
        grid=(ntile + (1 if stream_tiles else 0),),
        in_specs=[row(D_MODEL)] + [_const_spec(c.shape) for c in consts],
        out_specs=out_specs,
        out_shape=out_shape,
        scratch_shapes=scratch,
        compiler_params=_params("arbitrary" if stream_tiles else "parallel"),
        name="ffn_proj",
    )(x, *consts)


def _out_ffn_kernel(x1_ref, gdn_ref, mla_ref, wog_ref, wom_ref, g2_ref, wg_ref, wu_ref, wd_ref, gf_ref,
                    y_ref, h_ref, acc_ref):
    x2 = (x1_ref[...] + jnp.dot(gdn_ref[...], wog_ref[...], preferred_element_type=F32)
          + jnp.dot(mla_ref[...], wom_ref[...], preferred_element_type=F32))
    x3 = _swiglu_half(x2, g2_ref, wg_ref, wu_ref, wd_ref, h_ref, acc_ref)
    y_ref[...] = _rms(x3, gf_ref[...])


def out_ffn(x1, gdn, mla, w, tm=512):
    n = x1.shape[0]
    tm = _row_tile(n, tm)
    row = lambda width: pl.BlockSpec((tm, width), lambda i: (i, 0))
    consts = (w['w_out_g'], w['w_out_m'], w['g2'], w['wg2'], w['wu2'], w['wd2'], w['gf'])
    return pl.pallas_call(
        _out_ffn_kernel,
        grid=(n // tm,),
        in_specs=[row(D_MODEL), row(H_GDN * GDN_DV), row(H_MLA * DV_MLA)] + [_const_spec(c.shape) for c in consts],
        out_specs=row(D_MODEL),
        out_shape=jax.ShapeDtypeStruct((n, D_MODEL), F32),
        scratch_shapes=[pltpu.VMEM((tm, D_MODEL), BF16), pltpu.VMEM((tm, D_MODEL), F32)],
        compiler_params=_params("parallel"),
        name="out_ffn",
    )(x1, gdn, mla, *consts)


def _cumsum_rows(x):
    n = x.shape[0]
    row = lax.broadcasted_iota(jnp.int32, x.shape, 0)
    shift = 1
    while shift < n:
        x = x + jnp.where(row >= shift, pltpu.roll(x, shift, 0), 0.0)
        shift *= 2
    return x


def _transpose_rows(x):
    length = x.shape[0]
    sq = jnp.concatenate([x, jnp.zeros((LANES - length, LANES), x.dtype)], axis=0)
    return sq.T[:, :length]


GDN_HIST = SUBLANES
GDN_PREP_COST = 4.0
GDN_PROMPT_ROWS = 2 * CHUNK
GDN_SAMPLE_GROUP = 8


def _gdn_features(conv_scr, convw_ref, i, row0):
    base = GDN_HIST - (CONV_W - 1) + row0
    parts = []
    for h in range(CONV_DIM // GDN_DK):
        cols = slice(h * GDN_DK, (h + 1) * GDN_DK)
        x = conv_scr[i, pl.ds(base, CHUNK), cols] * convw_ref[0:1, cols]
        for j in range(1, CONV_W):
            x = x + conv_scr[i, pl.ds(base + j, CHUNK), cols] * convw_ref[j:j + 1, cols]
        x = jax.nn.silu(x)
        if h < 2 * H_GDN:
            x = x * lax.rsqrt(jnp.sum(x * x, axis=-1, keepdims=True) + L2_EPS)
        parts.append(x * (GDN_DK ** -0.5) if h < H_GDN else x)
    return jnp.concatenate(parts, axis=1)


def _gdn_block_stages(first_block, rows, pad_front, features, ab_ref, z_ref, write_o, m_scr,
                      alog_ref, dtb_ref, gn_ref):
    bg = ab_ref.shape[0]
    length = CHUNK
    heads = H_GDN
    width = heads * length
    row = lax.broadcasted_iota(jnp.int32, (length, LANES), 0)
    lane = lax.broadcasted_iota(jnp.int32, (length, width), 1)
    lane_head = lane // length
    lane_col = lane % length
    row_p = lax.broadcasted_iota(jnp.int32, (length, width), 0)
    causal = lane_col <= row_p
    strict = lane_col < row_p

    def spread(cols):
        out = jnp.broadcast_to(cols[-1], (length, width))
        for h in range(heads - 2, -1, -1):
            out = jnp.where(lane_head == h, cols[h], out)
        return out

    def pick(mats):
        out = mats[-1]
        for h in range(heads - 2, -1, -1):
            out = jnp.where(lane_head == h, mats[h], out)
        return out

    def diag_blocks(p):
        return jnp.concatenate([jnp.where(lane_head == h, p, 0.0) for h in range(heads)], axis=0).astype(BF16)

    def diag_wide(mats):
        zero = jnp.zeros_like(mats[0])
        return jnp.concatenate(
            [jnp.concatenate([mats[h] if g == h else zero for g in range(heads)], axis=1) for h in range(heads)],
            axis=0).astype(BF16)

    def per_batch(i, row0):
        u = features(i, row0)
        ab = ab_ref[i, row0:row0 + length, :]
        g_all = -jnp.exp(alog_ref[...]) * jax.nn.softplus(ab + dtb_ref[...])
        beta_all = jax.nn.sigmoid(ab)
        if pad_front > row0:
            valid = jnp.logical_or(jnp.logical_not(first_block), row >= pad_front - row0)
            g_all = jnp.where(valid, g_all, 0.0)
            beta_all = jnp.where(valid, beta_all, 0.0)
        gc_all = _cumsum_rows(g_all)
        gc_rows = jnp.concatenate([gc_all, jnp.zeros((LANES - length, LANES), F32)], axis=0).T
        gc_rows = gc_rows[AB_LANE:AB_LANE + SUBLANES, :]
        gc_rows_hi = pltpu.roll(gc_rows, length, 1)
        gc_row = jnp.concatenate([gc_rows[h:h + 1] + gc_rows_hi[h + 1:h + 2] for h in range(0, heads, 2)], axis=1)

        hd = []
        for h in range(heads):
            q = u[:, h * GDN_DK:(h + 1) * GDN_DK]
            k = u[:, (heads + h) * GDN_DK:(heads + h + 1) * GDN_DK]
            v = u[:, 2 * heads * GDN_DK + h * GDN_DV:2 * heads * GDN_DK + (h + 1) * GDN_DV]
            gc = gc_all[:, AB_LANE + h:AB_LANE + h + 1]
            beta = beta_all[:, AB_LANE + heads + h:AB_LANE + heads + h + 1]
            kq = jnp.concatenate([k, q], axis=0).astype(BF16)
            hd.append(dict(h=h, k=k, v=v, gc=gc, beta=beta, kq=kq))
        gc_col = spread([c['gc'] for c in hd])
        decay = jnp.where(causal, jnp.exp(jnp.where(causal, gc_col - gc_row, 0.0)), 0.0)
        k_all = jnp.concatenate([c['k'] for c in hd], axis=0).astype(BF16)
        return dict(i=i, row0=row0, heads=hd, decay=decay, beta=spread([c['beta'] for c in hd]), k_all=k_all)

    per_chunk = []
    for row0 in range(0, rows, length):
        per_chunk.append([])
        for i in range(bg):
            per_chunk[-1].append(per_batch(i, row0))
            yield GDN_PREP_COST
    groups = [g for chunk in per_chunk for g in chunk]
    for g in groups:
        scores = [lax.dot_general(c['kq'], g['k_all'], NT_DIMS, preferred_element_type=F32) for c in g['heads']]
        g['kk'] = pick([s[:length] for s in scores])
        g['qk'] = pick([s[length:] for s in scores])
    yield 1.0
    for g in groups:
        g['a'] = jnp.where(strict, g['beta'] * g['kk'] * g['decay'], 0.0)
        g['qk_decay'] = g['qk'] * g['decay']
        g['y'] = -g['a']
    for g in groups:
        g['pw'] = jnp.dot(g['a'].astype(BF16), diag_blocks(g['a']), preferred_element_type=F32)
    yield 1.0
    span = 2
    while span < length:
        span *= 2
        for g in groups:
            blocks = diag_blocks(g['pw'])
            if span < length:
                prod = jnp.dot(jnp.concatenate([g['y'], g['pw']], axis=0).astype(BF16), blocks,
                               preferred_element_type=F32)
                g['y'] = g['y'] + g['pw'] + prod[:length]
                g['pw'] = prod[length:]
            else:
                g['y'] = g['y'] + g['pw'] + jnp.dot(g['y'].astype(BF16), blocks, preferred_element_type=F32)
        yield 1.0

    def emit_output(chunk):
        for g in chunk:
            prod = jnp.dot(g['qk_decay'].astype(BF16), diag_wide([c['uu'] for c in g['heads']]),
                           preferred_element_type=F32)
            rws = slice(g['row0'], g['row0'] + length)
            for c in g['heads']:
                cols = slice(c['h'] * GDN_DV, (c['h'] + 1) * GDN_DV)
                o = c['eg'] * c['kqm'][length:] + prod[:, cols]
                n = _rms(o, gn_ref[...])
                write_o(g['i'], rws, cols, (n * jax.nn.silu(z_ref[g['i'], rws, cols])).astype(BF16))

    pending = None
    for chunk in per_chunk:
        for g in chunk:
            for c in g['heads']:
                c['m0'] = m_scr[g['i'], c['h']]
                c['kqm'] = jnp.dot(c['kq'], c['m0'].astype(BF16), preferred_element_type=F32)
        yield 1.0
        if pending is not None:
            emit_output(pending)
            yield 1.0
        for g in chunk:
            for c in g['heads']:
                c['eg'] = jnp.exp(c['gc'])
                c['rhs'] = c['beta'] * (c['v'] - c['eg'] * c['kqm'][:length])
            prod = jnp.dot(g['y'].astype(BF16), diag_wide([c['rhs'] for c in g['heads']]),
                           preferred_element_type=F32)
            for c in g['heads']:
                c['uu'] = c['rhs'] + prod[:, c['h'] * GDN_DV:(c['h'] + 1) * GDN_DV]
        yield 1.0
        for g in chunk:
            for c in g['heads']:
                g_last = c['gc'][length - 1:length, :]
                k_dec = c['k'] * jnp.exp(g_last - c['gc'])
                m_scr[g['i'], c['h']] = jnp.exp(g_last) * c['m0'] + _bdot(_transpose_rows(k_dec), c['uu'])
        yield 1.0
        pending = chunk
    emit_output(pending)
    yield 1.0


def _gdn_stage_block(qkv_ref, conv_scr):
    bg, rows, _ = qkv_ref.shape
    for i in range(bg):
        conv_scr[i, GDN_HIST:GDN_HIST + rows, :] = qkv_ref[i]


def _gdn_keep_history(conv_scr, rows):
    for i in range(conv_scr.shape[0]):
        conv_scr[i, 0:GDN_HIST, :] = conv_scr[i, rows:rows + GDN_HIST, :]


def _gdn_kernel(qkv_ref, z_ref, ab_ref, m0_ref, conv0_ref, convw_ref, alog_ref, dtb_ref, gn_ref,
                o_ref, mout_ref, conv_scr, m_scr, *, pad_front, prepared):
    c = pl.program_id(1)
    rows = qkv_ref.shape[1]

    @pl.when(c == 0)
    def _():
        m_scr[...] = m0_ref[...]
        conv_scr[:, 0:GDN_HIST, :] = conv0_ref[...]

    def write_o(i, rws, cols, val):
        o_ref[i, rws, cols] = val

    if prepared:
        features = lambda i, row0: qkv_ref[i, row0:row0 + CHUNK, :]
    else:
        _gdn_stage_block(qkv_ref, conv_scr)
        features = functools.partial(_gdn_features, conv_scr, convw_ref)
    _drain(_gdn_block_stages(c == 0, rows, pad_front, features, ab_ref, z_ref, write_o, m_scr,
                             alog_ref, dtb_ref, gn_ref))
    if not prepared:
        _gdn_keep_history(conv_scr, rows)

    @pl.when(c == pl.num_programs(1) - 1)
    def _():
        mout_ref[...] = m_scr[...]


def _gdn_in_specs(bg, rows, blk):
    nd = len(blk(0, 0))
    const = lambda shape: pl.BlockSpec(shape, lambda *_: (0,) * len(shape))
    return [
        pl.BlockSpec((bg, rows, CONV_DIM), blk),
        pl.BlockSpec((bg, rows, H_GDN * GDN_DV), blk),
        pl.BlockSpec((bg, rows, LANES), lambda *a: blk(*a)[:nd - 1] + (MLA_IN // LANES - 1,)),
        pl.BlockSpec((bg, H_GDN, GDN_DK, GDN_DV), lambda g, c: (g, 0, 0, 0)),
        pl.BlockSpec((bg, GDN_HIST, CONV_DIM), lambda g, c: (g, 0, 0)),
        const((CONV_W, CONV_DIM)),
        const((1, LANES)),
        const((1, LANES)),
        const((1, GDN_DV)),
    ]


def gdn(qkv, z, mla_in, m0, conv0, w, *, pad_front=0, bg=4, rows=CHUNK, prepared=False):
    b, t, _ = qkv.shape
    bg = min(bg, b)
    assert b % bg == 0 and t % rows == 0 and rows % CHUNK == 0
    blk = lambda g, c: (g, c, 0)
    return pl.pallas_call(
        functools.partial(_gdn_kernel, pad_front=pad_front, prepared=prepared),
        grid=(b // bg, t // rows),
        in_specs=_gdn_in_specs(bg, rows, blk),
        out_specs=[
            pl.BlockSpec((bg, rows, H_GDN * GDN_DV), blk),
            pl.BlockSpec((bg, H_GDN, GDN_DK, GDN_DV), lambda g, c: (g, 0, 0, 0)),
        ],
        out_shape=[jax.ShapeDtypeStruct((b, t, H_GDN * GDN_DV), BF16),
                   jax.ShapeDtypeStruct((b, H_GDN, GDN_DK, GDN_DV), F32)],
        scratch_shapes=[pltpu.VMEM((bg, GDN_HIST + (0 if prepared else rows), CONV_DIM), F32),
                        pltpu.VMEM((bg, H_GDN, GDN_DK, GDN_DV), F32)],
        compiler_params=_params("parallel", "arbitrary"),
        name="gdn",
    )(qkv, z, mla_in, m0, conv0, w['conv_w'], w['alog'], w['dtb'], w['gn'])


def _rope(t, cos, sin, duplicated=False):
    half = DR // 2
    if duplicated:
        swapped = pltpu.roll(t, half, 1)
    else:
        lane = lax.broadcasted_iota(jnp.int32, t.shape, 1)
        swapped = jnp.where(lane < half, pltpu.roll(t, LANES - half, 1), pltpu.roll(t, half, 1))
    return t * cos + swapped * sin


def _mla_prep_kernel(x_ref, cos_ref, sin_ref, gq_ref, gkv_ref, wuq_ref, wuk_ref, wuvt_ref, *out_refs, expand):
    q_ref, c_ref, kr_ref = out_refs[:3]
    x = x_ref[...]
    cos = cos_ref[...]
    sin = sin_ref[...]
    q = _bdot(_rms(x[:, :Q_LORA], gq_ref[...]), wuq_ref[...])
    for h in range(H_MLA):
        lo = h * QK_PAD
        q_ref[:, lo:lo + DN] = (q[:, lo:lo + DN] * Q_SCALE).astype(BF16)
        q_ref[:, lo + DN:lo + QK_PAD] = (_rope(q[:, lo + DN:lo + QK_PAD], cos, sin, True) * Q_SCALE).astype(BF16)
    c = _rms(x[:, Q_LORA:Q_LORA + KV_LORA], gkv_ref[...])
    c_ref[...] = c
    kr = _rope(x[:, Q_LORA + KV_LORA:], cos, sin)
    if not expand:
        kr_ref[...] = kr
    else:
        kr_ref[...] = (kr.T if kr.shape[0] % LANES == 0 else _transpose_rows(kr))[:DR]
        k_ref, vt_ref = out_refs[3:]
        c16 = c.astype(BF16)
        k_nope = jnp.dot(c16, wuk_ref[...], preferred_element_type=F32)
        for h in range(H_MLA):
            k_ref[:, h * QK_PAD:h * QK_PAD + DN] = k_nope[:, h * DN:(h + 1) * DN].astype(BF16)
            k_ref[:, h * QK_PAD + DN:(h + 1) * QK_PAD] = kr.astype(BF16)
        vt_ref[...] = lax.dot_general(wuvt_ref[...], c16, NT_DIMS, preferred_element_type=F32).astype(BF16)


MLA_PREP_TILE = 1024


def mla_prep(mla_in, cos, sin, w, *, expand, tm=MLA_PREP_TILE):
    n = mla_in.shape[0]
    tm = _row_tile(min(n, cos.shape[0]), tm)
    nrep = cos.shape[0] // tm
    row = lambda width: pl.BlockSpec((tm, width), lambda i: (i, 0))
    tab = pl.BlockSpec((tm, LANES), lambda i: (i % nrep, 0))
    consts = (w['gq'], w['gkv'], w['w_uq'], w['w_uk'], w['w_uv_t'])
    out_specs = [row(H_MLA * QK_PAD), row(KV_LORA), row(LANES)]
    out_shape = [jax.ShapeDtypeStruct((n, H_MLA * QK_PAD), BF16), jax.ShapeDtypeStruct((n, KV_LORA), F32),
                 jax.ShapeDtypeStruct((n, LANES), F32)]
    if expand:
        out_specs[2] = pl.BlockSpec((DR, tm), lambda i: (0, i))
        out_shape[2] = jax.ShapeDtypeStruct((DR, n), F32)
        out_specs += [row(H_MLA * QK_PAD), pl.BlockSpec((H_MLA * DV_MLA, tm), lambda i: (0, i))]
        out_shape += [jax.ShapeDtypeStruct((n, H_MLA * QK_PAD), BF16), jax.ShapeDtypeStruct((H_MLA * DV_MLA, n), BF16)]
    return pl.pallas_call(
        functools.partial(_mla_prep_kernel, expand=expand),
        grid=(n // tm,),
        in_specs=[row(MLA_IN), tab, tab] + [_const_spec(c.shape) for c in consts],
        out_specs=out_specs,
        out_shape=out_shape,
        compiler_params=_params("parallel"),
        name="mla_prep_kv" if expand else "mla_prep",
    )(mla_in, cos, sin, *consts)


def _mla_prompt_kernel(q_ref, k_ref, vt_ref, km_ref, vmt_ref, o_ref, s_scr, m_scr, l_scr, acc_scr):
    qi = pl.program_id(1)
    tq = q_ref.shape[0]
    heads = range(H_MLA)

    def scores(off, h):
        return lax.dot_general(k_ref[pl.ds(off, tq), h * QK_PAD:(h + 1) * QK_PAD],
                               q_ref[:, h * QK_PAD:(h + 1) * QK_PAD], NT_DIMS, preferred_element_type=F32)

    def consume(h, off, masked):
        s_t = s_scr[h]
        if masked:
            key_chunk = lax.broadcasted_iota(jnp.int32, (tq, tq), 0) // CHUNK
            qry_chunk = lax.broadcasted_iota(jnp.int32, (tq, tq), 1) // CHUNK
            s_t = jnp.where(key_chunk <= qry_chunk, s_t, -jnp.inf)
        m = m_scr[h]
        m_new = jnp.maximum(m, jnp.max(s_t, axis=0, keepdims=True))
        alpha = jnp.exp2(m - m_new)
        p_t = jnp.exp2(s_t - m_new)
        m_scr[h] = m_new
        l_scr[h] = alpha * l_scr[h] + jnp.sum(p_t, axis=0, keepdims=True)
        acc_scr[h] = alpha * acc_scr[h] + jnp.dot(vt_ref[h * DV_MLA:(h + 1) * DV_MLA, pl.ds(off, tq)],
                                                  p_t.astype(BF16), preferred_element_type=F32)

    meta_s = [lax.dot_general(km_ref[:, h * QK_PAD:(h + 1) * QK_PAD], q_ref[:, h * QK_PAD:(h + 1) * QK_PAD],
                              NT_DIMS, preferred_element_type=F32) for h in heads]
    for h in heads:
        s_scr[h] = scores(0, h)
    meta_p = []
    for h in heads:
        m = jnp.max(meta_s[h], axis=0, keepdims=True)
        p_t = jnp.exp2(meta_s[h] - m)
        m_scr[h] = m
        l_scr[h] = jnp.sum(p_t, axis=0, keepdims=True)
        meta_p.append(p_t.astype(BF16))
    for h in heads:
        acc_scr[h] = jnp.dot(vmt_ref[h * DV_MLA:(h + 1) * DV_MLA, :], meta_p[h], preferred_element_type=F32)

    def full_tile(t):
        off = pl.multiple_of(t * tq, tq)
        nxt = pl.multiple_of((t + 1) * tq, tq)
        for h in heads:
            s_next = scores(nxt, h)
            consume(h, off, False)
            s_scr[h] = s_next

    def tile_pair(t2, carry):
        full_tile(2 * t2)
        full_tile(2 * t2 + 1)
        return carry

    lax.fori_loop(0, qi // 2, tile_pair, 0)

    @pl.when(qi % 2 == 1)
    def _():
        full_tile(qi - 1)

    for h in heads:
        consume(h, pl.multiple_of(qi * tq, tq), True)
    for h in heads:
        o_ref[:, h * DV_MLA:(h + 1) * DV_MLA] = (acc_scr[h] / l_scr[h]).T.astype(o_ref.dtype)


def mla_prompt(q, k, v_t, k_meta, v_meta_t, tq=512):
    b, s, _ = q.shape
    tq = _row_tile(s, tq)
    assert tq % LANES == 0
    whole = lambda shape: pl.BlockSpec(shape, lambda bi, i: (0, 0))
    return pl.pallas_call(
        _mla_prompt_kernel,
        grid=(b, s // tq),
        in_specs=[
            pl.BlockSpec((None, tq, H_MLA * QK_PAD), lambda bi, i: (bi, i, 0)),
            pl.BlockSpec((None, s, H_MLA * QK_PAD), lambda bi, i: (bi, 0, 0), pipeline_mode=pl.Buffered(1)),
            pl.BlockSpec((H_MLA * DV_MLA, s), lambda bi, i: (0, bi)),
            whole(k_meta.shape),
            whole(v_meta_t.shape),
        ],
        out_specs=pl.BlockSpec((None, tq, H_MLA * DV_MLA), lambda bi, i: (bi, i, 0)),
        out_shape=jax.ShapeDtypeStruct((b, s, H_MLA * DV_MLA), BF16),
        scratch_shapes=[pltpu.VMEM((H_MLA, tq, tq), F32), pltpu.VMEM((H_MLA, 1, tq), F32),
                        pltpu.VMEM((H_MLA, 1, tq), F32), pltpu.VMEM((H_MLA, DV_MLA, tq), F32)],
        compiler_params=_params("parallel", "arbitrary"),
        name="mla_prompt",
    )(q, k, v_t, k_meta, v_meta_t)


def _mla_sample_kernel(q_ref, c_ref, kr_ref, cc_ref, ckr_ref, wukv_ref, o_ref):
    nreq, t, _ = q_ref.shape
    w = wukv_ref[...]
    reqs = []
    for r in range(nreq):
        q = q_ref[r]
        qa = jnp.concatenate(
            [lax.dot_general(q[:, h * QK_PAD:h * QK_PAD + DN], w[:, h * (DN + DV_MLA):h * (DN + DV_MLA) + DN],
                             NT_DIMS, preferred_element_type=F32) for h in range(H_MLA)], axis=0).astype(BF16)
        qr = jnp.concatenate([q[:, h * QK_PAD + DN:(h + 1) * QK_PAD] for h in range(H_MLA)], axis=0)
        reqs.append(dict(qa=qa, qr=qr, cache_c=cc_ref[r].astype(BF16), own_c=c_ref[r].astype(BF16)))
    for r, rq in enumerate(reqs):
        rq['s_cache'] = (lax.dot_general(rq['qa'], rq['cache_c'], NT_DIMS, preferred_element_type=F32)
                         + jnp.dot(rq['qr'][:, :DR], ckr_ref[r].astype(BF16), preferred_element_type=F32))
        rq['s_own'] = (lax.dot_general(rq['qa'], rq['own_c'], NT_DIMS, preferred_element_type=F32)
                       + lax.dot_general(rq['qr'], kr_ref[r].astype(BF16), NT_DIMS, preferred_element_type=F32))
    for rq in reqs:
        m = jnp.maximum(jnp.max(rq['s_cache'], axis=-1, keepdims=True), jnp.max(rq['s_own'], axis=-1, keepdims=True))
        p_cache = jnp.exp2(rq['s_cache'] - m)
        p_own = jnp.exp2(rq['s_own'] - m)
        rq['l'] = jnp.sum(p_cache, axis=-1, keepdims=True) + jnp.sum(p_own, axis=-1, keepdims=True)
        rq['p_cache'], rq['p_own'] = p_cache.astype(BF16), p_own.astype(BF16)
    for rq in reqs:
        rq['pc'] = (jnp.dot(rq['p_cache'], rq['cache_c'], preferred_element_type=F32)
                    + jnp.dot(rq['p_own'], rq['own_c'], preferred_element_type=F32)) / rq['l']
    for r, rq in enumerate(reqs):
        for h in range(H_MLA):
            lo = h * (DN + DV_MLA) + DN
            o_ref[r, :, h * DV_MLA:(h + 1) * DV_MLA] = _bdot(rq['pc'][h * t:(h + 1) * t],
                                                             w[:, lo:lo + DV_MLA]).astype(o_ref.dtype)


def mla_sample(q, c, kr, cache_c, cache_kr, w_ukv, group=4):
    b, t, _ = q.shape
    p = cache_c.shape[1]
    group = min(group, b)
    assert b % group == 0
    per_b = lambda rows, width: pl.BlockSpec((group, rows, width), lambda bi: (bi, 0, 0))
    return pl.pallas_call(
        _mla_sample_kernel,
        grid=(b // group,),
        in_specs=[per_b(t, H_MLA * QK_PAD), per_b(t, KV_LORA), per_b(t, LANES), per_b(p, KV_LORA), per_b(DR, p),
                  _const_spec(w_ukv.shape)],
        out_specs=per_b(t, H_MLA * DV_MLA),
        out_shape=jax.ShapeDtypeStruct((b, t, H_MLA * DV_MLA), BF16),
        compiler_params=_params("parallel"),
        name="mla_sample",
    )(q, c, kr, cache_c, cache_kr, w_ukv)


def _rope_tables(first, count, repeat=1):
    inv = ROPE_BASE ** (-np.arange(0, DR, 2, dtype=np.float64) / DR)
    ang = (first + np.arange(count, dtype=np.float64))[:, None] * inv[None, :]
    cos, sin = np.cos(ang), np.sin(ang)
    pad = np.zeros((count, LANES - DR))
    table = lambda parts: jnp.asarray(np.tile(np.concatenate(parts, axis=1), (repeat, 1)), F32)
    return table([cos, cos, pad]), table([-sin, sin, pad])


def _prepare_weights(ffn1_norm, ffn1_wg, ffn1_wu, ffn1_wd, mix_norm, w_in, conv_w, a_log, dt_bias, gdn_norm,
                     q_norm, kv_norm, w_uq, w_ukv, w_out, ffn2_norm, ffn2_wg, ffn2_wu, ffn2_wd, final_norm):
    w = {}
    w['g1'], w['gm'], w['g2'] = ffn1_norm[0][None], mix_norm[0][None], ffn2_norm[0][None]
    w['gf'] = final_norm[None]
    assert D_FF % FF_CHUNK == 0
    w['wg1'], w['wu1'], w['wd1'] = (m[0].astype(BF16) for m in (ffn1_wg, ffn1_wu, ffn1_wd))
    w['wg2'], w['wu2'], w['wd2'] = (m[0].astype(BF16) for m in (ffn2_wg, ffn2_wu, ffn2_wd))
    o_z = CONV_DIM
    o_a = o_z + H_GDN * GDN_DV
    o_b = o_a + H_GDN
    o_cq = o_b + H_GDN
    o_kr = o_cq + Q_LORA + KV_LORA
    wi = w_in[0]
    w['w_qkv'] = wi[:, :o_z].astype(BF16)
    w['w_z'] = wi[:, o_z:o_a].astype(BF16)
    tail_pad = jnp.zeros((D_MODEL, LANES - DR - 2 * H_GDN), wi.dtype)
    w['w_mla'] = jnp.concatenate([wi[:, o_cq:o_kr], wi[:, o_kr:o_kr + DR], wi[:, o_a:o_cq], tail_pad], axis=1).astype(BF16)
    w['conv_w'] = conv_w[0]
    lane_vec = lambda v: jnp.zeros((1, LANES), F32).at[0, AB_LANE:AB_LANE + H_GDN].set(v.astype(F32))
    w['alog'], w['dtb'] = lane_vec(a_log[0]), lane_vec(dt_bias[0])
    w['gn'] = gdn_norm[0][None]
    w['gq'], w['gkv'] = q_norm[0][None], kv_norm[0][None]
    uq = w_uq[0].reshape(Q_LORA, H_MLA, DN + DR)
    assert QK_PAD - DN - DR == DR
    uq = jnp.concatenate([uq, uq[:, :, DN:]], axis=-1)
    w['w_uq'] = uq.reshape(Q_LORA, H_MLA * QK_PAD).astype(BF16)
    w['w_ukv'] = w_ukv[0].astype(BF16)
    ukv = w['w_ukv'].reshape(KV_LORA, H_MLA, DN + DV_MLA)
    w['w_uk'] = ukv[:, :, :DN].reshape(KV_LORA, H_MLA * DN)
    w['w_uv_t'] = ukv[:, :, DN:].reshape(KV_LORA, H_MLA * DV_MLA).T
    w['w_out_g'] = w_out[0][:H_GDN * GDN_DV].astype(BF16)
    w['w_out_m'] = w_out[0][H_GDN * GDN_DV:].astype(BF16)
    return w


def kernel(x_prompt, x_sample, cache_mla_ckv, cache_mla_krope, state_gdn, state_conv, meta, ffn1_norm, ffn1_wg,
           ffn1_wu, ffn1_wd, mix_norm, w_in, conv_w, a_log, dt_bias, gdn_norm, q_norm, kv_norm, w_uq, w_ukv, w_out,
           ffn2_norm, ffn2_wg, ffn2_wu, ffn2_wd, final_norm):
    assert ffn1_wg.shape[0] == 1, "one layer: the meta rows are not carried past the mixer"
    bsz, s_len, _ = x_prompt.shape
    dbs, d_seq, _ = x_sample.shape
    past = cache_mla_ckv.shape[2]
    w = _prepare_weights(ffn1_norm, ffn1_wg, ffn1_wu, ffn1_wd, mix_norm, w_in, conv_w, a_log, dt_bias, gdn_norm,
                         q_norm, kv_norm, w_uq, w_ukv, w_out, ffn2_norm, ffn2_wg, ffn2_wu, ffn2_wd, final_norm)

    front = CHUNK - N_META
    _, qkv_m, _, mla_m = ffn_proj(meta.astype(F32), w)
    meta_qkv = jnp.pad(qkv_m, ((front, 0), (0, 0)))[None]
    x1_p, feat_p, z_p, mla_p, tail_p = ffn_proj(x_prompt.reshape(bsz * s_len, D_MODEL), w,
                                                 hist=meta_qkv[0, CHUNK - GDN_HIST:], stream_len=s_len)
    x1_s, qkv_s, z_s, mla_s = ffn_proj(x_sample.reshape(dbs * d_seq, D_MODEL), w)

    _, m_meta = gdn(meta_qkv, jnp.zeros((1, CHUNK, H_GDN * GDN_DV), F32), jnp.pad(mla_m, ((front, 0), (0, 0)))[None],
                    jnp.zeros((1, H_GDN, GDN_DK, GDN_DV), F32), jnp.zeros((1, GDN_HIST, CONV_DIM), F32), w,
                    pad_front=front)
    m0_p = jnp.broadcast_to(m_meta, (bsz,) + m_meta.shape[1:])
    gdn_p, m_p = gdn(feat_p.reshape(bsz, s_len, CONV_DIM), z_p.reshape(bsz, s_len, -1),
                     mla_p.reshape(bsz, s_len, MLA_IN), m0_p, jnp.zeros((bsz, GDN_HIST, CONV_DIM), F32), w,
                     rows=GDN_PROMPT_ROWS, prepared=True)
    conv_p = tail_p.reshape(bsz, -1, GDN_HIST, CONV_DIM)[:, -1, GDN_HIST - (CONV_W - 1):]
    qkv_s3 = qkv_s.reshape(dbs, d_seq, CONV_DIM)
    conv0_s = jnp.pad(state_conv[0].astype(F32), ((0, 0), (GDN_HIST - (CONV_W - 1), 0), (0, 0)))
    gdn_s, m_s = gdn(qkv_s3, z_s.reshape(dbs, d_seq, -1), mla_s.reshape(dbs, d_seq, MLA_IN),
                     state_gdn[0].astype(F32), conv0_s, w, bg=GDN_SAMPLE_GROUP)

    cos_m, sin_m = _rope_tables(0, N_META)
    cos_p, sin_p = _rope_tables(N_META, s_len)
    cos_s, sin_s = _rope_tables(past, d_seq, repeat=min(dbs, MLA_PREP_TILE // d_seq))
    q_p, c_p, kr_p, k_p, vt_p = mla_prep(mla_p, cos_p, sin_p, w, expand=True)
    _, c_m, kr_m, k_m, vt_m = mla_prep(mla_m, cos_m, sin_m, w, expand=True)
    q_s, c_s, kr_s = mla_prep(mla_s, cos_s, sin_s, w, expand=False)
    mla_o_p = mla_prompt(q_p.reshape(bsz, s_len, -1), k_p.reshape(bsz, s_len, -1), vt_p, k_m, vt_m)
    mla_o_s = mla_sample(q_s.reshape(dbs, d_seq, -1), c_s.reshape(dbs, d_seq, -1), kr_s.reshape(dbs, d_seq, -1),
                         cache_mla_ckv[0].astype(F32), jnp.swapaxes(cache_mla_krope[0].astype(F32), 1, 2), w['w_ukv'])

    y_p = out_ffn(x1_p, gdn_p.reshape(bsz * s_len, -1), mla_o_p.reshape(bsz * s_len, -1), w)
    y_s = out_ffn(x1_s, gdn_s.reshape(dbs * d_seq, -1), mla_o_s.reshape(dbs * d_seq, -1), w)

    with_meta = lambda m_rows, rows: jnp.concatenate(
        [jnp.broadcast_to(m_rows[None], (bsz,) + m_rows.shape), rows.reshape(bsz, s_len, -1)], axis=1)
    krope_t = jnp.concatenate([jnp.broadcast_to(kr_m[None], (bsz, DR, N_META)),
                               kr_p.reshape(DR, bsz, s_len).transpose(1, 0, 2)], axis=2)
    return (y_p.reshape(bsz, s_len, D_MODEL), y_s.reshape(dbs, d_seq, D_MODEL),
            with_meta(c_m, c_p)[None], jnp.swapaxes(krope_t, 1, 2)[None],
            m_p[None], conv_p[None],
            c_s.reshape(dbs, d_seq, KV_LORA)[None], kr_s[:, :DR].reshape(dbs, d_seq, DR)[None],
            m_s[None], qkv_s3[:, d_seq - (CONV_W - 1):][None])
```

```python
import functools

import jax
import jax.numpy as jnp
import numpy as np
from jax import lax
from jax.experimental import pallas as pl
from jax.experimental.pallas import tpu as pltpu

F32 = jnp.float32
BF16 = jnp.bfloat16

D_MODEL = 1024
CHUNK = 64
N_META = 16
H_GDN = 4
GDN_DK = 128
GDN_DV = 128
CONV_W = 4
CONV_DIM = H_GDN * (2 * GDN_DK + GDN_DV)
H_MLA = 4
Q_LORA = 384
KV_LORA = 256
DN = 128
DR = 64
DV_MLA = 128
ROPE_BASE = 10000.0
SM_SCALE = (DN + DR) ** -0.5
LOG2_E = 1.4426950408889634
Q_SCALE = SM_SCALE * LOG2_E
D_FF = 2816
EPS = 1e-6
L2_EPS = 1e-6

LANES = 128
SUBLANES = 8
FF_CHUNK = 256
MLA_IN = Q_LORA + KV_LORA + LANES
QK_PAD = 2 * LANES
AB_LANE = DR
VMEM_LIMIT = 56 * 1024 * 1024

NT_DIMS = (((1,), (1,)), ((), ()))


def _rms(x, g):
    return x * lax.rsqrt(jnp.mean(x * x, axis=-1, keepdims=True) + EPS) * g


def _bdot(a, b):
    return jnp.dot(a.astype(BF16), b.astype(BF16), preferred_element_type=F32)


def _bdot_nt(a, b):
    return lax.dot_general(a.astype(BF16), b.astype(BF16), NT_DIMS, preferred_element_type=F32)


def _const_spec(shape):
    nd = len(shape)
    return pl.BlockSpec(shape, lambda *_: (0,) * nd, pipeline_mode=pl.Buffered(1))


def _params(*sem):
    return pltpu.CompilerParams(dimension_semantics=sem, vmem_limit_bytes=VMEM_LIMIT)


def _drain(*stage_gens):
    live = [[g, 0.0, float(t)] for g, t in (sg if isinstance(sg, tuple) else (sg, 1.0) for sg in stage_gens)]
    while live:
        entry = min(live, key=lambda e: e[1] / e[2])
        try:
            entry[1] += next(entry[0]) or 0.0
        except StopIteration:
            live.remove(entry)


def _swiglu_half_stages(x, g_ref, wg_ref, wu_ref, wd_ref, h_ref, acc_ref, result):
    h_ref[...] = _rms(x, g_ref[...]).astype(BF16)
    nf = wg_ref.shape[1] // FF_CHUNK

    def gate_up(f):
        cols = slice(f * FF_CHUNK, (f + 1) * FF_CHUNK)
        gate = jnp.dot(h_ref[...], wg_ref[:, cols], preferred_element_type=F32)
        up = jnp.dot(h_ref[...], wu_ref[:, cols], preferred_element_type=F32)
        return gate, up

    nxt = gate_up(0)
    yield 1.0
    for f in range(nf):
        gate, up = nxt
        if f + 1 < nf:
            nxt = gate_up(f + 1)
            yield 1.0
        act = (jax.nn.silu(gate) * up).astype(BF16)
        down = jnp.dot(act, wd_ref[f * FF_CHUNK:(f + 1) * FF_CHUNK, :], preferred_element_type=F32)
        if f == 0:
            acc_ref[...] = down
        else:
            acc_ref[...] += down
        yield 1.0
    result.append(x + 0.5 * acc_ref[...])


def _swiglu_half(x, g_ref, wg_ref, wu_ref, wd_ref, h_ref, acc_ref):
    result = []
    _drain(_swiglu_half_stages(x, g_ref, wg_ref, wu_ref, wd_ref, h_ref, acc_ref, result))
    return result[0]


def _ffn_proj_kernel(x_ref, g1_ref, wg_ref, wu_ref, wd_ref, gm_ref, wqkv_ref, wz_ref, wmla_ref, *rest,
                     stream_tiles):
    if not stream_tiles:
        x1_ref, qkv_ref, z_ref, mla_ref, h_ref, acc_ref = rest
        x1 = _swiglu_half(x_ref[...], g1_ref, wg_ref, wu_ref, wd_ref, h_ref, acc_ref)
    else:
        hist_ref, convw_ref, x1_ref, qkv_ref, z_ref, mla_ref, tail_ref, h_ref, acc_ref, conv_scr = rest
        i = pl.program_id(0)
        tm = x_ref.shape[0]

        @pl.when(i == 0)
        def _():
            conv_scr[...] = jnp.zeros_like(conv_scr)

        @pl.when((i + stream_tiles - 1) % stream_tiles == 0)
        def _():
            conv_scr[0, 0:GDN_HIST, :] = hist_ref[...]

        def feature_stages():
            for row0 in range(0, tm, CHUNK):
                qkv_ref[row0:row0 + CHUNK, :] = _gdn_features(conv_scr, convw_ref, 0, row0)
                yield 1.0
            conv_scr[0, 0:GDN_HIST, :] = conv_scr[0, tm:tm + GDN_HIST, :]

        result = []
        _drain((_swiglu_half_stages(x_ref[...], g1_ref, wg_ref, wu_ref, wd_ref, h_ref, acc_ref, result),
                2 * (D_FF // FF_CHUNK)), (feature_stages(), tm // CHUNK))
        x1 = result[0]
    x1_ref[...] = x1
    h_ref[...] = _rms(x1, gm_ref[...]).astype(BF16)
    qkv = jnp.dot(h_ref[...], wqkv_ref[...], preferred_element_type=F32)
    z_ref[...] = jnp.dot(h_ref[...], wz_ref[...], preferred_element_type=F32)
    mla_ref[...] = jnp.dot(h_ref[...], wmla_ref[...], preferred_element_type=F32)
    if stream_tiles:
        conv_scr[0, GDN_HIST:GDN_HIST + tm, :] = qkv
        tail_ref[0] = qkv[tm - GDN_HIST:, :]
    else:
        qkv_ref[...] = qkv


def _row_tile(n, want):
    t = min(want, n)
    assert n % t == 0, (n, t)
    return t


def ffn_proj(x, w, hist=None, stream_len=None, tm=512):
    n = x.shape[0]
    tm = _row_tile(n, tm)
    ntile = n // tm
    stream_tiles = 0
    if hist is not None:
        assert stream_len % tm == 0 and tm % CHUNK == 0
        stream_tiles = stream_len // tm
    tile = (lambda i: jnp.minimum(i, ntile - 1)) if stream_tiles else (lambda i: i)
    row = lambda width: pl.BlockSpec((tm, width), lambda i: (tile(i), 0))
    consts = (w['g1'], w['wg1'], w['wu1'], w['wd1'], w['gm'], w['w_qkv'], w['w_z'], w['w_mla'])
    out_specs = [row(D_MODEL), row(CONV_DIM), row(H_GDN * GDN_DV), row(MLA_IN)]
    out_shape = [jax.ShapeDtypeStruct((n, D_MODEL), F32), jax.ShapeDtypeStruct((n, CONV_DIM), F32),
                 jax.ShapeDtypeStruct((n, H_GDN * GDN_DV), F32), jax.ShapeDtypeStruct((n, MLA_IN), F32)]
    scratch = [pltpu.VMEM((tm, D_MODEL), BF16), pltpu.VMEM((tm, D_MODEL), F32)]
    if stream_tiles:
        consts += (hist, w['conv_w'])
        out_specs[1] = pl.BlockSpec((tm, CONV_DIM), lambda i: (jnp.maximum(i - 1, 0), 0))
        out_specs.append(pl.BlockSpec((1, GDN_HIST, CONV_DIM), lambda i: (tile(i), 0, 0)))
        out_shape.append(jax.ShapeDtypeStruct((ntile, GDN_HIST, CONV_DIM), F32))
        scratch.append(pltpu.VMEM((1, GDN_HIST + tm, CONV_DIM), F32))
    return pl.pallas_call(
        functools.partial(_ffn_proj_kernel, stream_tiles=stream_tiles),
        grid=(ntile + (1 if stream_tiles else 0),),
        in_specs=[row(D_MODEL)] + [_const_spec(c.shape) for c in consts],
        out_specs=out_specs,
        out_shape=out_shape,
        scratch_shapes=scratch,
        compiler_params=_params("arbitrary" if stream_tiles else "parallel"),
        name="ffn_proj",
    )(x, *consts)


def _out_ffn_kernel(x1_ref, gdn_ref, mla_ref, wog_ref, wom_ref, g2_ref, wg_ref, wu_ref, wd_ref, gf_ref,
                    y_ref, h_ref, acc_ref):
    x2 = (x1_ref[...] + jnp.dot(gdn_ref[...], wog_ref[...], preferred_element_type=F32)
          + jnp.dot(mla_ref[...], wom_ref[...], preferred_element_type=F32))
    x3 = _swiglu_half(x2, g2_ref, wg_ref, wu_ref, wd_ref, h_ref, acc_ref)
    y_ref[...] = _rms(x3, gf_ref[...])


def out_ffn(x1, gdn, mla, w, tm=512):
    n = x1.shape[0]
    tm = _row_tile(n, tm)
    row = lambda width: pl.BlockSpec((tm, width), lambda i: (i, 0))
    consts = (w['w_out_g'], w['w_out_m'], w['g2'], w['wg2'], w['wu2'], w['wd2'], w['gf'])
    return pl.pallas_call(
        _out_ffn_kernel,
        grid=(n // tm,),
        in_specs=[row(D_MODEL), row(H_GDN * GDN_DV), row(H_MLA * DV_MLA)] + [_const_spec(c.shape) for c in consts],
        out_specs=row(D_MODEL),
        out_shape=jax.ShapeDtypeStruct((n, D_MODEL), F32),
        scratch_shapes=[pltpu.VMEM((tm, D_MODEL), BF16), pltpu.VMEM((tm, D_MODEL), F32)],
        compiler_params=_params("parallel"),
        name="out_ffn",
    )(x1, gdn, mla, *consts)


def _cumsum_rows(x):
    n = x.shape[0]
    row = lax.broadcasted_iota(jnp.int32, x.shape, 0)
    shift = 1
    while shift < n:
        x = x + jnp.where(row >= shift, pltpu.roll(x, shift, 0), 0.0)
        shift *= 2
    return x


def _transpose_rows(x):
    length = x.shape[0]
    sq = jnp.concatenate([x, jnp.zeros((LANES - length, LANES), x.dtype)], axis=0)
    return sq.T[:, :length]


GDN_HIST = SUBLANES
GDN_PREP_COST = 4.0
GDN_PROMPT_ROWS = 2 * CHUNK
GDN_SAMPLE_GROUP = 8


def _gdn_features(conv_scr, convw_ref, i, row0):
    base = GDN_HIST - (CONV_W - 1) + row0
    parts = []
    for h in range(CONV_DIM // GDN_DK):
        cols = slice(h * GDN_DK, (h + 1) * GDN_DK)
        x = conv_scr[i, pl.ds(base, CHUNK), cols] * convw_ref[0:1, cols]
        for j in range(1, CONV_W):
            x = x + conv_scr[i, pl.ds(base + j, CHUNK), cols] * convw_ref[j:j + 1, cols]
        x = jax.nn.silu(x)
        if h < 2 * H_GDN:
            x = x * lax.rsqrt(jnp.sum(x * x, axis=-1, keepdims=True) + L2_EPS)
        parts.append(x * (GDN_DK ** -0.5) if h < H_GDN else x)
    return jnp.concatenate(parts, axis=1)


def _gdn_block_stages(first_block, rows, pad_front, features, ab_ref, z_ref, write_o, m_scr,
                      alog_ref, dtb_ref, gn_ref):
    bg = ab_ref.shape[0]
    length = CHUNK
    heads = H_GDN
    width = heads * length
    row = lax.broadcasted_iota(jnp.int32, (length, LANES), 0)
    lane = lax.broadcasted_iota(jnp.int32, (length, width), 1)
    lane_head = lane // length
    lane_col = lane % length
    row_p = lax.broadcasted_iota(jnp.int32, (length, width), 0)
    causal = lane_col <= row_p
    strict = lane_col < row_p

    def spread(cols):
        out = jnp.broadcast_to(cols[-1], (length, width))
        for h in range(heads - 2, -1, -1):
            out = jnp.where(lane_head == h, cols[h], out)
        return out

    def pick(mats):
        out = mats[-1]
        for h in range(heads - 2, -1, -1):
            out = jnp.where(lane_head == h, mats[h], out)
        return out

    def diag_blocks(p):
        return jnp.concatenate([jnp.where(lane_head == h, p, 0.0) for h in range(heads)], axis=0).astype(BF16)

    def diag_wide(mats):
        zero = jnp.zeros_like(mats[0])
        return jnp.concatenate(
            [jnp.concatenate([mats[h] if g == h else zero for g in range(heads)], axis=1) for h in range(heads)],
            axis=0).astype(BF16)

    def per_batch(i, row0):
        u = features(i, row0)
        ab = ab_ref[i, row0:row0 + length, :]
        g_all = -jnp.exp(alog_ref[...]) * jax.nn.softplus(ab + dtb_ref[...])
        beta_all = jax.nn.sigmoid(ab)
        if pad_front > row0:
            valid = jnp.logical_or(jnp.logical_not(first_block), row >= pad_front - row0)
            g_all = jnp.where(valid, g_all, 0.0)
            beta_all = jnp.where(valid, beta_all, 0.0)
        gc_all = _cumsum_rows(g_all)
        gc_rows = jnp.concatenate([gc_all, jnp.zeros((LANES - length, LANES), F32)], axis=0).T
        gc_rows = gc_rows[AB_LANE:AB_LANE + SUBLANES, :]
        gc_rows_hi = pltpu.roll(gc_rows, length, 1)
        gc_row = jnp.concatenate([gc_rows[h:h + 1] + gc_rows_hi[h + 1:h + 2] for h in range(0, heads, 2)], axis=1)

        hd = []
        for h in range(heads):
            q = u[:, h * GDN_DK:(h + 1) * GDN_DK]
            k = u[:, (heads + h) * GDN_DK:(heads + h + 1) * GDN_DK]
            v = u[:, 2 * heads * GDN_DK + h * GDN_DV:2 * heads * GDN_DK + (h + 1) * GDN_DV]
            gc = gc_all[:, AB_LANE + h:AB_LANE + h + 1]
            beta = beta_all[:, AB_LANE + heads + h:AB_LANE + heads + h + 1]
            kq = jnp.concatenate([k, q], axis=0).astype(BF16)
            hd.append(dict(h=h, k=k, v=v, gc=gc, beta=beta, kq=kq))
        gc_col = spread([c['gc'] for c in hd])
        decay = jnp.where(causal, jnp.exp(jnp.where(causal, gc_col - gc_row, 0.0)), 0.0)
        k_all = jnp.concatenate([c['k'] for c in hd], axis=0).astype(BF16)
        return dict(i=i, row0=row0, heads=hd, decay=decay, beta=spread([c['beta'] for c in hd]), k_all=k_all)

    per_chunk = []
    for row0 in range(0, rows, length):
        per_chunk.append([])
        for i in range(bg):
            per_chunk[-1].append(per_batch(i, row0))
            yield GDN_PREP_COST
    groups = [g for chunk in per_chunk for g in chunk]
    for g in groups:
        scores = [lax.dot_general(c['kq'], g['k_all'], NT_DIMS, preferred_element_type=F32) for c in g['heads']]
        g['kk'] = pick([s[:length] for s in scores])
        g['qk'] = pick([s[length:] for s in scores])
    yield 1.0
    for g in groups:
        g['a'] = jnp.where(strict, g['beta'] * g['kk'] * g['decay'], 0.0)
        g['qk_decay'] = g['qk'] * g['decay']
        g['y'] = -g['a']
    for g in groups:
        g['pw'] = jnp.dot(g['a'].astype(BF16), diag_blocks(g['a']), preferred_element_type=F32)
    yield 1.0
    span = 2
    while span < length:
        span *= 2
        for g in groups:
            blocks = diag_blocks(g['pw'])
            if span < length:
                prod = jnp.dot(jnp.concatenate([g['y'], g['pw']], axis=0).astype(BF16), blocks,
                               preferred_element_type=F32)
                g['y'] = g['y'] + g['pw'] + prod[:length]
                g['pw'] = prod[length:]
            else:
                g['y'] = g['y'] + g['pw'] + jnp.dot(g['y'].astype(BF16), blocks, preferred_element_type=F32)
        yield 1.0

    def emit_output(chunk):
        for g in chunk:
            prod = jnp.dot(g['qk_decay'].astype(BF16), diag_wide([c['uu'] for c in g['heads']]),
                           preferred_element_type=F32)
            rws = slice(g['row0'], g['row0'] + length)
            for c in g['heads']:
                cols = slice(c['h'] * GDN_DV, (c['h'] + 1) * GDN_DV)
                o = c['eg'] * c['kqm'][length:] + prod[:, cols]
                n = _rms(o, gn_ref[...])
                write_o(g['i'], rws, cols, (n * jax.nn.silu(z_ref[g['i'], rws, cols])).astype(BF16))

    pending = None
    for chunk in per_chunk:
        for g in chunk:
            for c in g['heads']:
                c['m0'] = m_scr[g['i'], c['h']]
                c['kqm'] = jnp.dot(c['kq'], c['m0'].astype(BF16), preferred_element_type=F32)
        yield 1.0
        if pending is not None:
            emit_output(pending)
            yield 1.0
        for g in chunk:
            for c in g['heads']:
                c['eg'] = jnp.exp(c['gc'])
                c['rhs'] = c['beta'] * (c['v'] - c['eg'] * c['kqm'][:length])
            prod = jnp.dot(g['y'].astype(BF16), diag_wide([c['rhs'] for c in g['heads']]),
                           preferred_element_type=F32)
            for c in g['heads']:
                c['uu'] = c['rhs'] + prod[:, c['h'] * GDN_DV:(c['h'] + 1) * GDN_DV]
        yield 1.0
        for g in chunk:
            for c in g['heads']:
                g_last = c['gc'][length - 1:length, :]
                k_dec = c['k'] * jnp.exp(g_last - c['gc'])
                m_scr[g['i'], c['h']] = jnp.exp(g_last) * c['m0'] + _bdot(_transpose_rows(k_dec), c['uu'])
        yield 1.0
        pending = chunk
    emit_output(pending)
    yield 1.0


def _gdn_stage_block(qkv_ref, conv_scr):
    bg, rows, _ = qkv_ref.shape
    for i in range(bg):
        conv_scr[i, GDN_HIST:GDN_HIST + rows, :] = qkv_ref[i]


def _gdn_keep_history(conv_scr, rows):
    for i in range(conv_scr.shape[0]):
        conv_scr[i, 0:GDN_HIST, :] = conv_scr[i, rows:rows + GDN_HIST, :]


def _gdn_kernel(qkv_ref, z_ref, ab_ref, m0_ref, conv0_ref, convw_ref, alog_ref, dtb_ref, gn_ref,
                o_ref, mout_ref, conv_scr, m_scr, *, pad_front, prepared):
    c = pl.program_id(1)
    rows = qkv_ref.shape[1]

    @pl.when(c == 0)
    def _():
        m_scr[...] = m0_ref[...]
        conv_scr[:, 0:GDN_HIST, :] = conv0_ref[...]

    def write_o(i, rws, cols, val):
        o_ref[i, rws, cols] = val

    if prepared:
        features = lambda i, row0: qkv_ref[i, row0:row0 + CHUNK, :]
    else:
        _gdn_stage_block(qkv_ref, conv_scr)
        features = functools.partial(_gdn_features, conv_scr, convw_ref)
    _drain(_gdn_block_stages(c == 0, rows, pad_front, features, ab_ref, z_ref, write_o, m_scr,
                             alog_ref, dtb_ref, gn_ref))
    if not prepared:
        _gdn_keep_history(conv_scr, rows)

    @pl.when(c == pl.num_programs(1) - 1)
    def _():
        mout_ref[...] = m_scr[...]


def _gdn_in_specs(bg, rows, blk):
    nd = len(blk(0, 0))
    const = lambda shape: pl.BlockSpec(shape, lambda *_: (0,) * len(shape))
    return [
        pl.BlockSpec((bg, rows, CONV_DIM), blk),
        pl.BlockSpec((bg, rows, H_GDN * GDN_DV), blk),
        pl.BlockSpec((bg, rows, LANES), lambda *a: blk(*a)[:nd - 1] + (MLA_IN // LANES - 1,)),
        pl.BlockSpec((bg, H_GDN, GDN_DK, GDN_DV), lambda g, c: (g, 0, 0, 0)),
        pl.BlockSpec((bg, GDN_HIST, CONV_DIM), lambda g, c: (g, 0, 0)),
        const((CONV_W, CONV_DIM)),
        const((1, LANES)),
        const((1, LANES)),
        const((1, GDN_DV)),
    ]


def gdn(qkv, z, mla_in, m0, conv0, w, *, pad_front=0, bg=4, rows=CHUNK, prepared=False):
    b, t, _ = qkv.shape
    bg = min(bg, b)
    assert b % bg == 0 and t % rows == 0 and rows % CHUNK == 0
    blk = lambda g, c: (g, c, 0)
    return pl.pallas_call(
        functools.partial(_gdn_kernel, pad_front=pad_front, prepared=prepared),
        grid=(b // bg, t // rows),
        in_specs=_gdn_in_specs(bg, rows, blk),
        out_specs=[
            pl.BlockSpec((bg, rows, H_GDN * GDN_DV), blk),
            pl.BlockSpec((bg, H_GDN, GDN_DK, GDN_DV), lambda g, c: (g, 0, 0, 0)),
        ],
        out_shape=[jax.ShapeDtypeStruct((b, t, H_GDN * GDN_DV), BF16),
                   jax.ShapeDtypeStruct((b, H_GDN, GDN_DK, GDN_DV), F32)],
        scratch_shapes=[pltpu.VMEM((bg, GDN_HIST + (0 if prepared else rows), CONV_DIM), F32),
                        pltpu.VMEM((bg, H_GDN, GDN_DK, GDN_DV), F32)],
        compiler_params=_params("parallel", "arbitrary"),
        name="gdn",
    )(qkv, z, mla_in, m0, conv0, w['conv_w'], w['alog'], w['dtb'], w['gn'])


def _rope(t, cos, sin, duplicated=False):
    half = DR // 2
    if duplicated:
        swapped = pltpu.roll(t, half, 1)
    else:
        lane = lax.broadcasted_iota(jnp.int32, t.shape, 1)
        swapped = jnp.where(lane < half, pltpu.roll(t, LANES - half, 1), pltpu.roll(t, half, 1))
    return t * cos + swapped * sin


def _mla_prep_kernel(x_ref, cos_ref, sin_ref, gq_ref, gkv_ref, wuq_ref, wuk_ref, wuvt_ref, *out_refs, expand):
    q_ref, c_ref, kr_ref = out_refs[:3]
    x = x_ref[...]
    cos = cos_ref[...]
    sin = sin_ref[...]
    q = _bdot(_rms(x[:, :Q_LORA], gq_ref[...]), wuq_ref[...])
    for h in range(H_MLA):
        lo = h * QK_PAD
        q_ref[:, lo:lo + DN] = (q[:, lo:lo + DN] * Q_SCALE).astype(BF16)
        q_ref[:, lo + DN:lo + QK_PAD] = (_rope(q[:, lo + DN:lo + QK_PAD], cos, sin, True) * Q_SCALE).astype(BF16)
    c = _rms(x[:, Q_LORA:Q_LORA + KV_LORA], gkv_ref[...])
    c_ref[...] = c
    kr = _rope(x[:, Q_LORA + KV_LORA:], cos, sin)
    if not expand:
        kr_ref[...] = kr
    else:
        kr_ref[...] = (kr.T if kr.shape[0] % LANES == 0 else _transpose_rows(kr))[:DR]
        k_ref, vt_ref = out_refs[3:]
        c16 = c.astype(BF16)
        k_nope = jnp.dot(c16, wuk_ref[...], preferred_element_type=F32)
        for h in range(H_MLA):
            k_ref[:, h * QK_PAD:h * QK_PAD + DN] = k_nope[:, h * DN:(h + 1) * DN].astype(BF16)
            k_ref[:, h * QK_PAD + DN:(h + 1) * QK_PAD] = kr.astype(BF16)
        vt_ref[...] = lax.dot_general(wuvt_ref[...], c16, NT_DIMS, preferred_element_type=F32).astype(BF16)


MLA_PREP_TILE = 1024


def mla_prep(mla_in, cos, sin, w, *, expand, tm=MLA_PREP_TILE):
    n = mla_in.shape[0]
    tm = _row_tile(min(n, cos.shape[0]), tm)
    nrep = cos.shape[0] // tm
    row = lambda width: pl.BlockSpec((tm, width), lambda i: (i, 0))
    tab = pl.BlockSpec((tm, LANES), lambda i: (i % nrep, 0))
    consts = (w['gq'], w['gkv'], w['w_uq'], w['w_uk'], w['w_uv_t'])
    out_specs = [row(H_MLA * QK_PAD), row(KV_LORA), row(LANES)]
    out_shape = [jax.ShapeDtypeStruct((n, H_MLA * QK_PAD), BF16), jax.ShapeDtypeStruct((n, KV_LORA), F32),
                 jax.ShapeDtypeStruct((n, LANES), F32)]
    if expand:
        out_specs[2] = pl.BlockSpec((DR, tm), lambda i: (0, i))
        out_shape[2] = jax.ShapeDtypeStruct((DR, n), F32)
        out_specs += [row(H_MLA * QK_PAD), pl.BlockSpec((H_MLA * DV_MLA, tm), lambda i: (0, i))]
        out_shape += [jax.ShapeDtypeStruct((n, H_MLA * QK_PAD), BF16), jax.ShapeDtypeStruct((H_MLA * DV_MLA, n), BF16)]
    return pl.pallas_call(
        functools.partial(_mla_prep_kernel, expand=expand),
        grid=(n // tm,),
        in_specs=[row(MLA_IN), tab, tab] + [_const_spec(c.shape) for c in consts],
        out_specs=out_specs,
        out_shape=out_shape,
        compiler_params=_params("parallel"),
        name="mla_prep_kv" if expand else "mla_prep",
    )(mla_in, cos, sin, *consts)


def _mla_prompt_kernel(q_ref, k_ref, vt_ref, km_ref, vmt_ref, o_ref, s_scr, m_scr, l_scr, acc_scr, *, tq):
    step = pl.program_id(1)
    nq = q_ref.shape[0] // tq
    chains = [(a, h) for a in range(nq) for h in range(H_MLA)]
    first = step * nq

    def q_of(c):
        a, h = chains[c]
        return q_ref[a * tq:(a + 1) * tq, h * QK_PAD:(h + 1) * QK_PAD]

    def scores(tile, c):
        h = chains[c][1]
        off = pl.multiple_of(tile * tq, tq)
        return lax.dot_general(k_ref[pl.ds(off, tq), h * QK_PAD:(h + 1) * QK_PAD], q_of(c), NT_DIMS,
                               preferred_element_type=F32)

    def consume(c, tile, masked):
        h = chains[c][1]
        off = pl.multiple_of(tile * tq, tq)
        s_t = s_scr[c]
        if masked:
            key_chunk = lax.broadcasted_iota(jnp.int32, (tq, tq), 0) // CHUNK
            qry_chunk = lax.broadcasted_iota(jnp.int32, (tq, tq), 1) // CHUNK
            s_t = jnp.where(key_chunk <= qry_chunk, s_t, -jnp.inf)
        m = m_scr[c]
        m_new = jnp.maximum(m, jnp.max(s_t, axis=0, keepdims=True))
        alpha = jnp.exp2(m - m_new)
        p_t = jnp.exp2(s_t - m_new)
        m_scr[c] = m_new
        l_scr[c] = alpha * l_scr[c] + jnp.sum(p_t, axis=0, keepdims=True)
        acc_scr[c] = alpha * acc_scr[c] + jnp.dot(vt_ref[h * DV_MLA:(h + 1) * DV_MLA, pl.ds(off, tq)],
                                                  p_t.astype(BF16), preferred_element_type=F32)

    def fold(tile, live, masked_tile):
        for c, (a, h) in enumerate(chains):
            if a < live:
                continue
            if a == masked_tile:
                consume(c, tile, True)
            else:
                s_next = scores(tile + 1, c)
                consume(c, tile, False)
                s_scr[c] = s_next

    meta_s = [lax.dot_general(km_ref[:, h * QK_PAD:(h + 1) * QK_PAD], q_of(c), NT_DIMS,
                              preferred_element_type=F32) for c, (a, h) in enumerate(chains)]
    for c in range(len(chains)):
        s_scr[c] = scores(0, c)
    meta_p = []
    for c in range(len(chains)):
        m = jnp.max(meta_s[c], axis=0, keepdims=True)
        p_t = jnp.exp2(meta_s[c] - m)
        m_scr[c] = m
        l_scr[c] = jnp.sum(p_t, axis=0, keepdims=True)
        meta_p.append(p_t.astype(BF16))
    for c, (a, h) in enumerate(chains):
        acc_scr[c] = jnp.dot(vmt_ref[h * DV_MLA:(h + 1) * DV_MLA, :], meta_p[c], preferred_element_type=F32)

    def trip(t, carry):
        for u in range(nq):
            fold(t * nq + u, 0, None)
        return carry

    lax.fori_loop(0, step, trip, 0)
    for a in range(nq):
        fold(first + a, a, a)
    for c, (a, h) in enumerate(chains):
        o_ref[a * tq:(a + 1) * tq, h * DV_MLA:(h + 1) * DV_MLA] = (acc_scr[c] / l_scr[c]).T.astype(o_ref.dtype)


def mla_prompt(q, k, v_t, k_meta, v_meta_t, tq=512, nq=2):
    b, s, _ = q.shape
    tq = _row_tile(s, tq)
    nq = min(nq, s // tq)
    assert tq % LANES == 0 and s % (nq * tq) == 0
    rows = nq * tq
    nchain = nq * H_MLA
    whole = lambda shape: pl.BlockSpec(shape, lambda bi, i: (0, 0))
    return pl.pallas_call(
        functools.partial(_mla_prompt_kernel, tq=tq),
        grid=(b, s // rows),
        in_specs=[
            pl.BlockSpec((None, rows, H_MLA * QK_PAD), lambda bi, i: (bi, i, 0)),
            pl.BlockSpec((None, s, H_MLA * QK_PAD), lambda bi, i: (bi, 0, 0), pipeline_mode=pl.Buffered(1)),
            pl.BlockSpec((H_MLA * DV_MLA, s), lambda bi, i: (0, bi), pipeline_mode=pl.Buffered(1)),
            whole(k_meta.shape),
            whole(v_meta_t.shape),
        ],
        out_specs=pl.BlockSpec((None, rows, H_MLA * DV_MLA), lambda bi, i: (bi, i, 0)),
        out_shape=jax.ShapeDtypeStruct((b, s, H_MLA * DV_MLA), BF16),
        scratch_shapes=[pltpu.VMEM((nchain, tq, tq), F32), pltpu.VMEM((nchain, 1, tq), F32),
                        pltpu.VMEM((nchain, 1, tq), F32), pltpu.VMEM((nchain, DV_MLA, tq), F32)],
        compiler_params=_params("parallel", "arbitrary"),
        name="mla_prompt",
    )(q, k, v_t, k_meta, v_meta_t)


def _mla_sample_kernel(q_ref, c_ref, kr_ref, cc_ref, ckr_ref, wukv_ref, o_ref):
    nreq, t, _ = q_ref.shape
    w = wukv_ref[...]
    reqs = []
    for r in range(nreq):
        q = q_ref[r]
        qa = jnp.concatenate(
            [lax.dot_general(q[:, h * QK_PAD:h * QK_PAD + DN], w[:, h * (DN + DV_MLA):h * (DN + DV_MLA) + DN],
                             NT_DIMS, preferred_element_type=F32) for h in range(H_MLA)], axis=0).astype(BF16)
        qr = jnp.concatenate([q[:, h * QK_PAD + DN:(h + 1) * QK_PAD] for h in range(H_MLA)], axis=0)
        reqs.append(dict(qa=qa, qr=qr, cache_c=cc_ref[r].astype(BF16), own_c=c_ref[r].astype(BF16)))
    for r, rq in enumerate(reqs):
        rq['s_cache'] = (lax.dot_general(rq['qa'], rq['cache_c'], NT_DIMS, preferred_element_type=F32)
                         + jnp.dot(rq['qr'][:, :DR], ckr_ref[r].astype(BF16), preferred_element_type=F32))
        rq['s_own'] = (lax.dot_general(rq['qa'], rq['own_c'], NT_DIMS, preferred_element_type=F32)
                       + lax.dot_general(rq['qr'], kr_ref[r].astype(BF16), NT_DIMS, preferred_element_type=F32))
    for rq in reqs:
        m = jnp.maximum(jnp.max(rq['s_cache'], axis=-1, keepdims=True), jnp.max(rq['s_own'], axis=-1, keepdims=True))
        p_cache = jnp.exp2(rq['s_cache'] - m)
        p_own = jnp.exp2(rq['s_own'] - m)
        rq['l'] = jnp.sum(p_cache, axis=-1, keepdims=True) + jnp.sum(p_own, axis=-1, keepdims=True)
        rq['p_cache'], rq['p_own'] = p_cache.astype(BF16), p_own.astype(BF16)
    for rq in reqs:
        rq['pc'] = (jnp.dot(rq['p_cache'], rq['cache_c'], preferred_element_type=F32)
                    + jnp.dot(rq['p_own'], rq['own_c'], preferred_element_type=F32)) / rq['l']
    for r, rq in enumerate(reqs):
        for h in range(H_MLA):
            lo = h * (DN + DV_MLA) + DN
            o_ref[r, :, h * DV_MLA:(h + 1) * DV_MLA] = _bdot(rq['pc'][h * t:(h + 1) * t],
                                                             w[:, lo:lo + DV_MLA]).astype(o_ref.dtype)


def mla_sample(q, c, kr, cache_c, cache_kr, w_ukv, group=4):
    b, t, _ = q.shape
    p = cache_c.shape[1]
    group = min(group, b)
    assert b % group == 0
    per_b = lambda rows, width: pl.BlockSpec((group, rows, width), lambda bi: (bi, 0, 0))
    return pl.pallas_call(
        _mla_sample_kernel,
        grid=(b // group,),
        in_specs=[per_b(t, H_MLA * QK_PAD), per_b(t, KV_LORA), per_b(t, LANES), per_b(p, KV_LORA), per_b(DR, p),
                  _const_spec(w_ukv.shape)],
        out_specs=per_b(t, H_MLA * DV_MLA),
        out_shape=jax.ShapeDtypeStruct((b, t, H_MLA * DV_MLA), BF16),
        compiler_params=_params("parallel"),
        name="mla_sample",
    )(q, c, kr, cache_c, cache_kr, w_ukv)


def _rope_tables(first, count, repeat=1):
    inv = ROPE_BASE ** (-np.arange(0, DR, 2, dtype=np.float64) / DR)
    ang = (first + np.arange(count, dtype=np.float64))[:, None] * inv[None, :]
    cos, sin = np.cos(ang), np.sin(ang)
    pad = np.zeros((count, LANES - DR))
    table = lambda parts: jnp.asarray(np.tile(np.concatenate(parts, axis=1), (repeat, 1)), F32)
    return table([cos, cos, pad]), table([-sin, sin, pad])


def _prepare_weights(ffn1_norm, ffn1_wg, ffn1_wu, ffn1_wd, mix_norm, w_in, conv_w, a_log, dt_bias, gdn_norm,
                     q_norm, kv_norm, w_uq, w_ukv, w_out, ffn2_norm, ffn2_wg, ffn2_wu, ffn2_wd, final_norm):
    w = {}
    w['g1'], w['gm'], w['g2'] = ffn1_norm[0][None], mix_norm[0][None], ffn2_norm[0][None]
    w['gf'] = final_norm[None]
    assert D_FF % FF_CHUNK == 0
    w['wg1'], w['wu1'], w['wd1'] = (m[0].astype(BF16) for m in (ffn1_wg, ffn1_wu, ffn1_wd))
    w['wg2'], w['wu2'], w['wd2'] = (m[0].astype(BF16) for m in (ffn2_wg, ffn2_wu, ffn2_wd))
    o_z = CONV_DIM
    o_a = o_z + H_GDN * GDN_DV
    o_b = o_a + H_GDN
    o_cq = o_b + H_GDN
    o_kr = o_cq + Q_LORA + KV_LORA
    wi = w_in[0]
    w['w_qkv'] = wi[:, :o_z].astype(BF16)
    w['w_z'] = wi[:, o_z:o_a].astype(BF16)
    tail_pad = jnp.zeros((D_MODEL, LANES - DR - 2 * H_GDN), wi.dtype)
    w['w_mla'] = jnp.concatenate([wi[:, o_cq:o_kr], wi[:, o_kr:o_kr + DR], wi[:, o_a:o_cq], tail_pad], axis=1).astype(BF16)
    w['conv_w'] = conv_w[0]
    lane_vec = lambda v: jnp.zeros((1, LANES), F32).at[0, AB_LANE:AB_LANE + H_GDN].set(v.astype(F32))
    w['alog'], w['dtb'] = lane_vec(a_log[0]), lane_vec(dt_bias[0])
    w['gn'] = gdn_norm[0][None]
    w['gq'], w['gkv'] = q_norm[0][None], kv_norm[0][None]
    uq = w_uq[0].reshape(Q_LORA, H_MLA, DN + DR)
    assert QK_PAD - DN - DR == DR
    uq = jnp.concatenate([uq, uq[:, :, DN:]], axis=-1)
    w['w_uq'] = uq.reshape(Q_LORA, H_MLA * QK_PAD).astype(BF16)
    w['w_ukv'] = w_ukv[0].astype(BF16)
    ukv = w['w_ukv'].reshape(KV_LORA, H_MLA, DN + DV_MLA)
    w['w_uk'] = ukv[:, :, :DN].reshape(KV_LORA, H_MLA * DN)
    w['w_uv_t'] = ukv[:, :, DN:].reshape(KV_LORA, H_MLA * DV_MLA).T
    w['w_out_g'] = w_out[0][:H_GDN * GDN_DV].astype(BF16)
    w['w_out_m'] = w_out[0][H_GDN * GDN_DV:].astype(BF16)
    return w


def kernel(x_prompt, x_sample, cache_mla_ckv, cache_mla_krope, state_gdn, state_conv, meta, ffn1_norm, ffn1_wg,
           ffn1_wu, ffn1_wd, mix_norm, w_in, conv_w, a_log, dt_bias, gdn_norm, q_norm, kv_norm, w_uq, w_ukv, w_out,
           ffn2_norm, ffn2_wg, ffn2_wu, ffn2_wd, final_norm):
    assert ffn1_wg.shape[0] == 1, "one layer: the meta rows are not carried past the mixer"
    bsz, s_len, _ = x_prompt.shape
    dbs, d_seq, _ = x_sample.shape
    past = cache_mla_ckv.shape[2]
    w = _prepare_weights(ffn1_norm, ffn1_wg, ffn1_wu, ffn1_wd, mix_norm, w_in, conv_w, a_log, dt_bias, gdn_norm,
                         q_norm, kv_norm, w_uq, w_ukv, w_out, ffn2_norm, ffn2_wg, ffn2_wu, ffn2_wd, final_norm)

    front = CHUNK - N_META
    _, qkv_m, _, mla_m = ffn_proj(meta.astype(F32), w)
    meta_qkv = jnp.pad(qkv_m, ((front, 0), (0, 0)))[None]
    x1_p, feat_p, z_p, mla_p, tail_p = ffn_proj(x_prompt.reshape(bsz * s_len, D_MODEL), w,
                                                 hist=meta_qkv[0, CHUNK - GDN_HIST:], stream_len=s_len)
    x1_s, qkv_s, z_s, mla_s = ffn_proj(x_sample.reshape(dbs * d_seq, D_MODEL), w)

    _, m_meta = gdn(meta_qkv, jnp.zeros((1, CHUNK, H_GDN * GDN_DV), F32), jnp.pad(mla_m, ((front, 0), (0, 0)))[None],
                    jnp.zeros((1, H_GDN, GDN_DK, GDN_DV), F32), jnp.zeros((1, GDN_HIST, CONV_DIM), F32), w,
                    pad_front=front)
    m0_p = jnp.broadcast_to(m_meta, (bsz,) + m_meta.shape[1:])
    gdn_p, m_p = gdn(feat_p.reshape(bsz, s_len, CONV_DIM), z_p.reshape(bsz, s_len, -1),
                     mla_p.reshape(bsz, s_len, MLA_IN), m0_p, jnp.zeros((bsz, GDN_HIST, CONV_DIM), F32), w,
                     rows=GDN_PROMPT_ROWS, prepared=True)
    conv_p = tail_p.reshape(bsz, -1, GDN_HIST, CONV_DIM)[:, -1, GDN_HIST - (CONV_W - 1):]
    qkv_s3 = qkv_s.reshape(dbs, d_seq, CONV_DIM)
    conv0_s = jnp.pad(state_conv[0].astype(F32), ((0, 0), (GDN_HIST - (CONV_W - 1), 0), (0, 0)))
    gdn_s, m_s = gdn(qkv_s3, z_s.reshape(dbs, d_seq, -1), mla_s.reshape(dbs, d_seq, MLA_IN),
                     state_gdn[0].astype(F32), conv0_s, w, bg=GDN_SAMPLE_GROUP)

    cos_m, sin_m = _rope_tables(0, N_META)
    cos_p, sin_p = _rope_tables(N_META, s_len)
    cos_s, sin_s = _rope_tables(past, d_seq, repeat=min(dbs, MLA_PREP_TILE // d_seq))
    q_p, c_p, kr_p, k_p, vt_p = mla_prep(mla_p, cos_p, sin_p, w, expand=True)
    _, c_m, kr_m, k_m, vt_m = mla_prep(mla_m, cos_m, sin_m, w, expand=True)
    q_s, c_s, kr_s = mla_prep(mla_s, cos_s, sin_s, w, expand=False)
    mla_o_p = mla_prompt(q_p.reshape(bsz, s_len, -1), k_p.reshape(bsz, s_len, -1), vt_p, k_m, vt_m)
    mla_o_s = mla_sample(q_s.reshape(dbs, d_seq, -1), c_s.reshape(dbs, d_seq, -1), kr_s.reshape(dbs, d_seq, -1),
                         cache_mla_ckv[0].astype(F32), jnp.swapaxes(cache_mla_krope[0].astype(F32), 1, 2), w['w_ukv'])

    y_p = out_ffn(x1_p, gdn_p.reshape(bsz * s_len, -1), mla_o_p.reshape(bsz * s_len, -1), w)
    y_s = out_ffn(x1_s, gdn_s.reshape(dbs * d_seq, -1), mla_o_s.reshape(dbs * d_seq, -1), w)

    with_meta = lambda m_rows, rows: jnp.concatenate(
        [jnp.broadcast_to(m_rows[None], (bsz,) + m_rows.shape), rows.reshape(bsz, s_len, -1)], axis=1)
    krope_t = jnp.concatenate([jnp.broadcast_to(kr_m[None], (bsz, DR, N_META)),
                               kr_p.reshape(DR, bsz, s_len).transpose(1, 0, 2)], axis=2)
    return (y_p.reshape(bsz, s_len, D_MODEL), y_s.reshape(dbs, d_seq, D_MODEL),
            with_meta(c_m, c_p)[None], jnp.swapaxes(krope_t, 1, 2)[None],
            m_p[None], conv_p[None],
            c_s.reshape(dbs, d_seq, KV_LORA)[None], kr_s[:, :DR].reshape(dbs, d_seq, DR)[None],
            m_s[None], qkv_s3[:, d_seq - (CONV_W - 1):][None])
```

```python
import functools

import jax
import jax.numpy as jnp
import numpy as np
from jax import lax
from jax.experimental import pallas as pl
from jax.experimental.pallas import tpu as pltpu

F32 = jnp.float32
BF16 = jnp.bfloat16

D_MODEL = 1024
CHUNK = 64
N_META = 16
H_GDN = 4
GDN_DK = 128
GDN_DV = 128
CONV_W = 4
CONV_DIM = H_GDN * (2 * GDN_DK + GDN_DV)
H_MLA = 4
Q_LORA = 384
KV_LORA = 256
DN = 128
DR = 64
DV_MLA = 128
ROPE_BASE = 10000.0
SM_SCALE = (DN + DR) ** -0.5
LOG2_E = 1.4426950408889634
Q_SCALE = SM_SCALE * LOG2_E
D_FF = 2816
EPS = 1e-6
L2_EPS = 1e-6

LANES = 128
SUBLANES = 8
FF_CHUNK = 256
MLA_IN = Q_LORA + KV_LORA + LANES
QK_PAD = 2 * LANES
AB_LANE = DR
VMEM_LIMIT = 56 * 1024 * 1024

TOKEN_TILE = 512
MLA_PREP_TILE = 1024
ATTN_TILE = 512
ATTN_TILES_PER_STEP = 2
MLA_SAMPLE_GROUP = 4
GDN_GROUP = 4
GDN_SAMPLE_GROUP = 8
GDN_PROMPT_ROWS = 2 * CHUNK
GDN_HIST = SUBLANES
GDN_PREP_COST = 4.0

NT_DIMS = (((1,), (1,)), ((), ()))


def _rms(x, g):
    return x * lax.rsqrt(jnp.mean(x * x, axis=-1, keepdims=True) + EPS) * g


def _bdot(a, b):
    return jnp.dot(a.astype(BF16), b.astype(BF16), preferred_element_type=F32)


def _const_spec(shape):
    nd = len(shape)
    return pl.BlockSpec(shape, lambda *_: (0,) * nd, pipeline_mode=pl.Buffered(1))


def _params(*sem):
    return pltpu.CompilerParams(dimension_semantics=sem, vmem_limit_bytes=VMEM_LIMIT)


def _drain(*stage_gens):
    live = [[g, 0.0, float(t)] for g, t in (sg if isinstance(sg, tuple) else (sg, 1.0) for sg in stage_gens)]
    while live:
        entry = min(live, key=lambda e: e[1] / e[2])
        try:
            entry[1] += next(entry[0]) or 0.0
        except StopIteration:
            live.remove(entry)


def _swiglu_half_stages(x, g_ref, wg_ref, wu_ref, wd_ref, h_ref, acc_ref, result):
    h_ref[...] = _rms(x, g_ref[...]).astype(BF16)
    nf = wg_ref.shape[1] // FF_CHUNK

    def gate_up(f):
        cols = slice(f * FF_CHUNK, (f + 1) * FF_CHUNK)
        gate = jnp.dot(h_ref[...], wg_ref[:, cols], preferred_element_type=F32)
        up = jnp.dot(h_ref[...], wu_ref[:, cols], preferred_element_type=F32)
        return gate, up

    nxt = gate_up(0)
    yield 1.0
    for f in range(nf):
        gate, up = nxt
        if f + 1 < nf:
            nxt = gate_up(f + 1)
            yield 1.0
        act = (jax.nn.silu(gate) * up).astype(BF16)
        down = jnp.dot(act, wd_ref[f * FF_CHUNK:(f + 1) * FF_CHUNK, :], preferred_element_type=F32)
        if f == 0:
            acc_ref[...] = down
        else:
            acc_ref[...] += down
        yield 1.0
    result.append(x + 0.5 * acc_ref[...])


def _swiglu_half(x, g_ref, wg_ref, wu_ref, wd_ref, h_ref, acc_ref):
    result = []
    _drain(_swiglu_half_stages(x, g_ref, wg_ref, wu_ref, wd_ref, h_ref, acc_ref, result))
    return result[0]


def _ffn_proj_kernel(x_ref, g1_ref, wg_ref, wu_ref, wd_ref, gm_ref, wqkv_ref, wz_ref, wmla_ref, *rest,
                     stream_tiles):
    if not stream_tiles:
        x1_ref, qkv_ref, z_ref, mla_ref, h_ref, acc_ref = rest
        x1 = _swiglu_half(x_ref[...], g1_ref, wg_ref, wu_ref, wd_ref, h_ref, acc_ref)
    else:
        hist_ref, convw_ref, x1_ref, qkv_ref, z_ref, mla_ref, tail_ref, h_ref, acc_ref, conv_scr = rest
        i = pl.program_id(0)
        tm = x_ref.shape[0]

        @pl.when(i == 0)
        def _():
            conv_scr[...] = jnp.zeros_like(conv_scr)

        @pl.when((i + stream_tiles - 1) % stream_tiles == 0)
        def _():
            conv_scr[0, 0:GDN_HIST, :] = hist_ref[...]

        def feature_stages():
            for row0 in range(0, tm, CHUNK):
                qkv_ref[row0:row0 + CHUNK, :] = _gdn_features(conv_scr, convw_ref, 0, row0)
                yield 1.0
            conv_scr[0, 0:GDN_HIST, :] = conv_scr[0, tm:tm + GDN_HIST, :]

        result = []
        _drain((_swiglu_half_stages(x_ref[...], g1_ref, wg_ref, wu_ref, wd_ref, h_ref, acc_ref, result),
                2 * (D_FF // FF_CHUNK)), (feature_stages(), tm // CHUNK))
        x1 = result[0]
    x1_ref[...] = x1
    h_ref[...] = _rms(x1, gm_ref[...]).astype(BF16)
    qkv = jnp.dot(h_ref[...], wqkv_ref[...], preferred_element_type=F32)
    z_ref[...] = jnp.dot(h_ref[...], wz_ref[...], preferred_element_type=F32)
    mla_ref[...] = jnp.dot(h_ref[...], wmla_ref[...], preferred_element_type=F32)
    if stream_tiles:
        conv_scr[0, GDN_HIST:GDN_HIST + tm, :] = qkv
        tail_ref[0] = qkv[tm - GDN_HIST:, :]
    else:
        qkv_ref[...] = qkv


def _row_tile(n, want):
    t = min(want, n)
    assert n % t == 0, (n, t)
    return t


def ffn_proj(x, w, hist=None, stream_len=None, tm=TOKEN_TILE):
    n = x.shape[0]
    tm = _row_tile(n, tm)
    ntile = n // tm
    stream_tiles = 0
    if hist is not None:
        assert stream_len % tm == 0 and tm % CHUNK == 0
        stream_tiles = stream_len // tm
    tile = (lambda i: jnp.minimum(i, ntile - 1)) if stream_tiles else (lambda i: i)
    row = lambda width: pl.BlockSpec((tm, width), lambda i: (tile(i), 0))
    consts = (w['g1'], w['wg1'], w['wu1'], w['wd1'], w['gm'], w['w_qkv'], w['w_z'], w['w_mla'])
    out_specs = [row(D_MODEL), row(CONV_DIM), row(H_GDN * GDN_DV), row(MLA_IN)]
    out_shape = [jax.ShapeDtypeStruct((n, D_MODEL), F32), jax.ShapeDtypeStruct((n, CONV_DIM), F32),
                 jax.ShapeDtypeStruct((n, H_GDN * GDN_DV), F32), jax.ShapeDtypeStruct((n, MLA_IN), F32)]
    scratch = [pltpu.VMEM((tm, D_MODEL), BF16), pltpu.VMEM((tm, D_MODEL), F32)]
    if stream_tiles:
        consts += (hist, w['conv_w'])
        out_specs[1] = pl.BlockSpec((tm, CONV_DIM), lambda i: (jnp.maximum(i - 1, 0), 0))
        out_specs.append(pl.BlockSpec((1, GDN_HIST, CONV_DIM), lambda i: (tile(i), 0, 0)))
        out_shape.append(jax.ShapeDtypeStruct((ntile, GDN_HIST, CONV_DIM), F32))
        scratch.append(pltpu.VMEM((1, GDN_HIST + tm, CONV_DIM), F32))
    return pl.pallas_call(
        functools.partial(_ffn_proj_kernel, stream_tiles=stream_tiles),
        grid=(ntile + (1 if stream_tiles else 0),),
        in_specs=[row(D_MODEL)] + [_const_spec(c.shape) for c in consts],
        out_specs=out_specs,
        out_shape=out_shape,
        scratch_shapes=scratch,
        compiler_params=_params("arbitrary" if stream_tiles else "parallel"),
        name="ffn_proj",
    )(x, *consts)


def _out_ffn_kernel(x1_ref, gdn_ref, mla_ref, wog_ref, wom_ref, g2_ref, wg_ref, wu_ref, wd_ref, gf_ref,
                    y_ref, h_ref, acc_ref):
    x2 = (x1_ref[...] + jnp.dot(gdn_ref[...], wog_ref[...], preferred_element_type=F32)
          + jnp.dot(mla_ref[...], wom_ref[...], preferred_element_type=F32))
    x3 = _swiglu_half(x2, g2_ref, wg_ref, wu_ref, wd_ref, h_ref, acc_ref)
    y_ref[...] = _rms(x3, gf_ref[...])


def out_ffn(x1, gdn, mla, w, tm=TOKEN_TILE):
    n = x1.shape[0]
    tm = _row_tile(n, tm)
    row = lambda width: pl.BlockSpec((tm, width), lambda i: (i, 0))
    consts = (w['w_out_g'], w['w_out_m'], w['g2'], w['wg2'], w['wu2'], w['wd2'], w['gf'])
    return pl.pallas_call(
        _out_ffn_kernel,
        grid=(n // tm,),
        in_specs=[row(D_MODEL), row(H_GDN * GDN_DV), row(H_MLA * DV_MLA)] + [_const_spec(c.shape) for c in consts],
        out_specs=row(D_MODEL),
        out_shape=jax.ShapeDtypeStruct((n, D_MODEL), F32),
        scratch_shapes=[pltpu.VMEM((tm, D_MODEL), BF16), pltpu.VMEM((tm, D_MODEL), F32)],
        compiler_params=_params("parallel"),
        name="out_ffn",
    )(x1, gdn, mla, *consts)


def _cumsum_rows(x):
    n = x.shape[0]
    row = lax.broadcasted_iota(jnp.int32, x.shape, 0)
    shift = 1
    while shift < n:
        x = x + jnp.where(row >= shift, pltpu.roll(x, shift, 0), 0.0)
        shift *= 2
    return x


def _transpose_rows(x):
    length = x.shape[0]
    sq = jnp.concatenate([x, jnp.zeros((LANES - length, LANES), x.dtype)], axis=0)
    return sq.T[:, :length]


def _gdn_features(conv_scr, convw_ref, i, row0):
    base = GDN_HIST - (CONV_W - 1) + row0
    parts = []
    for h in range(CONV_DIM // GDN_DK):
        cols = slice(h * GDN_DK, (h + 1) * GDN_DK)
        x = conv_scr[i, pl.ds(base, CHUNK), cols] * convw_ref[0:1, cols]
        for j in range(1, CONV_W):
            x = x + conv_scr[i, pl.ds(base + j, CHUNK), cols] * convw_ref[j:j + 1, cols]
        x = jax.nn.silu(x)
        if h < 2 * H_GDN:
            x = x * lax.rsqrt(jnp.sum(x * x, axis=-1, keepdims=True) + L2_EPS)
        parts.append(x * (GDN_DK ** -0.5) if h < H_GDN else x)
    return jnp.concatenate(parts, axis=1)


def _gdn_block_stages(first_block, rows, pad_front, features, ab_ref, z_ref, write_o, m_scr,
                      alog_ref, dtb_ref, gn_ref):
    bg = ab_ref.shape[0]
    length = CHUNK
    heads = H_GDN
    width = heads * length
    row = lax.broadcasted_iota(jnp.int32, (length, LANES), 0)
    lane = lax.broadcasted_iota(jnp.int32, (length, width), 1)
    lane_head = lane // length
    lane_col = lane % length
    row_p = lax.broadcasted_iota(jnp.int32, (length, width), 0)
    causal = lane_col <= row_p
    strict = lane_col < row_p

    def spread(cols):
        out = jnp.broadcast_to(cols[-1], (length, width))
        for h in range(heads - 2, -1, -1):
            out = jnp.where(lane_head == h, cols[h], out)
        return out

    def pick(mats):
        out = mats[-1]
        for h in range(heads - 2, -1, -1):
            out = jnp.where(lane_head == h, mats[h], out)
        return out

    def diag_blocks(p):
        return jnp.concatenate([jnp.where(lane_head == h, p, 0.0) for h in range(heads)], axis=0).astype(BF16)

    def diag_wide(mats):
        zero = jnp.zeros_like(mats[0])
        return jnp.concatenate(
            [jnp.concatenate([mats[h] if g == h else zero for g in range(heads)], axis=1) for h in range(heads)],
            axis=0).astype(BF16)

    def per_batch(i, row0):
        u = features(i, row0)
        ab = ab_ref[i, row0:row0 + length, :]
        g_all = -jnp.exp(alog_ref[...]) * jax.nn.softplus(ab + dtb_ref[...])
        beta_all = jax.nn.sigmoid(ab)
        if pad_front > row0:
            valid = jnp.logical_or(jnp.logical_not(first_block), row >= pad_front - row0)
            g_all = jnp.where(valid, g_all, 0.0)
            beta_all = jnp.where(valid, beta_all, 0.0)
        gc_all = _cumsum_rows(g_all)
        gc_rows = jnp.concatenate([gc_all, jnp.zeros((LANES - length, LANES), F32)], axis=0).T
        gc_rows = gc_rows[AB_LANE:AB_LANE + SUBLANES, :]
        gc_rows_hi = pltpu.roll(gc_rows, length, 1)
        gc_row = jnp.concatenate([gc_rows[h:h + 1] + gc_rows_hi[h + 1:h + 2] for h in range(0, heads, 2)], axis=1)

        hd = []
        for h in range(heads):
            q = u[:, h * GDN_DK:(h + 1) * GDN_DK]
            k = u[:, (heads + h) * GDN_DK:(heads + h + 1) * GDN_DK]
            v = u[:, 2 * heads * GDN_DK + h * GDN_DV:2 * heads * GDN_DK + (h + 1) * GDN_DV]
            gc = gc_all[:, AB_LANE + h:AB_LANE + h + 1]
            beta = beta_all[:, AB_LANE + heads + h:AB_LANE + heads + h + 1]
            kq = jnp.concatenate([k, q], axis=0).astype(BF16)
            hd.append(dict(h=h, k=k, v=v, gc=gc, beta=beta, kq=kq))
        gc_col = spread([c['gc'] for c in hd])
        decay = jnp.where(causal, jnp.exp(jnp.where(causal, gc_col - gc_row, 0.0)), 0.0)
        k_all = jnp.concatenate([c['k'] for c in hd], axis=0).astype(BF16)
        return dict(i=i, row0=row0, heads=hd, decay=decay, beta=spread([c['beta'] for c in hd]), k_all=k_all)

    per_chunk = []
    for row0 in range(0, rows, length):
        per_chunk.append([])
        for i in range(bg):
            per_chunk[-1].append(per_batch(i, row0))
            yield GDN_PREP_COST
    groups = [g for chunk in per_chunk for g in chunk]
    for g in groups:
        scores = [lax.dot_general(c['kq'], g['k_all'], NT_DIMS, preferred_element_type=F32) for c in g['heads']]
        g['kk'] = pick([s[:length] for s in scores])
        g['qk'] = pick([s[length:] for s in scores])
    yield 1.0
    for g in groups:
        g['a'] = jnp.where(strict, g['beta'] * g['kk'] * g['decay'], 0.0)
        g['qk_decay'] = g['qk'] * g['decay']
        g['y'] = -g['a']
    for g in groups:
        g['pw'] = jnp.dot(g['a'].astype(BF16), diag_blocks(g['a']), preferred_element_type=F32)
    yield 1.0
    span = 2
    while span < length:
        span *= 2
        for g in groups:
            blocks = diag_blocks(g['pw'])
            if span < length:
                prod = jnp.dot(jnp.concatenate([g['y'], g['pw']], axis=0).astype(BF16), blocks,
                               preferred_element_type=F32)
                g['y'] = g['y'] + g['pw'] + prod[:length]
                g['pw'] = prod[length:]
            else:
                g['y'] = g['y'] + g['pw'] + jnp.dot(g['y'].astype(BF16), blocks, preferred_element_type=F32)
        yield 1.0

    def emit_output(chunk):
        for g in chunk:
            prod = jnp.dot(g['qk_decay'].astype(BF16), diag_wide([c['uu'] for c in g['heads']]),
                           preferred_element_type=F32)
            rws = slice(g['row0'], g['row0'] + length)
            for c in g['heads']:
                cols = slice(c['h'] * GDN_DV, (c['h'] + 1) * GDN_DV)
                o = c['eg'] * c['kqm'][length:] + prod[:, cols]
                n = _rms(o, gn_ref[...])
                write_o(g['i'], rws, cols, (n * jax.nn.silu(z_ref[g['i'], rws, cols])).astype(BF16))

    pending = None
    for chunk in per_chunk:
        for g in chunk:
            for c in g['heads']:
                c['m0'] = m_scr[g['i'], c['h']]
                c['kqm'] = jnp.dot(c['kq'], c['m0'].astype(BF16), preferred_element_type=F32)
        yield 1.0
        if pending is not None:
            emit_output(pending)
            yield 1.0
        for g in chunk:
            for c in g['heads']:
                c['eg'] = jnp.exp(c['gc'])
                c['rhs'] = c['beta'] * (c['v'] - c['eg'] * c['kqm'][:length])
            prod = jnp.dot(g['y'].astype(BF16), diag_wide([c['rhs'] for c in g['heads']]),
                           preferred_element_type=F32)
            for c in g['heads']:
                c['uu'] = c['rhs'] + prod[:, c['h'] * GDN_DV:(c['h'] + 1) * GDN_DV]
        yield 1.0
        for g in chunk:
            for c in g['heads']:
                g_last = c['gc'][length - 1:length, :]
                k_dec = c['k'] * jnp.exp(g_last - c['gc'])
                m_scr[g['i'], c['h']] = jnp.exp(g_last) * c['m0'] + _bdot(_transpose_rows(k_dec), c['uu'])
        yield 1.0
        pending = chunk
    emit_output(pending)
    yield 1.0


def _gdn_stage_block(qkv_ref, conv_scr):
    bg, rows, _ = qkv_ref.shape
    for i in range(bg):
        conv_scr[i, GDN_HIST:GDN_HIST + rows, :] = qkv_ref[i]


def _gdn_keep_history(conv_scr, rows):
    for i in range(conv_scr.shape[0]):
        conv_scr[i, 0:GDN_HIST, :] = conv_scr[i, rows:rows + GDN_HIST, :]


def _gdn_kernel(qkv_ref, z_ref, ab_ref, m0_ref, conv0_ref, convw_ref, alog_ref, dtb_ref, gn_ref,
                o_ref, mout_ref, conv_scr, m_scr, *, pad_front, prepared):
    c = pl.program_id(1)
    rows = qkv_ref.shape[1]

    @pl.when(c == 0)
    def _():
        m_scr[...] = m0_ref[...]
        conv_scr[:, 0:GDN_HIST, :] = conv0_ref[...]

    def write_o(i, rws, cols, val):
        o_ref[i, rws, cols] = val

    if prepared:
        features = lambda i, row0: qkv_ref[i, row0:row0 + CHUNK, :]
    else:
        _gdn_stage_block(qkv_ref, conv_scr)
        features = functools.partial(_gdn_features, conv_scr, convw_ref)
    _drain(_gdn_block_stages(c == 0, rows, pad_front, features, ab_ref, z_ref, write_o, m_scr,
                             alog_ref, dtb_ref, gn_ref))
    if not prepared:
        _gdn_keep_history(conv_scr, rows)

    @pl.when(c == pl.num_programs(1) - 1)
    def _():
        mout_ref[...] = m_scr[...]


def _gdn_in_specs(bg, rows, blk):
    nd = len(blk(0, 0))
    const = lambda shape: pl.BlockSpec(shape, lambda *_: (0,) * len(shape))
    return [
        pl.BlockSpec((bg, rows, CONV_DIM), blk),
        pl.BlockSpec((bg, rows, H_GDN * GDN_DV), blk),
        pl.BlockSpec((bg, rows, LANES), lambda *a: blk(*a)[:nd - 1] + (MLA_IN // LANES - 1,)),
        pl.BlockSpec((bg, H_GDN, GDN_DK, GDN_DV), lambda g, c: (g, 0, 0, 0)),
        pl.BlockSpec((bg, GDN_HIST, CONV_DIM), lambda g, c: (g, 0, 0)),
        const((CONV_W, CONV_DIM)),
        const((1, LANES)),
        const((1, LANES)),
        const((1, GDN_DV)),
    ]


def gdn(qkv, z, mla_in, m0, conv0, w, *, pad_front=0, bg=GDN_GROUP, rows=CHUNK, prepared=False):
    b, t, _ = qkv.shape
    bg = min(bg, b)
    assert b % bg == 0 and t % rows == 0 and rows % CHUNK == 0
    blk = lambda g, c: (g, c, 0)
    return pl.pallas_call(
        functools.partial(_gdn_kernel, pad_front=pad_front, prepared=prepared),
        grid=(b // bg, t // rows),
        in_specs=_gdn_in_specs(bg, rows, blk),
        out_specs=[
            pl.BlockSpec((bg, rows, H_GDN * GDN_DV), blk),
            pl.BlockSpec((bg, H_GDN, GDN_DK, GDN_DV), lambda g, c: (g, 0, 0, 0)),
        ],
        out_shape=[jax.ShapeDtypeStruct((b, t, H_GDN * GDN_DV), BF16),
                   jax.ShapeDtypeStruct((b, H_GDN, GDN_DK, GDN_DV), F32)],
        scratch_shapes=[pltpu.VMEM((bg, GDN_HIST + (0 if prepared else rows), CONV_DIM), F32),
                        pltpu.VMEM((bg, H_GDN, GDN_DK, GDN_DV), F32)],
        compiler_params=_params("parallel", "arbitrary"),
        name="gdn",
    )(qkv, z, mla_in, m0, conv0, w['conv_w'], w['alog'], w['dtb'], w['gn'])


def _rope(t, cos, sin, duplicated=False):
    half = DR // 2
    if duplicated:
        swapped = pltpu.roll(t, half, 1)
    else:
        lane = lax.broadcasted_iota(jnp.int32, t.shape, 1)
        swapped = jnp.where(lane < half, pltpu.roll(t, LANES - half, 1), pltpu.roll(t, half, 1))
    return t * cos + swapped * sin


def _mla_prep_kernel(x_ref, cos_ref, sin_ref, gq_ref, gkv_ref, wuq_ref, wuk_ref, wuvt_ref, *out_refs, expand):
    q_ref, c_ref, kr_ref = out_refs[:3]
    x = x_ref[...]
    cos = cos_ref[...]
    sin = sin_ref[...]
    q = _bdot(_rms(x[:, :Q_LORA], gq_ref[...]), wuq_ref[...])
    for h in range(H_MLA):
        lo = h * QK_PAD
        q_ref[:, lo:lo + DN] = (q[:, lo:lo + DN] * Q_SCALE).astype(BF16)
        q_ref[:, lo + DN:lo + QK_PAD] = (_rope(q[:, lo + DN:lo + QK_PAD], cos, sin, True) * Q_SCALE).astype(BF16)
    c = _rms(x[:, Q_LORA:Q_LORA + KV_LORA], gkv_ref[...])
    c_ref[...] = c
    kr = _rope(x[:, Q_LORA + KV_LORA:], cos, sin)
    if not expand:
        kr_ref[...] = kr
    else:
        kr_ref[...] = (kr.T if kr.shape[0] % LANES == 0 else _transpose_rows(kr))[:DR]
        k_ref, vt_ref = out_refs[3:]
        c16 = c.astype(BF16)
        k_nope = jnp.dot(c16, wuk_ref[...], preferred_element_type=F32)
        for h in range(H_MLA):
            k_ref[:, h * QK_PAD:h * QK_PAD + DN] = k_nope[:, h * DN:(h + 1) * DN].astype(BF16)
            k_ref[:, h * QK_PAD + DN:(h + 1) * QK_PAD] = kr.astype(BF16)
        vt_ref[...] = lax.dot_general(wuvt_ref[...], c16, NT_DIMS, preferred_element_type=F32).astype(BF16)


def mla_prep(mla_in, cos, sin, w, *, expand, tm=MLA_PREP_TILE):
    n = mla_in.shape[0]
    tm = _row_tile(min(n, cos.shape[0]), tm)
    nrep = cos.shape[0] // tm
    row = lambda width: pl.BlockSpec((tm, width), lambda i: (i, 0))
    tab = pl.BlockSpec((tm, LANES), lambda i: (i % nrep, 0))
    consts = (w['gq'], w['gkv'], w['w_uq'], w['w_uk'], w['w_uv_t'])
    out_specs = [row(H_MLA * QK_PAD), row(KV_LORA), row(LANES)]
    out_shape = [jax.ShapeDtypeStruct((n, H_MLA * QK_PAD), BF16), jax.ShapeDtypeStruct((n, KV_LORA), F32),
                 jax.ShapeDtypeStruct((n, LANES), F32)]
    if expand:
        out_specs[2] = pl.BlockSpec((DR, tm), lambda i: (0, i))
        out_shape[2] = jax.ShapeDtypeStruct((DR, n), F32)
        out_specs += [row(H_MLA * QK_PAD), pl.BlockSpec((H_MLA * DV_MLA, tm), lambda i: (0, i))]
        out_shape += [jax.ShapeDtypeStruct((n, H_MLA * QK_PAD), BF16), jax.ShapeDtypeStruct((H_MLA * DV_MLA, n), BF16)]
    return pl.pallas_call(
        functools.partial(_mla_prep_kernel, expand=expand),
        grid=(n // tm,),
        in_specs=[row(MLA_IN), tab, tab] + [_const_spec(c.shape) for c in consts],
        out_specs=out_specs,
        out_shape=out_shape,
        compiler_params=_params("parallel"),
        name="mla_prep_kv" if expand else "mla_prep",
    )(mla_in, cos, sin, *consts)


def _mla_prompt_kernel(q_ref, k_ref, vt_ref, km_ref, vmt_ref, o_ref, s_scr, m_scr, l_scr, acc_scr, *, tq):
    step = pl.program_id(1)
    nq = q_ref.shape[0] // tq
    chains = [(a, h) for a in range(nq) for h in range(H_MLA)]
    first = step * nq

    def q_of(c):
        a, h = chains[c]
        return q_ref[a * tq:(a + 1) * tq, h * QK_PAD:(h + 1) * QK_PAD]

    def scores(tile, c):
        h = chains[c][1]
        off = pl.multiple_of(tile * tq, tq)
        return lax.dot_general(k_ref[pl.ds(off, tq), h * QK_PAD:(h + 1) * QK_PAD], q_of(c), NT_DIMS,
                               preferred_element_type=F32)

    def consume(c, tile, masked):
        h = chains[c][1]
        off = pl.multiple_of(tile * tq, tq)
        s_t = s_scr[c]
        if masked:
            key_chunk = lax.broadcasted_iota(jnp.int32, (tq, tq), 0) // CHUNK
            qry_chunk = lax.broadcasted_iota(jnp.int32, (tq, tq), 1) // CHUNK
            s_t = jnp.where(key_chunk <= qry_chunk, s_t, -jnp.inf)
        m = m_scr[c]
        m_new = jnp.maximum(m, jnp.max(s_t, axis=0, keepdims=True))
        alpha = jnp.exp2(m - m_new)
        p_t = jnp.exp2(s_t - m_new)
        m_scr[c] = m_new
        l_scr[c] = alpha * l_scr[c] + jnp.sum(p_t, axis=0, keepdims=True)
        acc_scr[c] = alpha * acc_scr[c] + jnp.dot(vt_ref[h * DV_MLA:(h + 1) * DV_MLA, pl.ds(off, tq)],
                                                  p_t.astype(BF16), preferred_element_type=F32)

    def fold(tile, live, masked_tile):
        for c, (a, h) in enumerate(chains):
            if a < live:
                continue
            if a == masked_tile:
                consume(c, tile, True)
            else:
                s_next = scores(tile + 1, c)
                consume(c, tile, False)
                s_scr[c] = s_next

    meta_s = [lax.dot_general(km_ref[:, h * QK_PAD:(h + 1) * QK_PAD], q_of(c), NT_DIMS,
                              preferred_element_type=F32) for c, (a, h) in enumerate(chains)]
    for c in range(len(chains)):
        s_scr[c] = scores(0, c)
    meta_p = []
    for c in range(len(chains)):
        m = jnp.max(meta_s[c], axis=0, keepdims=True)
        p_t = jnp.exp2(meta_s[c] - m)
        m_scr[c] = m
        l_scr[c] = jnp.sum(p_t, axis=0, keepdims=True)
        meta_p.append(p_t.astype(BF16))
    for c, (a, h) in enumerate(chains):
        acc_scr[c] = jnp.dot(vmt_ref[h * DV_MLA:(h + 1) * DV_MLA, :], meta_p[c], preferred_element_type=F32)

    def trip(t, carry):
        for u in range(nq):
            fold(t * nq + u, 0, None)
        return carry

    lax.fori_loop(0, step, trip, 0)
    for a in range(nq):
        fold(first + a, a, a)
    for c, (a, h) in enumerate(chains):
        o_ref[a * tq:(a + 1) * tq, h * DV_MLA:(h + 1) * DV_MLA] = (acc_scr[c] / l_scr[c]).T.astype(o_ref.dtype)


def mla_prompt(q, k, v_t, k_meta, v_meta_t, tq=ATTN_TILE, nq=ATTN_TILES_PER_STEP):
    b, s, _ = q.shape
    tq = _row_tile(s, tq)
    nq = min(nq, s // tq)
    assert tq % LANES == 0 and s % (nq * tq) == 0
    rows = nq * tq
    nchain = nq * H_MLA
    whole = lambda shape: pl.BlockSpec(shape, lambda bi, i: (0, 0))
    return pl.pallas_call(
        functools.partial(_mla_prompt_kernel, tq=tq),
        grid=(b, s // rows),
        in_specs=[
            pl.BlockSpec((None, rows, H_MLA * QK_PAD), lambda bi, i: (bi, i, 0)),
            pl.BlockSpec((None, s, H_MLA * QK_PAD), lambda bi, i: (bi, 0, 0), pipeline_mode=pl.Buffered(1)),
            pl.BlockSpec((H_MLA * DV_MLA, s), lambda bi, i: (0, bi), pipeline_mode=pl.Buffered(1)),
            whole(k_meta.shape),
            whole(v_meta_t.shape),
        ],
        out_specs=pl.BlockSpec((None, rows, H_MLA * DV_MLA), lambda bi, i: (bi, i, 0)),
        out_shape=jax.ShapeDtypeStruct((b, s, H_MLA * DV_MLA), BF16),
        scratch_shapes=[pltpu.VMEM((nchain, tq, tq), F32), pltpu.VMEM((nchain, 1, tq), F32),
                        pltpu.VMEM((nchain, 1, tq), F32), pltpu.VMEM((nchain, DV_MLA, tq), F32)],
        compiler_params=_params("parallel", "arbitrary"),
        name="mla_prompt",
    )(q, k, v_t, k_meta, v_meta_t)


def _mla_sample_kernel(q_ref, c_ref, kr_ref, cc_ref, ckr_ref, wukv_ref, o_ref):
    nreq, t, _ = q_ref.shape
    w = wukv_ref[...]
    reqs = []
    for r in range(nreq):
        q = q_ref[r]
        qa = jnp.concatenate(
            [lax.dot_general(q[:, h * QK_PAD:h * QK_PAD + DN], w[:, h * (DN + DV_MLA):h * (DN + DV_MLA) + DN],
                             NT_DIMS, preferred_element_type=F32) for h in range(H_MLA)], axis=0).astype(BF16)
        qr = jnp.concatenate([q[:, h * QK_PAD + DN:(h + 1) * QK_PAD] for h in range(H_MLA)], axis=0)
        reqs.append(dict(qa=qa, qr=qr, cache_c=cc_ref[r].astype(BF16), own_c=c_ref[r].astype(BF16)))
    for r, rq in enumerate(reqs):
        rq['s_cache'] = (lax.dot_general(rq['qa'], rq['cache_c'], NT_DIMS, preferred_element_type=F32)
                         + jnp.dot(rq['qr'][:, :DR], ckr_ref[r].astype(BF16), preferred_element_type=F32))
        rq['s_own'] = (lax.dot_general(rq['qa'], rq['own_c'], NT_DIMS, preferred_element_type=F32)
                       + lax.dot_general(rq['qr'], kr_ref[r].astype(BF16), NT_DIMS, preferred_element_type=F32))
    for rq in reqs:
        m = jnp.maximum(jnp.max(rq['s_cache'], axis=-1, keepdims=True), jnp.max(rq['s_own'], axis=-1, keepdims=True))
        p_cache = jnp.exp2(rq['s_cache'] - m)
        p_own = jnp.exp2(rq['s_own'] - m)
        rq['l'] = jnp.sum(p_cache, axis=-1, keepdims=True) + jnp.sum(p_own, axis=-1, keepdims=True)
        rq['p_cache'], rq['p_own'] = p_cache.astype(BF16), p_own.astype(BF16)
    for rq in reqs:
        rq['pc'] = (jnp.dot(rq['p_cache'], rq['cache_c'], preferred_element_type=F32)
                    + jnp.dot(rq['p_own'], rq['own_c'], preferred_element_type=F32)) / rq['l']
    for r, rq in enumerate(reqs):
        for h in range(H_MLA):
            lo = h * (DN + DV_MLA) + DN
            o_ref[r, :, h * DV_MLA:(h + 1) * DV_MLA] = _bdot(rq['pc'][h * t:(h + 1) * t],
                                                             w[:, lo:lo + DV_MLA]).astype(o_ref.dtype)


def mla_sample(q, c, kr, cache_c, cache_kr, w_ukv, group=MLA_SAMPLE_GROUP):
    b, t, _ = q.shape
    p = cache_c.shape[1]
    group = min(group, b)
    assert b % group == 0
    per_b = lambda rows, width: pl.BlockSpec((group, rows, width), lambda bi: (bi, 0, 0))
    return pl.pallas_call(
        _mla_sample_kernel,
        grid=(b // group,),
        in_specs=[per_b(t, H_MLA * QK_PAD), per_b(t, KV_LORA), per_b(t, LANES), per_b(p, KV_LORA), per_b(DR, p),
                  _const_spec(w_ukv.shape)],
        out_specs=per_b(t, H_MLA * DV_MLA),
        out_shape=jax.ShapeDtypeStruct((b, t, H_MLA * DV_MLA), BF16),
        compiler_params=_params("parallel"),
        name="mla_sample",
    )(q, c, kr, cache_c, cache_kr, w_ukv)


def _rope_tables(first, count, repeat=1):
    inv = ROPE_BASE ** (-np.arange(0, DR, 2, dtype=np.float64) / DR)
    ang = (first + np.arange(count, dtype=np.float64))[:, None] * inv[None, :]
    cos, sin = np.cos(ang), np.sin(ang)
    pad = np.zeros((count, LANES - DR))
    table = lambda parts: jnp.asarray(np.tile(np.concatenate(parts, axis=1), (repeat, 1)), F32)
    return table([cos, cos, pad]), table([-sin, sin, pad])


def _prepare_weights(ffn1_norm, ffn1_wg, ffn1_wu, ffn1_wd, mix_norm, w_in, conv_w, a_log, dt_bias, gdn_norm,
                     q_norm, kv_norm, w_uq, w_ukv, w_out, ffn2_norm, ffn2_wg, ffn2_wu, ffn2_wd, final_norm):
    w = {}
    w['g1'], w['gm'], w['g2'] = ffn1_norm[0][None], mix_norm[0][None], ffn2_norm[0][None]
    w['gf'] = final_norm[None]
    assert D_FF % FF_CHUNK == 0
    w['wg1'], w['wu1'], w['wd1'] = (m[0].astype(BF16) for m in (ffn1_wg, ffn1_wu, ffn1_wd))
    w['wg2'], w['wu2'], w['wd2'] = (m[0].astype(BF16) for m in (ffn2_wg, ffn2_wu, ffn2_wd))
    o_z = CONV_DIM
    o_a = o_z + H_GDN * GDN_DV
    o_b = o_a + H_GDN
    o_cq = o_b + H_GDN
    o_kr = o_cq + Q_LORA + KV_LORA
    wi = w_in[0]
    w['w_qkv'] = wi[:, :o_z].astype(BF16)
    w['w_z'] = wi[:, o_z:o_a].astype(BF16)
    tail_pad = jnp.zeros((D_MODEL, LANES - DR - 2 * H_GDN), wi.dtype)
    w['w_mla'] = jnp.concatenate([wi[:, o_cq:o_kr], wi[:, o_kr:o_kr + DR], wi[:, o_a:o_cq], tail_pad], axis=1).astype(BF16)
    w['conv_w'] = conv_w[0]
    lane_vec = lambda v: jnp.zeros((1, LANES), F32).at[0, AB_LANE:AB_LANE + H_GDN].set(v.astype(F32))
    w['alog'], w['dtb'] = lane_vec(a_log[0]), lane_vec(dt_bias[0])
    w['gn'] = gdn_norm[0][None]
    w['gq'], w['gkv'] = q_norm[0][None], kv_norm[0][None]
    uq = w_uq[0].reshape(Q_LORA, H_MLA, DN + DR)
    assert QK_PAD - DN - DR == DR
    uq = jnp.concatenate([uq, uq[:, :, DN:]], axis=-1)
    w['w_uq'] = uq.reshape(Q_LORA, H_MLA * QK_PAD).astype(BF16)
    w['w_ukv'] = w_ukv[0].astype(BF16)
    ukv = w['w_ukv'].reshape(KV_LORA, H_MLA, DN + DV_MLA)
    w['w_uk'] = ukv[:, :, :DN].reshape(KV_LORA, H_MLA * DN)
    w['w_uv_t'] = ukv[:, :, DN:].reshape(KV_LORA, H_MLA * DV_MLA).T
    w['w_out_g'] = w_out[0][:H_GDN * GDN_DV].astype(BF16)
    w['w_out_m'] = w_out[0][H_GDN * GDN_DV:].astype(BF16)
    return w


def kernel(x_prompt, x_sample, cache_mla_ckv, cache_mla_krope, state_gdn, state_conv, meta, ffn1_norm, ffn1_wg,
           ffn1_wu, ffn1_wd, mix_norm, w_in, conv_w, a_log, dt_bias, gdn_norm, q_norm, kv_norm, w_uq, w_ukv, w_out,
           ffn2_norm, ffn2_wg, ffn2_wu, ffn2_wd, final_norm):
    assert ffn1_wg.shape[0] == 1, "one layer: the meta rows are not carried past the mixer"
    bsz, s_len, _ = x_prompt.shape
    dbs, d_seq, _ = x_sample.shape
    past = cache_mla_ckv.shape[2]
    w = _prepare_weights(ffn1_norm, ffn1_wg, ffn1_wu, ffn1_wd, mix_norm, w_in, conv_w, a_log, dt_bias, gdn_norm,
                         q_norm, kv_norm, w_uq, w_ukv, w_out, ffn2_norm, ffn2_wg, ffn2_wu, ffn2_wd, final_norm)

    front = CHUNK - N_META
    _, qkv_m, _, mla_m = ffn_proj(meta.astype(F32), w)
    meta_qkv = jnp.pad(qkv_m, ((front, 0), (0, 0)))[None]
    x1_p, feat_p, z_p, mla_p, tail_p = ffn_proj(x_prompt.reshape(bsz * s_len, D_MODEL), w,
                                                 hist=meta_qkv[0, CHUNK - GDN_HIST:], stream_len=s_len)
    x1_s, qkv_s, z_s, mla_s = ffn_proj(x_sample.reshape(dbs * d_seq, D_MODEL), w)

    _, m_meta = gdn(meta_qkv, jnp.zeros((1, CHUNK, H_GDN * GDN_DV), F32), jnp.pad(mla_m, ((front, 0), (0, 0)))[None],
                    jnp.zeros((1, H_GDN, GDN_DK, GDN_DV), F32), jnp.zeros((1, GDN_HIST, CONV_DIM), F32), w,
                    pad_front=front)
    m0_p = jnp.broadcast_to(m_meta, (bsz,) + m_meta.shape[1:])
    gdn_p, m_p = gdn(feat_p.reshape(bsz, s_len, CONV_DIM), z_p.reshape(bsz, s_len, -1),
                     mla_p.reshape(bsz, s_len, MLA_IN), m0_p, jnp.zeros((bsz, GDN_HIST, CONV_DIM), F32), w,
                     rows=GDN_PROMPT_ROWS, prepared=True)
    conv_p = tail_p.reshape(bsz, -1, GDN_HIST, CONV_DIM)[:, -1, GDN_HIST - (CONV_W - 1):]
    qkv_s3 = qkv_s.reshape(dbs, d_seq, CONV_DIM)
    conv0_s = jnp.pad(state_conv[0].astype(F32), ((0, 0), (GDN_HIST - (CONV_W - 1), 0), (0, 0)))
    gdn_s, m_s = gdn(qkv_s3, z_s.reshape(dbs, d_seq, -1), mla_s.reshape(dbs, d_seq, MLA_IN),
                     state_gdn[0].astype(F32), conv0_s, w, bg=GDN_SAMPLE_GROUP)

    cos_m, sin_m = _rope_tables(0, N_META)
    cos_p, sin_p = _rope_tables(N_META, s_len)
    cos_s, sin_s = _rope_tables(past, d_seq, repeat=min(dbs, MLA_PREP_TILE // d_seq))
    q_p, c_p, kr_p, k_p, vt_p = mla_prep(mla_p, cos_p, sin_p, w, expand=True)
    _, c_m, kr_m, k_m, vt_m = mla_prep(mla_m, cos_m, sin_m, w, expand=True)
    q_s, c_s, kr_s = mla_prep(mla_s, cos_s, sin_s, w, expand=False)
    mla_o_p = mla_prompt(q_p.reshape(bsz, s_len, -1), k_p.reshape(bsz, s_len, -1), vt_p, k_m, vt_m)
    mla_o_s = mla_sample(q_s.reshape(dbs, d_seq, -1), c_s.reshape(dbs, d_seq, -1), kr_s.reshape(dbs, d_seq, -1),
                         cache_mla_ckv[0].astype(F32), jnp.swapaxes(cache_mla_krope[0].astype(F32), 1, 2), w['w_ukv'])

    y_p = out_ffn(x1_p, gdn_p.reshape(bsz * s_len, -1), mla_o_p.reshape(bsz * s_len, -1), w)
    y_s = out_ffn(x1_s, gdn_s.reshape(dbs * d_seq, -1), mla_o_s.reshape(dbs * d_seq, -1), w)

    with_meta = lambda m_rows, rows: jnp.concatenate(
        [jnp.broadcast_to(m_rows[None], (bsz,) + m_rows.shape), rows.reshape(bsz, s_len, -1)], axis=1)
    krope_t = jnp.concatenate([jnp.broadcast_to(kr_m[None], (bsz, DR, N_META)),
                               kr_p.reshape(DR, bsz, s_len).transpose(1, 0, 2)], axis=2)
    return (y_p.reshape(bsz, s_len, D_MODEL), y_s.reshape(dbs, d_seq, D_MODEL),
            with_meta(c_m, c_p)[None], jnp.swapaxes(krope_t, 1, 2)[None],
            m_p[None], conv_p[None],
            c_s.reshape(dbs, d_seq, KV_LORA)[None], kr_s[:, :DR].reshape(dbs, d_seq, DR)[None],
            m_s[None], qkv_s3[:, d_seq - (CONV_W - 1):][None])
```

```python
import functools

import jax
import jax.numpy as jnp
import numpy as np
from jax import lax
from jax.experimental import pallas as pl
from jax.experimental.pallas import tpu as pltpu

F32 = jnp.float32
BF16 = jnp.bfloat16

D_MODEL = 1024
CHUNK = 64
N_META = 16
H_GDN = 4
GDN_DK = 128
GDN_DV = 128
CONV_W = 4
CONV_DIM = H_GDN * (2 * GDN_DK + GDN_DV)
H_MLA = 4
Q_LORA = 384
KV_LORA = 256
DN = 128
DR = 64
DV_MLA = 128
ROPE_BASE = 10000.0
SM_SCALE = (DN + DR) ** -0.5
LOG2_E = 1.4426950408889634
Q_SCALE = SM_SCALE * LOG2_E
D_FF = 2816
EPS = 1e-6
L2_EPS = 1e-6

LANES = 128
SUBLANES = 8
FF_CHUNK = 256
MLA_IN = Q_LORA + KV_LORA + LANES
QK_PAD = 2 * LANES
AB_LANE = DR
VMEM_LIMIT = 56 * 1024 * 1024

TOKEN_TILE = 512
MLA_PREP_TILE = 1024
ATTN_TILE = 512
ATTN_TILES_PER_STEP = 2
MLA_SAMPLE_GROUP = 4
GDN_GROUP = 4
GDN_SAMPLE_GROUP = 8
GDN_PROMPT_ROWS = 2 * CHUNK
GDN_HIST = SUBLANES
GDN_PREP_COST = 4.0

NT_DIMS = (((1,), (1,)), ((), ()))


def _rms(x, g):
    return x * lax.rsqrt(jnp.mean(x * x, axis=-1, keepdims=True) + EPS) * g


def _bdot(a, b):
    return jnp.dot(a.astype(BF16), b.astype(BF16), preferred_element_type=F32)


def _const_spec(shape):
    nd = len(shape)
    return pl.BlockSpec(shape, lambda *_: (0,) * nd, pipeline_mode=pl.Buffered(1))


def _params(*sem):
    return pltpu.CompilerParams(dimension_semantics=sem, vmem_limit_bytes=VMEM_LIMIT)


def _drain(*stage_gens):
    live = [[g, 0.0, float(t)] for g, t in (sg if isinstance(sg, tuple) else (sg, 1.0) for sg in stage_gens)]
    while live:
        entry = min(live, key=lambda e: e[1] / e[2])
        try:
            entry[1] += next(entry[0]) or 0.0
        except StopIteration:
            live.remove(entry)


def _swiglu_half_stages(x, g_ref, wg_ref, wu_ref, wd_ref, h_ref, acc_ref, result):
    h_ref[...] = _rms(x, g_ref[...]).astype(BF16)
    nf = wg_ref.shape[1] // FF_CHUNK

    def gate_up(f):
        cols = slice(f * FF_CHUNK, (f + 1) * FF_CHUNK)
        gate = jnp.dot(h_ref[...], wg_ref[:, cols], preferred_element_type=F32)
        up = jnp.dot(h_ref[...], wu_ref[:, cols], preferred_element_type=F32)
        return gate, up

    nxt = gate_up(0)
    yield 1.0
    for f in range(nf):
        gate, up = nxt
        if f + 1 < nf:
            nxt = gate_up(f + 1)
            yield 1.0
        act = (jax.nn.silu(gate) * up).astype(BF16)
        down = jnp.dot(act, wd_ref[f * FF_CHUNK:(f + 1) * FF_CHUNK, :], preferred_element_type=F32)
        if f == 0:
            acc_ref[...] = down
        else:
            acc_ref[...] += down
        yield 1.0
    result.append(x + 0.5 * acc_ref[...])


def _swiglu_half(x, g_ref, wg_ref, wu_ref, wd_ref, h_ref, acc_ref):
    result = []
    _drain(_swiglu_half_stages(x, g_ref, wg_ref, wu_ref, wd_ref, h_ref, acc_ref, result))
    return result[0]


def _ffn_proj_kernel(x_ref, g1_ref, wg_ref, wu_ref, wd_ref, gm_ref, wqkv_ref, wz_ref, wmla_ref, *rest,
                     stream_tiles):
    if not stream_tiles:
        x1_ref, qkv_ref, z_ref, mla_ref, h_ref, acc_ref = rest
        x1 = _swiglu_half(x_ref[...], g1_ref, wg_ref, wu_ref, wd_ref, h_ref, acc_ref)
    else:
        hist_ref, convw_ref, x1_ref, qkv_ref, z_ref, mla_ref, tail_ref, h_ref, acc_ref, conv_scr = rest
        i = pl.program_id(0)
        tm = x_ref.shape[0]

        @pl.when(i == 0)
        def _():
            conv_scr[...] = jnp.zeros_like(conv_scr)

        @pl.when((i + stream_tiles - 1) % stream_tiles == 0)
        def _():
            conv_scr[0, 0:GDN_HIST, :] = hist_ref[...]

        def feature_stages():
            for row0 in range(0, tm, CHUNK):
                qkv_ref[row0:row0 + CHUNK, :] = _gdn_features(conv_scr, convw_ref, 0, row0)
                yield 1.0
            conv_scr[0, 0:GDN_HIST, :] = conv_scr[0, tm:tm + GDN_HIST, :]

        result = []
        _drain((_swiglu_half_stages(x_ref[...], g1_ref, wg_ref, wu_ref, wd_ref, h_ref, acc_ref, result),
                2 * (D_FF // FF_CHUNK)), (feature_stages(), tm // CHUNK))
        x1 = result[0]
    x1_ref[...] = x1
    h_ref[...] = _rms(x1, gm_ref[...]).astype(BF16)
    qkv = jnp.dot(h_ref[...], wqkv_ref[...], preferred_element_type=F32)
    z_ref[...] = jnp.dot(h_ref[...], wz_ref[...], preferred_element_type=F32)
    mla_ref[...] = jnp.dot(h_ref[...], wmla_ref[...], preferred_element_type=F32)
    if stream_tiles:
        conv_scr[0, GDN_HIST:GDN_HIST + tm, :] = qkv
        tail_ref[0] = qkv[tm - GDN_HIST:, :]
    else:
        qkv_ref[...] = qkv


def _row_tile(n, want):
    t = min(want, n)
    assert n % t == 0, (n, t)
    return t


def ffn_proj(x, w, hist=None, stream_len=None, tm=TOKEN_TILE):
    n = x.shape[0]
    tm = _row_tile(n, tm)
    ntile = n // tm
    stream_tiles = 0
    if hist is not None:
        assert stream_len % tm == 0 and tm % CHUNK == 0
        stream_tiles = stream_len // tm
    tile = (lambda i: jnp.minimum(i, ntile - 1)) if stream_tiles else (lambda i: i)
    row = lambda width: pl.BlockSpec((tm, width), lambda i: (tile(i), 0))
    consts = (w['g1'], w['wg1'], w['wu1'], w['wd1'], w['gm'], w['w_qkv'], w['w_z'], w['w_mla'])
    out_specs = [row(D_MODEL), row(CONV_DIM), row(H_GDN * GDN_DV), row(MLA_IN)]
    out_shape = [jax.ShapeDtypeStruct((n, D_MODEL), F32), jax.ShapeDtypeStruct((n, CONV_DIM), F32),
                 jax.ShapeDtypeStruct((n, H_GDN * GDN_DV), F32), jax.ShapeDtypeStruct((n, MLA_IN), F32)]
    scratch = [pltpu.VMEM((tm, D_MODEL), BF16), pltpu.VMEM((tm, D_MODEL), F32)]
    if stream_tiles:
        consts += (hist, w['conv_w'])
        out_specs[1] = pl.BlockSpec((tm, CONV_DIM), lambda i: (jnp.maximum(i - 1, 0), 0))
        out_specs.append(pl.BlockSpec((1, GDN_HIST, CONV_DIM), lambda i: (tile(i), 0, 0)))
        out_shape.append(jax.ShapeDtypeStruct((ntile, GDN_HIST, CONV_DIM), F32))
        scratch.append(pltpu.VMEM((1, GDN_HIST + tm, CONV_DIM), F32))
    return pl.pallas_call(
        functools.partial(_ffn_proj_kernel, stream_tiles=stream_tiles),
        grid=(ntile + (1 if stream_tiles else 0),),
        in_specs=[row(D_MODEL)] + [_const_spec(c.shape) for c in consts],
        out_specs=out_specs,
        out_shape=out_shape,
        scratch_shapes=scratch,
        compiler_params=_params("arbitrary" if stream_tiles else "parallel"),
        name="ffn_proj",
    )(x, *consts)


def _out_ffn_kernel(x1_ref, gdn_ref, mla_ref, wog_ref, wom_ref, g2_ref, wg_ref, wu_ref, wd_ref, gf_ref,
                    y_ref, h_ref, acc_ref):
    x2 = (x1_ref[...] + jnp.dot(gdn_ref[...], wog_ref[...], preferred_element_type=F32)
          + jnp.dot(mla_ref[...], wom_ref[...], preferred_element_type=F32))
    x3 = _swiglu_half(x2, g2_ref, wg_ref, wu_ref, wd_ref, h_ref, acc_ref)
    y_ref[...] = _rms(x3, gf_ref[...])


def out_ffn(x1, gdn, mla, w, tm=TOKEN_TILE):
    n = x1.shape[0]
    tm = _row_tile(n, tm)
    row = lambda width: pl.BlockSpec((tm, width), lambda i: (i, 0))
    consts = (w['w_out_g'], w['w_out_m'], w['g2'], w['wg2'], w['wu2'], w['wd2'], w['gf'])
    return pl.pallas_call(
        _out_ffn_kernel,
        grid=(n // tm,),
        in_specs=[row(D_MODEL), row(H_GDN * GDN_DV), row(H_MLA * DV_MLA)] + [_const_spec(c.shape) for c in consts],
        out_specs=row(D_MODEL),
        out_shape=jax.ShapeDtypeStruct((n, D_MODEL), F32),
        scratch_shapes=[pltpu.VMEM((tm, D_MODEL), BF16), pltpu.VMEM((tm, D_MODEL), F32)],
        compiler_params=_params("parallel"),
        name="out_ffn",
    )(x1, gdn, mla, *consts)


def _cumsum_rows(x):
    n = x.shape[0]
    row = lax.broadcasted_iota(jnp.int32, x.shape, 0)
    shift = 1
    while shift < n:
        x = x + jnp.where(row >= shift, pltpu.roll(x, shift, 0), 0.0)
        shift *= 2
    return x


def _transpose_rows(x):
    length = x.shape[0]
    sq = jnp.concatenate([x, jnp.zeros((LANES - length, LANES), x.dtype)], axis=0)
    return sq.T[:, :length]


def _gdn_features(conv_scr, convw_ref, i, row0):
    base = GDN_HIST - (CONV_W - 1) + row0
    parts = []
    for h in range(CONV_DIM // GDN_DK):
        cols = slice(h * GDN_DK, (h + 1) * GDN_DK)
        x = conv_scr[i, pl.ds(base, CHUNK), cols] * convw_ref[0:1, cols]
        for j in range(1, CONV_W):
            x = x + conv_scr[i, pl.ds(base + j, CHUNK), cols] * convw_ref[j:j + 1, cols]
        x = jax.nn.silu(x)
        if h < 2 * H_GDN:
            x = x * lax.rsqrt(jnp.sum(x * x, axis=-1, keepdims=True) + L2_EPS)
        parts.append(x * (GDN_DK ** -0.5) if h < H_GDN else x)
    return jnp.concatenate(parts, axis=1)


def _gdn_block_stages(first_block, rows, pad_front, features, ab_ref, z_ref, write_o, m_scr,
                      alog_ref, dtb_ref, gn_ref):
    bg = ab_ref.shape[0]
    length = CHUNK
    heads = H_GDN
    width = heads * length
    row = lax.broadcasted_iota(jnp.int32, (length, LANES), 0)
    lane = lax.broadcasted_iota(jnp.int32, (length, width), 1)
    lane_head = lane // length
    lane_col = lane % length
    row_p = lax.broadcasted_iota(jnp.int32, (length, width), 0)
    causal = lane_col <= row_p
    strict = lane_col < row_p

    def spread(cols):
        out = jnp.broadcast_to(cols[-1], (length, width))
        for h in range(heads - 2, -1, -1):
            out = jnp.where(lane_head == h, cols[h], out)
        return out

    def pick(mats):
        out = mats[-1]
        for h in range(heads - 2, -1, -1):
            out = jnp.where(lane_head == h, mats[h], out)
        return out

    def diag_blocks(p):
        return jnp.concatenate([jnp.where(lane_head == h, p, 0.0) for h in range(heads)], axis=0).astype(BF16)

    def diag_wide(mats):
        zero = jnp.zeros_like(mats[0])
        return jnp.concatenate(
            [jnp.concatenate([mats[h] if g == h else zero for g in range(heads)], axis=1) for h in range(heads)],
            axis=0).astype(BF16)

    def per_batch(i, row0):
        u = features(i, row0)
        ab = ab_ref[i, row0:row0 + length, :]
        g_all = -jnp.exp(alog_ref[...]) * jax.nn.softplus(ab + dtb_ref[...])
        beta_all = jax.nn.sigmoid(ab)
        if pad_front > row0:
            valid = jnp.logical_or(jnp.logical_not(first_block), row >= pad_front - row0)
            g_all = jnp.where(valid, g_all, 0.0)
            beta_all = jnp.where(valid, beta_all, 0.0)
        gc_all = _cumsum_rows(g_all)
        gc_rows = jnp.concatenate([gc_all, jnp.zeros((LANES - length, LANES), F32)], axis=0).T
        gc_rows = gc_rows[AB_LANE:AB_LANE + SUBLANES, :]
        gc_rows_hi = pltpu.roll(gc_rows, length, 1)
        gc_row = jnp.concatenate([gc_rows[h:h + 1] + gc_rows_hi[h + 1:h + 2] for h in range(0, heads, 2)], axis=1)

        hd = []
        for h in range(heads):
            q = u[:, h * GDN_DK:(h + 1) * GDN_DK]
            k = u[:, (heads + h) * GDN_DK:(heads + h + 1) * GDN_DK]
            v = u[:, 2 * heads * GDN_DK + h * GDN_DV:2 * heads * GDN_DK + (h + 1) * GDN_DV]
            gc = gc_all[:, AB_LANE + h:AB_LANE + h + 1]
            beta = beta_all[:, AB_LANE + heads + h:AB_LANE + heads + h + 1]
            kq = jnp.concatenate([k, q], axis=0).astype(BF16)
            hd.append(dict(h=h, k=k, v=v, gc=gc, beta=beta, kq=kq))
        gc_col = spread([c['gc'] for c in hd])
        decay = jnp.where(causal, jnp.exp(jnp.where(causal, gc_col - gc_row, 0.0)), 0.0)
        k_all = jnp.concatenate([c['k'] for c in hd], axis=0).astype(BF16)
        return dict(i=i, row0=row0, heads=hd, decay=decay, beta=spread([c['beta'] for c in hd]), k_all=k_all)

    per_chunk = []
    for row0 in range(0, rows, length):
        per_chunk.append([])
        for i in range(bg):
            per_chunk[-1].append(per_batch(i, row0))
            yield GDN_PREP_COST
    groups = [g for chunk in per_chunk for g in chunk]
    for g in groups:
        scores = [lax.dot_general(c['kq'], g['k_all'], NT_DIMS, preferred_element_type=F32) for c in g['heads']]
        g['kk'] = pick([s[:length] for s in scores])
        g['qk'] = pick([s[length:] for s in scores])
    yield 1.0
    for g in groups:
        g['a'] = jnp.where(strict, g['beta'] * g['kk'] * g['decay'], 0.0)
        g['qk_decay'] = g['qk'] * g['decay']
        g['y'] = -g['a']
    for g in groups:
        g['pw'] = jnp.dot(g['a'].astype(BF16), diag_blocks(g['a']), preferred_element_type=F32)
    yield 1.0
    span = 2
    while span < length:
        span *= 2
        for g in groups:
            blocks = diag_blocks(g['pw'])
            if span < length:
                prod = jnp.dot(jnp.concatenate([g['y'], g['pw']], axis=0).astype(BF16), blocks,
                               preferred_element_type=F32)
                g['y'] = g['y'] + g['pw'] + prod[:length]
                g['pw'] = prod[length:]
            else:
                g['y'] = g['y'] + g['pw'] + jnp.dot(g['y'].astype(BF16), blocks, preferred_element_type=F32)
        yield 1.0

    def emit_output(chunk):
        for g in chunk:
            prod = jnp.dot(g['qk_decay'].astype(BF16), diag_wide([c['uu'] for c in g['heads']]),
                           preferred_element_type=F32)
            rws = slice(g['row0'], g['row0'] + length)
            for c in g['heads']:
                cols = slice(c['h'] * GDN_DV, (c['h'] + 1) * GDN_DV)
                o = c['eg'] * c['kqm'][length:] + prod[:, cols]
                n = _rms(o, gn_ref[...])
                write_o(g['i'], rws, cols, (n * jax.nn.silu(z_ref[g['i'], rws, cols])).astype(BF16))

    pending = None
    for chunk in per_chunk:
        for g in chunk:
            for c in g['heads']:
                c['m0'] = m_scr[g['i'], c['h']]
                c['kqm'] = jnp.dot(c['kq'], c['m0'].astype(BF16), preferred_element_type=F32)
        yield 1.0
        if pending is not None:
            emit_output(pending)
            yield 1.0
        for g in chunk:
            for c in g['heads']:
                c['eg'] = jnp.exp(c['gc'])
                c['rhs'] = c['beta'] * (c['v'] - c['eg'] * c['kqm'][:length])
            prod = jnp.dot(g['y'].astype(BF16), diag_wide([c['rhs'] for c in g['heads']]),
                           preferred_element_type=F32)
            for c in g['heads']:
                c['uu'] = c['rhs'] + prod[:, c['h'] * GDN_DV:(c['h'] + 1) * GDN_DV]
        yield 1.0
        for g in chunk:
            for c in g['heads']:
                g_last = c['gc'][length - 1:length, :]
                k_dec = c['k'] * jnp.exp(g_last - c['gc'])
                m_scr[g['i'], c['h']] = jnp.exp(g_last) * c['m0'] + _bdot(_transpose_rows(k_dec), c['uu'])
        yield 1.0
        pending = chunk
    emit_output(pending)
    yield 1.0


def _gdn_stage_block(qkv_ref, conv_scr):
    bg, rows, _ = qkv_ref.shape
    for i in range(bg):
        conv_scr[i, GDN_HIST:GDN_HIST + rows, :] = qkv_ref[i]


def _gdn_keep_history(conv_scr, rows):
    for i in range(conv_scr.shape[0]):
        conv_scr[i, 0:GDN_HIST, :] = conv_scr[i, rows:rows + GDN_HIST, :]


def _gdn_kernel(qkv_ref, z_ref, ab_ref, m0_ref, conv0_ref, convw_ref, alog_ref, dtb_ref, gn_ref,
                o_ref, mout_ref, conv_scr, m_scr, *, pad_front, prepared):
    c = pl.program_id(1)
    rows = qkv_ref.shape[1]

    @pl.when(c == 0)
    def _():
        m_scr[...] = m0_ref[...]
        conv_scr[:, 0:GDN_HIST, :] = conv0_ref[...]

    def write_o(i, rws, cols, val):
        o_ref[i, rws, cols] = val

    if prepared:
        features = lambda i, row0: qkv_ref[i, row0:row0 + CHUNK, :]
    else:
        _gdn_stage_block(qkv_ref, conv_scr)
        features = functools.partial(_gdn_features, conv_scr, convw_ref)
    _drain(_gdn_block_stages(c == 0, rows, pad_front, features, ab_ref, z_ref, write_o, m_scr,
                             alog_ref, dtb_ref, gn_ref))
    if not prepared:
        _gdn_keep_history(conv_scr, rows)

    @pl.when(c == pl.num_programs(1) - 1)
    def _():
        mout_ref[...] = m_scr[...]


def _gdn_in_specs(bg, rows, blk):
    nd = len(blk(0, 0))
    const = lambda shape: pl.BlockSpec(shape, lambda *_: (0,) * len(shape))
    return [
        pl.BlockSpec((bg, rows, CONV_DIM), blk),
        pl.BlockSpec((bg, rows, H_GDN * GDN_DV), blk),
        pl.BlockSpec((bg, rows, LANES), lambda *a: blk(*a)[:nd - 1] + (MLA_IN // LANES - 1,)),
        pl.BlockSpec((bg, H_GDN, GDN_DK, GDN_DV), lambda g, c: (g, 0, 0, 0)),
        pl.BlockSpec((bg, GDN_HIST, CONV_DIM), lambda g, c: (g, 0, 0)),
        const((CONV_W, CONV_DIM)),
        const((1, LANES)),
        const((1, LANES)),
        const((1, GDN_DV)),
    ]


def gdn(qkv, z, mla_in, m0, conv0, w, *, pad_front=0, bg=GDN_GROUP, rows=CHUNK, prepared=False):
    b, t, _ = qkv.shape
    bg = min(bg, b)
    assert b % bg == 0 and t % rows == 0 and rows % CHUNK == 0
    blk = lambda g, c: (g, c, 0)
    return pl.pallas_call(
        functools.partial(_gdn_kernel, pad_front=pad_front, prepared=prepared),
        grid=(b // bg, t // rows),
        in_specs=_gdn_in_specs(bg, rows, blk),
        out_specs=[
            pl.BlockSpec((bg, rows, H_GDN * GDN_DV), blk),
            pl.BlockSpec((bg, H_GDN, GDN_DK, GDN_DV), lambda g, c: (g, 0, 0, 0)),
        ],
        out_shape=[jax.ShapeDtypeStruct((b, t, H_GDN * GDN_DV), BF16),
                   jax.ShapeDtypeStruct((b, H_GDN, GDN_DK, GDN_DV), F32)],
        scratch_shapes=[pltpu.VMEM((bg, GDN_HIST + (0 if prepared else rows), CONV_DIM), F32),
                        pltpu.VMEM((bg, H_GDN, GDN_DK, GDN_DV), F32)],
        compiler_params=_params("parallel", "arbitrary"),
        name="gdn",
    )(qkv, z, mla_in, m0, conv0, w['conv_w'], w['alog'], w['dtb'], w['gn'])


def _rope(t, cos, sin, duplicated=False):
    half = DR // 2
    if duplicated:
        swapped = pltpu.roll(t, half, 1)
    else:
        lane = lax.broadcasted_iota(jnp.int32, t.shape, 1)
        swapped = jnp.where(lane < half, pltpu.roll(t, LANES - half, 1), pltpu.roll(t, half, 1))
    return t * cos + swapped * sin


def _mla_prep_kernel(x_ref, cos_ref, sin_ref, gq_ref, gkv_ref, wuq_ref, wuk_ref, wuvt_ref, *out_refs, expand):
    q_ref, c_ref, kr_ref = out_refs[:3]
    x = x_ref[...]
    cos = cos_ref[...]
    sin = sin_ref[...]
    q = _bdot(_rms(x[:, :Q_LORA], gq_ref[...]), wuq_ref[...])
    for h in range(H_MLA):
        lo = h * QK_PAD
        q_ref[:, lo:lo + DN] = (q[:, lo:lo + DN] * Q_SCALE).astype(BF16)
        q_ref[:, lo + DN:lo + QK_PAD] = (_rope(q[:, lo + DN:lo + QK_PAD], cos, sin, True) * Q_SCALE).astype(BF16)
    c = _rms(x[:, Q_LORA:Q_LORA + KV_LORA], gkv_ref[...])
    c_ref[...] = c.reshape(c_ref.shape)
    kr = _rope(x[:, Q_LORA + KV_LORA:], cos, sin)
    if not expand:
        kr_ref[...] = kr
    else:
        kr_ref[...] = (kr.T if kr.shape[0] % LANES == 0 else _transpose_rows(kr))[:DR]
        k_ref, vt_ref = out_refs[3:]
        c16 = c.astype(BF16)
        k_nope = jnp.dot(c16, wuk_ref[...], preferred_element_type=F32)
        for h in range(H_MLA):
            k_ref[:, h * QK_PAD:h * QK_PAD + DN] = k_nope[:, h * DN:(h + 1) * DN].astype(BF16)
            k_ref[:, h * QK_PAD + DN:(h + 1) * QK_PAD] = kr.astype(BF16)
        vt_ref[...] = lax.dot_general(wuvt_ref[...], c16, NT_DIMS, preferred_element_type=F32).astype(BF16)


def mla_prep(mla_in, cos, sin, w, *, expand, cache_rows=None, tm=MLA_PREP_TILE):
    n = mla_in.shape[0]
    tm = _row_tile(min(n, cos.shape[0]), tm)
    nrep = cos.shape[0] // tm
    row = lambda width: pl.BlockSpec((tm, width), lambda i: (i, 0))
    tab = pl.BlockSpec((tm, LANES), lambda i: (i % nrep, 0))
    consts = (w['gq'], w['gkv'], w['w_uq'], w['w_uk'], w['w_uv_t'])
    out_specs = [row(H_MLA * QK_PAD), row(KV_LORA), row(LANES)]
    out_shape = [jax.ShapeDtypeStruct((n, H_MLA * QK_PAD), BF16), jax.ShapeDtypeStruct((n, KV_LORA), F32),
                 jax.ShapeDtypeStruct((n, LANES), F32)]
    if expand:
        out_specs[2] = pl.BlockSpec((DR, tm), lambda i: (0, i))
        out_shape[2] = jax.ShapeDtypeStruct((DR, n), F32)
        out_specs += [row(H_MLA * QK_PAD), pl.BlockSpec((H_MLA * DV_MLA, tm), lambda i: (0, i))]
        out_shape += [jax.ShapeDtypeStruct((n, H_MLA * QK_PAD), BF16), jax.ShapeDtypeStruct((H_MLA * DV_MLA, n), BF16)]
    if cache_rows is not None:
        front, slen = cache_rows
        assert slen % tm == 0 and n % slen == 0 and front % SUBLANES == 0
        tps = slen // tm
        out_specs[1] = pl.BlockSpec((pl.Element(1), pl.Element(tm), pl.Element(KV_LORA)),
                                    lambda i: (i // tps, pl.multiple_of(front + (i % tps) * tm, SUBLANES), 0))
        out_shape[1] = jax.ShapeDtypeStruct((n // slen, front + slen, KV_LORA), F32)
    return pl.pallas_call(
        functools.partial(_mla_prep_kernel, expand=expand),
        grid=(n // tm,),
        in_specs=[row(MLA_IN), tab, tab] + [_const_spec(c.shape) for c in consts],
        out_specs=out_specs,
        out_shape=out_shape,
        compiler_params=_params("parallel"),
        name="mla_prep_kv" if expand else "mla_prep",
    )(mla_in, cos, sin, *consts)


def _mla_prompt_kernel(q_ref, k_ref, vt_ref, km_ref, vmt_ref, o_ref, s_scr, m_scr, l_scr, acc_scr, *, tq):
    step = pl.program_id(1)
    nq = q_ref.shape[0] // tq
    chains = [(a, h) for a in range(nq) for h in range(H_MLA)]
    first = step * nq

    def q_of(c):
        a, h = chains[c]
        return q_ref[a * tq:(a + 1) * tq, h * QK_PAD:(h + 1) * QK_PAD]

    def scores(tile, c):
        h = chains[c][1]
        off = pl.multiple_of(tile * tq, tq)
        return lax.dot_general(k_ref[pl.ds(off, tq), h * QK_PAD:(h + 1) * QK_PAD], q_of(c), NT_DIMS,
                               preferred_element_type=F32)

    def consume(c, tile, masked):
        h = chains[c][1]
        off = pl.multiple_of(tile * tq, tq)
        s_t = s_scr[c]
        if masked:
            key_chunk = lax.broadcasted_iota(jnp.int32, (tq, tq), 0) // CHUNK
            qry_chunk = lax.broadcasted_iota(jnp.int32, (tq, tq), 1) // CHUNK
            s_t = jnp.where(key_chunk <= qry_chunk, s_t, -jnp.inf)
        m = m_scr[c]
        m_new = jnp.maximum(m, jnp.max(s_t, axis=0, keepdims=True))
        alpha = jnp.exp2(m - m_new)
        p_t = jnp.exp2(s_t - m_new)
        m_scr[c] = m_new
        l_scr[c] = alpha * l_scr[c] + jnp.sum(p_t, axis=0, keepdims=True)
        acc_scr[c] = alpha * acc_scr[c] + jnp.dot(vt_ref[h * DV_MLA:(h + 1) * DV_MLA, pl.ds(off, tq)],
                                                  p_t.astype(BF16), preferred_element_type=F32)

    def fold(tile, live, masked_tile):
        for c, (a, h) in enumerate(chains):
            if a < live:
                continue
            if a == masked_tile:
                consume(c, tile, True)
            else:
                s_next = scores(tile + 1, c)
                consume(c, tile, False)
                s_scr[c] = s_next

    meta_s = [lax.dot_general(km_ref[:, h * QK_PAD:(h + 1) * QK_PAD], q_of(c), NT_DIMS,
                              preferred_element_type=F32) for c, (a, h) in enumerate(chains)]
    for c in range(len(chains)):
        s_scr[c] = scores(0, c)
    meta_p = []
    for c in range(len(chains)):
        m = jnp.max(meta_s[c], axis=0, keepdims=True)
        p_t = jnp.exp2(meta_s[c] - m)
        m_scr[c] = m
        l_scr[c] = jnp.sum(p_t, axis=0, keepdims=True)
        meta_p.append(p_t.astype(BF16))
    for c, (a, h) in enumerate(chains):
        acc_scr[c] = jnp.dot(vmt_ref[h * DV_MLA:(h + 1) * DV_MLA, :], meta_p[c], preferred_element_type=F32)

    def trip(t, carry):
        for u in range(nq):
            fold(t * nq + u, 0, None)
        return carry

    lax.fori_loop(0, step, trip, 0)
    for a in range(nq):
        fold(first + a, a, a)
    for c, (a, h) in enumerate(chains):
        o_ref[a * tq:(a + 1) * tq, h * DV_MLA:(h + 1) * DV_MLA] = (acc_scr[c] / l_scr[c]).T.astype(o_ref.dtype)


def mla_prompt(q, k, v_t, k_meta, v_meta_t, tq=ATTN_TILE, nq=ATTN_TILES_PER_STEP):
    b, s, _ = q.shape
    tq = _row_tile(s, tq)
    nq = min(nq, s // tq)
    assert tq % LANES == 0 and s % (nq * tq) == 0
    rows = nq * tq
    nchain = nq * H_MLA
    whole = lambda shape: pl.BlockSpec(shape, lambda bi, i: (0, 0))
    return pl.pallas_call(
        functools.partial(_mla_prompt_kernel, tq=tq),
        grid=(b, s // rows),
        in_specs=[
            pl.BlockSpec((None, rows, H_MLA * QK_PAD), lambda bi, i: (bi, i, 0)),
            pl.BlockSpec((None, s, H_MLA * QK_PAD), lambda bi, i: (bi, 0, 0), pipeline_mode=pl.Buffered(1)),
            pl.BlockSpec((H_MLA * DV_MLA, s), lambda bi, i: (0, bi), pipeline_mode=pl.Buffered(1)),
            whole(k_meta.shape),
            whole(v_meta_t.shape),
        ],
        out_specs=pl.BlockSpec((None, rows, H_MLA * DV_MLA), lambda bi, i: (bi, i, 0)),
        out_shape=jax.ShapeDtypeStruct((b, s, H_MLA * DV_MLA), BF16),
        scratch_shapes=[pltpu.VMEM((nchain, tq, tq), F32), pltpu.VMEM((nchain, 1, tq), F32),
                        pltpu.VMEM((nchain, 1, tq), F32), pltpu.VMEM((nchain, DV_MLA, tq), F32)],
        compiler_params=_params("parallel", "arbitrary"),
        name="mla_prompt",
    )(q, k, v_t, k_meta, v_meta_t)


def _mla_sample_kernel(q_ref, c_ref, kr_ref, cc_ref, ckr_ref, wukv_ref, o_ref):
    nreq, t, _ = q_ref.shape
    w = wukv_ref[...]
    reqs = []
    for r in range(nreq):
        q = q_ref[r]
        qa = jnp.concatenate(
            [lax.dot_general(q[:, h * QK_PAD:h * QK_PAD + DN], w[:, h * (DN + DV_MLA):h * (DN + DV_MLA) + DN],
                             NT_DIMS, preferred_element_type=F32) for h in range(H_MLA)], axis=0).astype(BF16)
        qr = jnp.concatenate([q[:, h * QK_PAD + DN:(h + 1) * QK_PAD] for h in range(H_MLA)], axis=0)
        reqs.append(dict(qa=qa, qr=qr, cache_c=cc_ref[r].astype(BF16), own_c=c_ref[r].astype(BF16)))
    for r, rq in enumerate(reqs):
        rq['s_cache'] = (lax.dot_general(rq['qa'], rq['cache_c'], NT_DIMS, preferred_element_type=F32)
                         + jnp.dot(rq['qr'][:, :DR], ckr_ref[r].astype(BF16), preferred_element_type=F32))
        rq['s_own'] = (lax.dot_general(rq['qa'], rq['own_c'], NT_DIMS, preferred_element_type=F32)
                       + lax.dot_general(rq['qr'], kr_ref[r].astype(BF16), NT_DIMS, preferred_element_type=F32))
    for rq in reqs:
        m = jnp.maximum(jnp.max(rq['s_cache'], axis=-1, keepdims=True), jnp.max(rq['s_own'], axis=-1, keepdims=True))
        p_cache = jnp.exp2(rq['s_cache'] - m)
        p_own = jnp.exp2(rq['s_own'] - m)
        rq['l'] = jnp.sum(p_cache, axis=-1, keepdims=True) + jnp.sum(p_own, axis=-1, keepdims=True)
        rq['p_cache'], rq['p_own'] = p_cache.astype(BF16), p_own.astype(BF16)
    for rq in reqs:
        rq['pc'] = (jnp.dot(rq['p_cache'], rq['cache_c'], preferred_element_type=F32)
                    + jnp.dot(rq['p_own'], rq['own_c'], preferred_element_type=F32)) / rq['l']
    for r, rq in enumerate(reqs):
        for h in range(H_MLA):
            lo = h * (DN + DV_MLA) + DN
            o_ref[r, :, h * DV_MLA:(h + 1) * DV_MLA] = _bdot(rq['pc'][h * t:(h + 1) * t],
                                                             w[:, lo:lo + DV_MLA]).astype(o_ref.dtype)


def mla_sample(q, c, kr, cache_c, cache_kr, w_ukv, group=MLA_SAMPLE_GROUP):
    b, t, _ = q.shape
    p = cache_c.shape[1]
    group = min(group, b)
    assert b % group == 0
    per_b = lambda rows, width: pl.BlockSpec((group, rows, width), lambda bi: (bi, 0, 0))
    return pl.pallas_call(
        _mla_sample_kernel,
        grid=(b // group,),
        in_specs=[per_b(t, H_MLA * QK_PAD), per_b(t, KV_LORA), per_b(t, LANES), per_b(p, KV_LORA), per_b(DR, p),
                  _const_spec(w_ukv.shape)],
        out_specs=per_b(t, H_MLA * DV_MLA),
        out_shape=jax.ShapeDtypeStruct((b, t, H_MLA * DV_MLA), BF16),
        compiler_params=_params("parallel"),
        name="mla_sample",
    )(q, c, kr, cache_c, cache_kr, w_ukv)


def _rope_tables(first, count, repeat=1):
    inv = ROPE_BASE ** (-np.arange(0, DR, 2, dtype=np.float64) / DR)
    ang = (first + np.arange(count, dtype=np.float64))[:, None] * inv[None, :]
    cos, sin = np.cos(ang), np.sin(ang)
    pad = np.zeros((count, LANES - DR))
    table = lambda parts: jnp.asarray(np.tile(np.concatenate(parts, axis=1), (repeat, 1)), F32)
    return table([cos, cos, pad]), table([-sin, sin, pad])


def _prepare_weights(ffn1_norm, ffn1_wg, ffn1_wu, ffn1_wd, mix_norm, w_in, conv_w, a_log, dt_bias, gdn_norm,
                     q_norm, kv_norm, w_uq, w_ukv, w_out, ffn2_norm, ffn2_wg, ffn2_wu, ffn2_wd, final_norm):
    w = {}
    w['g1'], w['gm'], w['g2'] = ffn1_norm[0][None], mix_norm[0][None], ffn2_norm[0][None]
    w['gf'] = final_norm[None]
    assert D_FF % FF_CHUNK == 0
    w['wg1'], w['wu1'], w['wd1'] = (m[0].astype(BF16) for m in (ffn1_wg, ffn1_wu, ffn1_wd))
    w['wg2'], w['wu2'], w['wd2'] = (m[0].astype(BF16) for m in (ffn2_wg, ffn2_wu, ffn2_wd))
    o_z = CONV_DIM
    o_a = o_z + H_GDN * GDN_DV
    o_b = o_a + H_GDN
    o_cq = o_b + H_GDN
    o_kr = o_cq + Q_LORA + KV_LORA
    wi = w_in[0]
    w['w_qkv'] = wi[:, :o_z].astype(BF16)
    w['w_z'] = wi[:, o_z:o_a].astype(BF16)
    tail_pad = jnp.zeros((D_MODEL, LANES - DR - 2 * H_GDN), wi.dtype)
    w['w_mla'] = jnp.concatenate([wi[:, o_cq:o_kr], wi[:, o_kr:o_kr + DR], wi[:, o_a:o_cq], tail_pad], axis=1).astype(BF16)
    w['conv_w'] = conv_w[0]
    lane_vec = lambda v: jnp.zeros((1, LANES), F32).at[0, AB_LANE:AB_LANE + H_GDN].set(v.astype(F32))
    w['alog'], w['dtb'] = lane_vec(a_log[0]), lane_vec(dt_bias[0])
    w['gn'] = gdn_norm[0][None]
    w['gq'], w['gkv'] = q_norm[0][None], kv_norm[0][None]
    uq = w_uq[0].reshape(Q_LORA, H_MLA, DN + DR)
    assert QK_PAD - DN - DR == DR
    uq = jnp.concatenate([uq, uq[:, :, DN:]], axis=-1)
    w['w_uq'] = uq.reshape(Q_LORA, H_MLA * QK_PAD).astype(BF16)
    w['w_ukv'] = w_ukv[0].astype(BF16)
    ukv = w['w_ukv'].reshape(KV_LORA, H_MLA, DN + DV_MLA)
    w['w_uk'] = ukv[:, :, :DN].reshape(KV_LORA, H_MLA * DN)
    w['w_uv_t'] = ukv[:, :, DN:].reshape(KV_LORA, H_MLA * DV_MLA).T
    w['w_out_g'] = w_out[0][:H_GDN * GDN_DV].astype(BF16)
    w['w_out_m'] = w_out[0][H_GDN * GDN_DV:].astype(BF16)
    return w


def kernel(x_prompt, x_sample, cache_mla_ckv, cache_mla_krope, state_gdn, state_conv, meta, ffn1_norm, ffn1_wg,
           ffn1_wu, ffn1_wd, mix_norm, w_in, conv_w, a_log, dt_bias, gdn_norm, q_norm, kv_norm, w_uq, w_ukv, w_out,
           ffn2_norm, ffn2_wg, ffn2_wu, ffn2_wd, final_norm):
    assert ffn1_wg.shape[0] == 1, "one layer: the meta rows are not carried past the mixer"
    bsz, s_len, _ = x_prompt.shape
    dbs, d_seq, _ = x_sample.shape
    past = cache_mla_ckv.shape[2]
    w = _prepare_weights(ffn1_norm, ffn1_wg, ffn1_wu, ffn1_wd, mix_norm, w_in, conv_w, a_log, dt_bias, gdn_norm,
                         q_norm, kv_norm, w_uq, w_ukv, w_out, ffn2_norm, ffn2_wg, ffn2_wu, ffn2_wd, final_norm)

    front = CHUNK - N_META
    _, qkv_m, _, mla_m = ffn_proj(meta.astype(F32), w)
    meta_qkv = jnp.pad(qkv_m, ((front, 0), (0, 0)))[None]
    x1_p, feat_p, z_p, mla_p, tail_p = ffn_proj(x_prompt.reshape(bsz * s_len, D_MODEL), w,
                                                 hist=meta_qkv[0, CHUNK - GDN_HIST:], stream_len=s_len)
    x1_s, qkv_s, z_s, mla_s = ffn_proj(x_sample.reshape(dbs * d_seq, D_MODEL), w)

    _, m_meta = gdn(meta_qkv, jnp.zeros((1, CHUNK, H_GDN * GDN_DV), F32), jnp.pad(mla_m, ((front, 0), (0, 0)))[None],
                    jnp.zeros((1, H_GDN, GDN_DK, GDN_DV), F32), jnp.zeros((1, GDN_HIST, CONV_DIM), F32), w,
                    pad_front=front)
    m0_p = jnp.broadcast_to(m_meta, (bsz,) + m_meta.shape[1:])
    gdn_p, m_p = gdn(feat_p.reshape(bsz, s_len, CONV_DIM), z_p.reshape(bsz, s_len, -1),
                     mla_p.reshape(bsz, s_len, MLA_IN), m0_p, jnp.zeros((bsz, GDN_HIST, CONV_DIM), F32), w,
                     rows=GDN_PROMPT_ROWS, prepared=True)
    conv_p = tail_p.reshape(bsz, -1, GDN_HIST, CONV_DIM)[:, -1, GDN_HIST - (CONV_W - 1):]
    qkv_s3 = qkv_s.reshape(dbs, d_seq, CONV_DIM)
    conv0_s = jnp.pad(state_conv[0].astype(F32), ((0, 0), (GDN_HIST - (CONV_W - 1), 0), (0, 0)))
    gdn_s, m_s = gdn(qkv_s3, z_s.reshape(dbs, d_seq, -1), mla_s.reshape(dbs, d_seq, MLA_IN),
                     state_gdn[0].astype(F32), conv0_s, w, bg=GDN_SAMPLE_GROUP)

    cos_m, sin_m = _rope_tables(0, N_META)
    cos_p, sin_p = _rope_tables(N_META, s_len)
    cos_s, sin_s = _rope_tables(past, d_seq, repeat=min(dbs, MLA_PREP_TILE // d_seq))
    q_p, c_p, kr_p, k_p, vt_p = mla_prep(mla_p, cos_p, sin_p, w, expand=True, cache_rows=(N_META, s_len))
    _, c_m, kr_m, k_m, vt_m = mla_prep(mla_m, cos_m, sin_m, w, expand=True)
    q_s, c_s, kr_s = mla_prep(mla_s, cos_s, sin_s, w, expand=False)
    mla_o_p = mla_prompt(q_p.reshape(bsz, s_len, -1), k_p.reshape(bsz, s_len, -1), vt_p, k_m, vt_m)
    mla_o_s = mla_sample(q_s.reshape(dbs, d_seq, -1), c_s.reshape(dbs, d_seq, -1), kr_s.reshape(dbs, d_seq, -1),
                         cache_mla_ckv[0].astype(F32), jnp.swapaxes(cache_mla_krope[0].astype(F32), 1, 2), w['w_ukv'])

    y_p = out_ffn(x1_p, gdn_p.reshape(bsz * s_len, -1), mla_o_p.reshape(bsz * s_len, -1), w)
    y_s = out_ffn(x1_s, gdn_s.reshape(dbs * d_seq, -1), mla_o_s.reshape(dbs * d_seq, -1), w)

    krope_t = jnp.concatenate([jnp.broadcast_to(kr_m[None], (bsz, DR, N_META)),
                               kr_p.reshape(DR, bsz, s_len).transpose(1, 0, 2)], axis=2)
    return (y_p.reshape(bsz, s_len, D_MODEL), y_s.reshape(dbs, d_seq, D_MODEL),
            c_p.at[:, :N_META].set(jnp.broadcast_to(c_m[None], (bsz,) + c_m.shape))[None],
            jnp.swapaxes(krope_t, 1, 2)[None],
            m_p[None], conv_p[None],
            c_s.reshape(dbs, d_seq, KV_LORA)[None], kr_s[:, :DR].reshape(dbs, d_seq, DR)[None],
            m_s[None], qkv_s3[:, d_seq - (CONV_W - 1):][None])
```

```python
import functools

import jax
import jax.numpy as jnp
import numpy as np
from jax import lax
from jax.experimental import pallas as pl
from jax.experimental.pallas import tpu as pltpu

F32 = jnp.float32
BF16 = jnp.bfloat16

D_MODEL = 1024
CHUNK = 64
N_META = 16
H_GDN = 4
GDN_DK = 128
GDN_DV = 128
CONV_W = 4
CONV_DIM = H_GDN * (2 * GDN_DK + GDN_DV)
H_MLA = 4
Q_LORA = 384
KV_LORA = 256
DN = 128
DR = 64
DV_MLA = 128
ROPE_BASE = 10000.0
SM_SCALE = (DN + DR) ** -0.5
LOG2_E = 1.4426950408889634
Q_SCALE = SM_SCALE * LOG2_E
D_FF = 2816
EPS = 1e-6
L2_EPS = 1e-6

LANES = 128
SUBLANES = 8
FF_CHUNK = 256
MLA_IN = Q_LORA + KV_LORA + LANES
QK_PAD = 2 * LANES
AB_LANE = DR
VMEM_LIMIT = 56 * 1024 * 1024

TOKEN_TILE = 512
MLA_PREP_TILE = 1024
ATTN_TILE = 512
ATTN_TILES_PER_STEP = 2
MLA_SAMPLE_GROUP = 4
GDN_GROUP = 4
GDN_SAMPLE_GROUP = 8
GDN_PROMPT_ROWS = 2 * CHUNK
GDN_HIST = SUBLANES
GDN_PREP_COST = 4.0

NT_DIMS = (((1,), (1,)), ((), ()))


def _rms(x, g):
    return x * lax.rsqrt(jnp.mean(x * x, axis=-1, keepdims=True) + EPS) * g


def _bdot(a, b):
    return jnp.dot(a.astype(BF16), b.astype(BF16), preferred_element_type=F32)


def _const_spec(shape):
    nd = len(shape)
    return pl.BlockSpec(shape, lambda *_: (0,) * nd, pipeline_mode=pl.Buffered(1))


def _params(*sem):
    return pltpu.CompilerParams(dimension_semantics=sem, vmem_limit_bytes=VMEM_LIMIT)


def _drain(*stage_gens):
    live = [[g, 0.0, float(t)] for g, t in (sg if isinstance(sg, tuple) else (sg, 1.0) for sg in stage_gens)]
    while live:
        entry = min(live, key=lambda e: e[1] / e[2])
        try:
            entry[1] += next(entry[0]) or 0.0
        except StopIteration:
            live.remove(entry)


def _swiglu_half_stages(x, g_ref, wg_ref, wu_ref, wd_ref, h_ref, acc_ref, result):
    h_ref[...] = _rms(x, g_ref[...]).astype(BF16)
    nf = wg_ref.shape[1] // FF_CHUNK

    def gate_up(f):
        cols = slice(f * FF_CHUNK, (f + 1) * FF_CHUNK)
        gate = jnp.dot(h_ref[...], wg_ref[:, cols], preferred_element_type=F32)
        up = jnp.dot(h_ref[...], wu_ref[:, cols], preferred_element_type=F32)
        return gate, up

    nxt = gate_up(0)
    yield 1.0
    for f in range(nf):
        gate, up = nxt
        if f + 1 < nf:
            nxt = gate_up(f + 1)
            yield 1.0
        act = (jax.nn.silu(gate) * up).astype(BF16)
        down = jnp.dot(act, wd_ref[f * FF_CHUNK:(f + 1) * FF_CHUNK, :], preferred_element_type=F32)
        if f == 0:
            acc_ref[...] = down
        else:
            acc_ref[...] += down
        yield 1.0
    result.append(x + 0.5 * acc_ref[...])


def _swiglu_half(x, g_ref, wg_ref, wu_ref, wd_ref, h_ref, acc_ref):
    result = []
    _drain(_swiglu_half_stages(x, g_ref, wg_ref, wu_ref, wd_ref, h_ref, acc_ref, result))
    return result[0]


def _ffn_proj_kernel(x_ref, g1_ref, wg_ref, wu_ref, wd_ref, gm_ref, wqkv_ref, wz_ref, wmla_ref, *rest,
                     stream_tiles):
    if not stream_tiles:
        x1_ref, qkv_ref, z_ref, mla_ref, h_ref, acc_ref = rest
        x1 = _swiglu_half(x_ref[...], g1_ref, wg_ref, wu_ref, wd_ref, h_ref, acc_ref)
    else:
        hist_ref, convw_ref, x1_ref, qkv_ref, z_ref, mla_ref, tail_ref, h_ref, acc_ref, conv_scr = rest
        i = pl.program_id(0)
        tm = x_ref.shape[0]

        @pl.when(i == 0)
        def _():
            conv_scr[...] = jnp.zeros_like(conv_scr)

        @pl.when((i + stream_tiles - 1) % stream_tiles == 0)
        def _():
            conv_scr[0, 0:GDN_HIST, :] = hist_ref[...]

        def feature_stages():
            for row0 in range(0, tm, CHUNK):
                qkv_ref[row0:row0 + CHUNK, :] = _gdn_features(conv_scr, convw_ref, 0, row0)
                yield 1.0
            conv_scr[0, 0:GDN_HIST, :] = conv_scr[0, tm:tm + GDN_HIST, :]

        result = []
        _drain((_swiglu_half_stages(x_ref[...], g1_ref, wg_ref, wu_ref, wd_ref, h_ref, acc_ref, result),
                2 * (D_FF // FF_CHUNK)), (feature_stages(), tm // CHUNK))
        x1 = result[0]
    x1_ref[...] = x1
    h_ref[...] = _rms(x1, gm_ref[...]).astype(BF16)
    qkv = jnp.dot(h_ref[...], wqkv_ref[...], preferred_element_type=F32)
    z_ref[...] = jnp.dot(h_ref[...], wz_ref[...], preferred_element_type=F32)
    mla_ref[...] = jnp.dot(h_ref[...], wmla_ref[...], preferred_element_type=F32)
    if stream_tiles:
        conv_scr[0, GDN_HIST:GDN_HIST + tm, :] = qkv
        tail_ref[0] = qkv[tm - GDN_HIST:, :]
    else:
        qkv_ref[...] = qkv


def _row_tile(n, want):
    t = min(want, n)
    assert n % t == 0, (n, t)
    return t


def ffn_proj(x, w, hist=None, stream_len=None, tm=TOKEN_TILE):
    n = x.shape[0]
    tm = _row_tile(n, tm)
    ntile = n // tm
    stream_tiles = 0
    if hist is not None:
        assert stream_len % tm == 0 and tm % CHUNK == 0
        stream_tiles = stream_len // tm
    tile = (lambda i: jnp.minimum(i, ntile - 1)) if stream_tiles else (lambda i: i)
    row = lambda width: pl.BlockSpec((tm, width), lambda i: (tile(i), 0))
    consts = (w['g1'], w['wg1'], w['wu1'], w['wd1'], w['gm'], w['w_qkv'], w['w_z'], w['w_mla'])
    out_specs = [row(D_MODEL), row(CONV_DIM), row(H_GDN * GDN_DV), row(MLA_IN)]
    out_shape = [jax.ShapeDtypeStruct((n, D_MODEL), F32), jax.ShapeDtypeStruct((n, CONV_DIM), F32),
                 jax.ShapeDtypeStruct((n, H_GDN * GDN_DV), F32), jax.ShapeDtypeStruct((n, MLA_IN), F32)]
    scratch = [pltpu.VMEM((tm, D_MODEL), BF16), pltpu.VMEM((tm, D_MODEL), F32)]
    if stream_tiles:
        consts += (hist, w['conv_w'])
        out_specs[1] = pl.BlockSpec((tm, CONV_DIM), lambda i: (jnp.maximum(i - 1, 0), 0))
        out_specs.append(pl.BlockSpec((1, GDN_HIST, CONV_DIM), lambda i: (tile(i), 0, 0)))
        out_shape.append(jax.ShapeDtypeStruct((ntile, GDN_HIST, CONV_DIM), F32))
        scratch.append(pltpu.VMEM((1, GDN_HIST + tm, CONV_DIM), F32))
    return pl.pallas_call(
        functools.partial(_ffn_proj_kernel, stream_tiles=stream_tiles),
        grid=(ntile + (1 if stream_tiles else 0),),
        in_specs=[row(D_MODEL)] + [_const_spec(c.shape) for c in consts],
        out_specs=out_specs,
        out_shape=out_shape,
        scratch_shapes=scratch,
        compiler_params=_params("arbitrary" if stream_tiles else "parallel"),
        name="ffn_proj",
    )(x, *consts)


def _out_ffn_kernel(x1_ref, gdn_ref, mla_ref, wog_ref, wom_ref, g2_ref, wg_ref, wu_ref, wd_ref, gf_ref,
                    y_ref, h_ref, acc_ref):
    x2 = (x1_ref[...] + jnp.dot(gdn_ref[...], wog_ref[...], preferred_element_type=F32)
          + jnp.dot(mla_ref[...], wom_ref[...], preferred_element_type=F32))
    x3 = _swiglu_half(x2, g2_ref, wg_ref, wu_ref, wd_ref, h_ref, acc_ref)
    y_ref[...] = _rms(x3, gf_ref[...])


def out_ffn(x1, gdn, mla, w, tm=TOKEN_TILE):
    n = x1.shape[0]
    tm = _row_tile(n, tm)
    row = lambda width: pl.BlockSpec((tm, width), lambda i: (i, 0))
    consts = (w['w_out_g'], w['w_out_m'], w['g2'], w['wg2'], w['wu2'], w['wd2'], w['gf'])
    return pl.pallas_call(
        _out_ffn_kernel,
        grid=(n // tm,),
        in_specs=[row(D_MODEL), row(H_GDN * GDN_DV), row(H_MLA * DV_MLA)] + [_const_spec(c.shape) for c in consts],
        out_specs=row(D_MODEL),
        out_shape=jax.ShapeDtypeStruct((n, D_MODEL), F32),
        scratch_shapes=[pltpu.VMEM((tm, D_MODEL), BF16), pltpu.VMEM((tm, D_MODEL), F32)],
        compiler_params=_params("parallel"),
        name="out_ffn",
    )(x1, gdn, mla, *consts)


def _cumsum_rows(x):
    n = x.shape[0]
    row = lax.broadcasted_iota(jnp.int32, x.shape, 0)
    shift = 1
    while shift < n:
        x = x + jnp.where(row >= shift, pltpu.roll(x, shift, 0), 0.0)
        shift *= 2
    return x


def _transpose_rows(x):
    length = x.shape[0]
    sq = jnp.concatenate([x, jnp.zeros((LANES - length, LANES), x.dtype)], axis=0)
    return sq.T[:, :length]


def _gdn_features(conv_scr, convw_ref, i, row0):
    base = GDN_HIST - (CONV_W - 1) + row0
    parts = []
    for h in range(CONV_DIM // GDN_DK):
        cols = slice(h * GDN_DK, (h + 1) * GDN_DK)
        x = conv_scr[i, pl.ds(base, CHUNK), cols] * convw_ref[0:1, cols]
        for j in range(1, CONV_W):
            x = x + conv_scr[i, pl.ds(base + j, CHUNK), cols] * convw_ref[j:j + 1, cols]
        x = jax.nn.silu(x)
        if h < 2 * H_GDN:
            x = x * lax.rsqrt(jnp.sum(x * x, axis=-1, keepdims=True) + L2_EPS)
        parts.append(x * (GDN_DK ** -0.5) if h < H_GDN else x)
    return jnp.concatenate(parts, axis=1)


def _gdn_block_stages(first_block, rows, pad_front, features, ab_ref, z_ref, write_o, m_scr,
                      alog_ref, dtb_ref, gn_ref):
    bg = ab_ref.shape[0]
    length = CHUNK
    heads = H_GDN
    width = heads * length
    row = lax.broadcasted_iota(jnp.int32, (length, LANES), 0)
    lane = lax.broadcasted_iota(jnp.int32, (length, width), 1)
    lane_head = lane // length
    lane_col = lane % length
    row_p = lax.broadcasted_iota(jnp.int32, (length, width), 0)
    causal = lane_col <= row_p
    strict = lane_col < row_p

    def spread(cols):
        out = jnp.broadcast_to(cols[-1], (length, width))
        for h in range(heads - 2, -1, -1):
            out = jnp.where(lane_head == h, cols[h], out)
        return out

    def pick(mats):
        out = mats[-1]
        for h in range(heads - 2, -1, -1):
            out = jnp.where(lane_head == h, mats[h], out)
        return out

    def diag_blocks(p):
        return jnp.concatenate([jnp.where(lane_head == h, p, 0.0) for h in range(heads)], axis=0).astype(BF16)

    def diag_wide(mats):
        zero = jnp.zeros_like(mats[0])
        return jnp.concatenate(
            [jnp.concatenate([mats[h] if g == h else zero for g in range(heads)], axis=1) for h in range(heads)],
            axis=0).astype(BF16)

    def per_batch(i, row0):
        u = features(i, row0)
        ab = ab_ref[i, row0:row0 + length, :]
        g_all = -jnp.exp(alog_ref[...]) * jax.nn.softplus(ab + dtb_ref[...])
        beta_all = jax.nn.sigmoid(ab)
        if pad_front > row0:
            valid = jnp.logical_or(jnp.logical_not(first_block), row >= pad_front - row0)
            g_all = jnp.where(valid, g_all, 0.0)
            beta_all = jnp.where(valid, beta_all, 0.0)
        gc_all = _cumsum_rows(g_all)
        gc_rows = jnp.concatenate([gc_all, jnp.zeros((LANES - length, LANES), F32)], axis=0).T
        gc_rows = gc_rows[AB_LANE:AB_LANE + SUBLANES, :]
        gc_rows_hi = pltpu.roll(gc_rows, length, 1)
        gc_row = jnp.concatenate([gc_rows[h:h + 1] + gc_rows_hi[h + 1:h + 2] for h in range(0, heads, 2)], axis=1)

        hd = []
        for h in range(heads):
            q = u[:, h * GDN_DK:(h + 1) * GDN_DK]
            k = u[:, (heads + h) * GDN_DK:(heads + h + 1) * GDN_DK]
            v = u[:, 2 * heads * GDN_DK + h * GDN_DV:2 * heads * GDN_DK + (h + 1) * GDN_DV]
            gc = gc_all[:, AB_LANE + h:AB_LANE + h + 1]
            beta = beta_all[:, AB_LANE + heads + h:AB_LANE + heads + h + 1]
            kq = jnp.concatenate([k, q], axis=0).astype(BF16)
            hd.append(dict(h=h, k=k, v=v, gc=gc, beta=beta, kq=kq))
        gc_col = spread([c['gc'] for c in hd])
        decay = jnp.where(causal, jnp.exp(jnp.where(causal, gc_col - gc_row, 0.0)), 0.0)
        k_all = jnp.concatenate([c['k'] for c in hd], axis=0).astype(BF16)
        return dict(i=i, row0=row0, heads=hd, decay=decay, beta=spread([c['beta'] for c in hd]), k_all=k_all)

    per_chunk = []
    for row0 in range(0, rows, length):
        per_chunk.append([])
        for i in range(bg):
            per_chunk[-1].append(per_batch(i, row0))
            yield GDN_PREP_COST
    groups = [g for chunk in per_chunk for g in chunk]
    for g in groups:
        scores = [lax.dot_general(c['kq'], g['k_all'], NT_DIMS, preferred_element_type=F32) for c in g['heads']]
        g['kk'] = pick([s[:length] for s in scores])
        g['qk'] = pick([s[length:] for s in scores])
    yield 1.0
    for g in groups:
        g['a'] = jnp.where(strict, g['beta'] * g['kk'] * g['decay'], 0.0)
        g['qk_decay'] = g['qk'] * g['decay']
        g['y'] = -g['a']
    for g in groups:
        g['pw'] = jnp.dot(g['a'].astype(BF16), diag_blocks(g['a']), preferred_element_type=F32)
    yield 1.0
    span = 2
    while span < length:
        span *= 2
        for g in groups:
            blocks = diag_blocks(g['pw'])
            if span < length:
                prod = jnp.dot(jnp.concatenate([g['y'], g['pw']], axis=0).astype(BF16), blocks,
                               preferred_element_type=F32)
                g['y'] = g['y'] + g['pw'] + prod[:length]
                g['pw'] = prod[length:]
            else:
                g['y'] = g['y'] + g['pw'] + jnp.dot(g['y'].astype(BF16), blocks, preferred_element_type=F32)
        yield 1.0

    def emit_output(chunk):
        for g in chunk:
            prod = jnp.dot(g['qk_decay'].astype(BF16), diag_wide([c['uu'] for c in g['heads']]),
                           preferred_element_type=F32)
            rws = slice(g['row0'], g['row0'] + length)
            for c in g['heads']:
                cols = slice(c['h'] * GDN_DV, (c['h'] + 1) * GDN_DV)
                o = c['eg'] * c['kqm'][length:] + prod[:, cols]
                n = _rms(o, gn_ref[...])
                write_o(g['i'], rws, cols, (n * jax.nn.silu(z_ref[g['i'], rws, cols])).astype(BF16))

    pending = None
    for chunk in per_chunk:
        for g in chunk:
            for c in g['heads']:
                c['m0'] = m_scr[g['i'], c['h']]
                c['kqm'] = jnp.dot(c['kq'], c['m0'].astype(BF16), preferred_element_type=F32)
        yield 1.0
        if pending is not None:
            emit_output(pending)
            yield 1.0
        for g in chunk:
            for c in g['heads']:
                c['eg'] = jnp.exp(c['gc'])
                c['rhs'] = c['beta'] * (c['v'] - c['eg'] * c['kqm'][:length])
            prod = jnp.dot(g['y'].astype(BF16), diag_wide([c['rhs'] for c in g['heads']]),
                           preferred_element_type=F32)
            for c in g['heads']:
                c['uu'] = c['rhs'] + prod[:, c['h'] * GDN_DV:(c['h'] + 1) * GDN_DV]
        yield 1.0
        for g in chunk:
            for c in g['heads']:
                g_last = c['gc'][length - 1:length, :]
                k_dec = c['k'] * jnp.exp(g_last - c['gc'])
                m_scr[g['i'], c['h']] = jnp.exp(g_last) * c['m0'] + _bdot(_transpose_rows(k_dec), c['uu'])
        yield 1.0
        pending = chunk
    emit_output(pending)
    yield 1.0


def _gdn_stage_block(qkv_ref, conv_scr):
    bg, rows, _ = qkv_ref.shape
    for i in range(bg):
        conv_scr[i, GDN_HIST:GDN_HIST + rows, :] = qkv_ref[i]


def _gdn_keep_history(conv_scr, rows):
    for i in range(conv_scr.shape[0]):
        conv_scr[i, 0:GDN_HIST, :] = conv_scr[i, rows:rows + GDN_HIST, :]


def _gdn_kernel(qkv_ref, z_ref, ab_ref, m0_ref, conv0_ref, convw_ref, alog_ref, dtb_ref, gn_ref,
                o_ref, mout_ref, conv_scr, m_scr, *, pad_front, prepared):
    c = pl.program_id(1)
    rows = qkv_ref.shape[1]

    @pl.when(c == 0)
    def _():
        m_scr[...] = m0_ref[...]
        conv_scr[:, 0:GDN_HIST, :] = conv0_ref[...]

    def write_o(i, rws, cols, val):
        o_ref[i, rws, cols] = val

    if prepared:
        features = lambda i, row0: qkv_ref[i, row0:row0 + CHUNK, :]
    else:
        _gdn_stage_block(qkv_ref, conv_scr)
        features = functools.partial(_gdn_features, conv_scr, convw_ref)
    _drain(_gdn_block_stages(c == 0, rows, pad_front, features, ab_ref, z_ref, write_o, m_scr,
                             alog_ref, dtb_ref, gn_ref))
    if not prepared:
        _gdn_keep_history(conv_scr, rows)

    @pl.when(c == pl.num_programs(1) - 1)
    def _():
        mout_ref[...] = m_scr[...]


def _gdn_in_specs(bg, rows, blk):
    nd = len(blk(0, 0))
    const = lambda shape: pl.BlockSpec(shape, lambda *_: (0,) * len(shape))
    return [
        pl.BlockSpec((bg, rows, CONV_DIM), blk),
        pl.BlockSpec((bg, rows, H_GDN * GDN_DV), blk),
        pl.BlockSpec((bg, rows, LANES), lambda *a: blk(*a)[:nd - 1] + (MLA_IN // LANES - 1,)),
        pl.BlockSpec((bg, H_GDN, GDN_DK, GDN_DV), lambda g, c: (g, 0, 0, 0)),
        pl.BlockSpec((bg, GDN_HIST, CONV_DIM), lambda g, c: (g, 0, 0)),
        const((CONV_W, CONV_DIM)),
        const((1, LANES)),
        const((1, LANES)),
        const((1, GDN_DV)),
    ]


def gdn(qkv, z, mla_in, m0, conv0, w, *, pad_front=0, bg=GDN_GROUP, rows=CHUNK, prepared=False):
    b, t, _ = qkv.shape
    bg = min(bg, b)
    assert b % bg == 0 and t % rows == 0 and rows % CHUNK == 0
    blk = lambda g, c: (g, c, 0)
    return pl.pallas_call(
        functools.partial(_gdn_kernel, pad_front=pad_front, prepared=prepared),
        grid=(b // bg, t // rows),
        in_specs=_gdn_in_specs(bg, rows, blk),
        out_specs=[
            pl.BlockSpec((bg, rows, H_GDN * GDN_DV), blk),
            pl.BlockSpec((bg, H_GDN, GDN_DK, GDN_DV), lambda g, c: (g, 0, 0, 0)),
        ],
        out_shape=[jax.ShapeDtypeStruct((b, t, H_GDN * GDN_DV), BF16),
                   jax.ShapeDtypeStruct((b, H_GDN, GDN_DK, GDN_DV), F32)],
        scratch_shapes=[pltpu.VMEM((bg, GDN_HIST + (0 if prepared else rows), CONV_DIM), F32),
                        pltpu.VMEM((bg, H_GDN, GDN_DK, GDN_DV), F32)],
        compiler_params=_params("parallel", "arbitrary"),
        name="gdn",
    )(qkv, z, mla_in, m0, conv0, w['conv_w'], w['alog'], w['dtb'], w['gn'])


def _rope(t, cos, sin, duplicated=False):
    half = DR // 2
    if duplicated:
        swapped = pltpu.roll(t, half, 1)
    else:
        lane = lax.broadcasted_iota(jnp.int32, t.shape, 1)
        swapped = jnp.where(lane < half, pltpu.roll(t, LANES - half, 1), pltpu.roll(t, half, 1))
    return t * cos + swapped * sin


MLA_PREP_DEPTH = 3


def _mla_prep_kernel(x_hbm, cos_ref, sin_ref, gq_ref, gkv_ref, wuq_ref, wuk_ref, wuvt_ref, *refs, expand, nsteps):
    *out_refs, x_buf, x_sem = refs
    q_ref, c_ref, kr_ref = out_refs[:3]
    i = pl.program_id(0)
    tm = x_buf.shape[1]

    def tile_copy(tile):
        slot = tile % MLA_PREP_DEPTH
        return pltpu.make_async_copy(x_hbm.at[pl.ds(pl.multiple_of(tile * tm, tm), tm), :], x_buf.at[slot],
                                     x_sem.at[slot])

    @pl.when(i == 0)
    def _():
        for t in range(min(MLA_PREP_DEPTH - 1, nsteps)):
            tile_copy(t).start()

    @pl.when(i + MLA_PREP_DEPTH - 1 < nsteps)
    def _():
        tile_copy(i + MLA_PREP_DEPTH - 1).start()

    tile_copy(i).wait()
    x = x_buf[i % MLA_PREP_DEPTH]
    cos = cos_ref[...]
    sin = sin_ref[...]
    q = _bdot(_rms(x[:, :Q_LORA], gq_ref[...]), wuq_ref[...])
    for h in range(H_MLA):
        lo = h * QK_PAD
        q_ref[:, lo:lo + DN] = (q[:, lo:lo + DN] * Q_SCALE).astype(BF16)
        q_ref[:, lo + DN:lo + QK_PAD] = (_rope(q[:, lo + DN:lo + QK_PAD], cos, sin, True) * Q_SCALE).astype(BF16)
    c = _rms(x[:, Q_LORA:Q_LORA + KV_LORA], gkv_ref[...])
    c_ref[...] = c
    kr = _rope(x[:, Q_LORA + KV_LORA:], cos, sin)
    if not expand:
        kr_ref[...] = kr
    else:
        kr_ref[...] = (kr.T if kr.shape[0] % LANES == 0 else _transpose_rows(kr))[:DR]
        k_ref, vt_ref = out_refs[3:]
        c16 = c.astype(BF16)
        k_nope = jnp.dot(c16, wuk_ref[...], preferred_element_type=F32)
        for h in range(H_MLA):
            k_ref[:, h * QK_PAD:h * QK_PAD + DN] = k_nope[:, h * DN:(h + 1) * DN].astype(BF16)
            k_ref[:, h * QK_PAD + DN:(h + 1) * QK_PAD] = kr.astype(BF16)
        vt_ref[...] = lax.dot_general(wuvt_ref[...], c16, NT_DIMS, preferred_element_type=F32).astype(BF16)


def mla_prep(mla_in, cos, sin, w, *, expand, tm=MLA_PREP_TILE):
    n = mla_in.shape[0]
    tm = _row_tile(min(n, cos.shape[0]), tm)
    nrep = cos.shape[0] // tm
    row = lambda width: pl.BlockSpec((tm, width), lambda i: (i, 0))
    tab = pl.BlockSpec((tm, LANES), lambda i: (i % nrep, 0))
    consts = (w['gq'], w['gkv'], w['w_uq'], w['w_uk'], w['w_uv_t'])
    out_specs = [row(H_MLA * QK_PAD), row(KV_LORA), row(LANES)]
    out_shape = [jax.ShapeDtypeStruct((n, H_MLA * QK_PAD), BF16), jax.ShapeDtypeStruct((n, KV_LORA), F32),
                 jax.ShapeDtypeStruct((n, LANES), F32)]
    if expand:
        out_specs[2] = pl.BlockSpec((DR, tm), lambda i: (0, i))
        out_shape[2] = jax.ShapeDtypeStruct((DR, n), F32)
        out_specs += [row(H_MLA * QK_PAD), pl.BlockSpec((H_MLA * DV_MLA, tm), lambda i: (0, i))]
        out_shape += [jax.ShapeDtypeStruct((n, H_MLA * QK_PAD), BF16), jax.ShapeDtypeStruct((H_MLA * DV_MLA, n), BF16)]
    return pl.pallas_call(
        functools.partial(_mla_prep_kernel, expand=expand, nsteps=n // tm),
        grid=(n // tm,),
        in_specs=[pl.BlockSpec(memory_space=pl.ANY), tab, tab] + [_const_spec(c.shape) for c in consts],
        out_specs=out_specs,
        out_shape=out_shape,
        scratch_shapes=[pltpu.VMEM((MLA_PREP_DEPTH, tm, MLA_IN), F32), pltpu.SemaphoreType.DMA((MLA_PREP_DEPTH,))],
        compiler_params=_params("arbitrary"),
        name="mla_prep_kv" if expand else "mla_prep",
    )(mla_in, cos, sin, *consts)


def _mla_prompt_kernel(q_ref, k_ref, vt_ref, km_ref, vmt_ref, o_ref, s_scr, m_scr, l_scr, acc_scr, *, tq):
    step = pl.program_id(1)
    nq = q_ref.shape[0] // tq
    chains = [(a, h) for a in range(nq) for h in range(H_MLA)]
    first = step * nq

    def q_of(c):
        a, h = chains[c]
        return q_ref[a * tq:(a + 1) * tq, h * QK_PAD:(h + 1) * QK_PAD]

    def scores(tile, c):
        h = chains[c][1]
        off = pl.multiple_of(tile * tq, tq)
        return lax.dot_general(k_ref[pl.ds(off, tq), h * QK_PAD:(h + 1) * QK_PAD], q_of(c), NT_DIMS,
                               preferred_element_type=F32)

    def consume(c, tile, masked):
        h = chains[c][1]
        off = pl.multiple_of(tile * tq, tq)
        s_t = s_scr[c]
        if masked:
            key_chunk = lax.broadcasted_iota(jnp.int32, (tq, tq), 0) // CHUNK
            qry_chunk = lax.broadcasted_iota(jnp.int32, (tq, tq), 1) // CHUNK
            s_t = jnp.where(key_chunk <= qry_chunk, s_t, -jnp.inf)
        m = m_scr[c]
        m_new = jnp.maximum(m, jnp.max(s_t, axis=0, keepdims=True))
        alpha = jnp.exp2(m - m_new)
        p_t = jnp.exp2(s_t - m_new)
        m_scr[c] = m_new
        l_scr[c] = alpha * l_scr[c] + jnp.sum(p_t, axis=0, keepdims=True)
        acc_scr[c] = alpha * acc_scr[c] + jnp.dot(vt_ref[h * DV_MLA:(h + 1) * DV_MLA, pl.ds(off, tq)],
                                                  p_t.astype(BF16), preferred_element_type=F32)

    def fold(tile, live, masked_tile):
        for c, (a, h) in enumerate(chains):
            if a < live:
                continue
            if a == masked_tile:
                consume(c, tile, True)
            else:
                s_next = scores(tile + 1, c)
                consume(c, tile, False)
                s_scr[c] = s_next

    meta_s = [lax.dot_general(km_ref[:, h * QK_PAD:(h + 1) * QK_PAD], q_of(c), NT_DIMS,
                              preferred_element_type=F32) for c, (a, h) in enumerate(chains)]
    for c in range(len(chains)):
        s_scr[c] = scores(0, c)
    meta_p = []
    for c in range(len(chains)):
        m = jnp.max(meta_s[c], axis=0, keepdims=True)
        p_t = jnp.exp2(meta_s[c] - m)
        m_scr[c] = m
        l_scr[c] = jnp.sum(p_t, axis=0, keepdims=True)
        meta_p.append(p_t.astype(BF16))
    for c, (a, h) in enumerate(chains):
        acc_scr[c] = jnp.dot(vmt_ref[h * DV_MLA:(h + 1) * DV_MLA, :], meta_p[c], preferred_element_type=F32)

    def trip(t, carry):
        for u in range(nq):
            fold(t * nq + u, 0, None)
        return carry

    lax.fori_loop(0, step, trip, 0)
    for a in range(nq):
        fold(first + a, a, a)
    for c, (a, h) in enumerate(chains):
        o_ref[a * tq:(a + 1) * tq, h * DV_MLA:(h + 1) * DV_MLA] = (acc_scr[c] / l_scr[c]).T.astype(o_ref.dtype)


def mla_prompt(q, k, v_t, k_meta, v_meta_t, tq=ATTN_TILE, nq=ATTN_TILES_PER_STEP):
    b, s, _ = q.shape
    tq = _row_tile(s, tq)
    nq = min(nq, s // tq)
    assert tq % LANES == 0 and s % (nq * tq) == 0
    rows = nq * tq
    nchain = nq * H_MLA
    whole = lambda shape: pl.BlockSpec(shape, lambda bi, i: (0, 0))
    return pl.pallas_call(
        functools.partial(_mla_prompt_kernel, tq=tq),
        grid=(b, s // rows),
        in_specs=[
            pl.BlockSpec((None, rows, H_MLA * QK_PAD), lambda bi, i: (bi, i, 0)),
            pl.BlockSpec((None, s, H_MLA * QK_PAD), lambda bi, i: (bi, 0, 0), pipeline_mode=pl.Buffered(1)),
            pl.BlockSpec((H_MLA * DV_MLA, s), lambda bi, i: (0, bi), pipeline_mode=pl.Buffered(1)),
            whole(k_meta.shape),
            whole(v_meta_t.shape),
        ],
        out_specs=pl.BlockSpec((None, rows, H_MLA * DV_MLA), lambda bi, i: (bi, i, 0)),
        out_shape=jax.ShapeDtypeStruct((b, s, H_MLA * DV_MLA), BF16),
        scratch_shapes=[pltpu.VMEM((nchain, tq, tq), F32), pltpu.VMEM((nchain, 1, tq), F32),
                        pltpu.VMEM((nchain, 1, tq), F32), pltpu.VMEM((nchain, DV_MLA, tq), F32)],
        compiler_params=_params("parallel", "arbitrary"),
        name="mla_prompt",
    )(q, k, v_t, k_meta, v_meta_t)


def _mla_sample_kernel(q_ref, c_ref, kr_ref, cc_ref, ckr_ref, wukv_ref, o_ref):
    nreq, t, _ = q_ref.shape
    w = wukv_ref[...]
    reqs = []
    for r in range(nreq):
        q = q_ref[r]
        qa = jnp.concatenate(
            [lax.dot_general(q[:, h * QK_PAD:h * QK_PAD + DN], w[:, h * (DN + DV_MLA):h * (DN + DV_MLA) + DN],
                             NT_DIMS, preferred_element_type=F32) for h in range(H_MLA)], axis=0).astype(BF16)
        qr = jnp.concatenate([q[:, h * QK_PAD + DN:(h + 1) * QK_PAD] for h in range(H_MLA)], axis=0)
        reqs.append(dict(qa=qa, qr=qr, cache_c=cc_ref[r].astype(BF16), own_c=c_ref[r].astype(BF16)))
    for r, rq in enumerate(reqs):
        rq['s_cache'] = (lax.dot_general(rq['qa'], rq['cache_c'], NT_DIMS, preferred_element_type=F32)
                         + jnp.dot(rq['qr'][:, :DR], ckr_ref[r].astype(BF16), preferred_element_type=F32))
        rq['s_own'] = (lax.dot_general(rq['qa'], rq['own_c'], NT_DIMS, preferred_element_type=F32)
                       + lax.dot_general(rq['qr'], kr_ref[r].astype(BF16), NT_DIMS, preferred_element_type=F32))
    for rq in reqs:
        m = jnp.maximum(jnp.max(rq['s_cache'], axis=-1, keepdims=True), jnp.max(rq['s_own'], axis=-1, keepdims=True))
        p_cache = jnp.exp2(rq['s_cache'] - m)
        p_own = jnp.exp2(rq['s_own'] - m)
        rq['l'] = jnp.sum(p_cache, axis=-1, keepdims=True) + jnp.sum(p_own, axis=-1, keepdims=True)
        rq['p_cache'], rq['p_own'] = p_cache.astype(BF16), p_own.astype(BF16)
    for rq in reqs:
        rq['pc'] = (jnp.dot(rq['p_cache'], rq['cache_c'], preferred_element_type=F32)
                    + jnp.dot(rq['p_own'], rq['own_c'], preferred_element_type=F32)) / rq['l']
    for r, rq in enumerate(reqs):
        for h in range(H_MLA):
            lo = h * (DN + DV_MLA) + DN
            o_ref[r, :, h * DV_MLA:(h + 1) * DV_MLA] = _bdot(rq['pc'][h * t:(h + 1) * t],
                                                             w[:, lo:lo + DV_MLA]).astype(o_ref.dtype)


def mla_sample(q, c, kr, cache_c, cache_kr, w_ukv, group=MLA_SAMPLE_GROUP):
    b, t, _ = q.shape
    p = cache_c.shape[1]
    group = min(group, b)
    assert b % group == 0
    per_b = lambda rows, width: pl.BlockSpec((group, rows, width), lambda bi: (bi, 0, 0))
    return pl.pallas_call(
        _mla_sample_kernel,
        grid=(b // group,),
        in_specs=[per_b(t, H_MLA * QK_PAD), per_b(t, KV_LORA), per_b(t, LANES), per_b(p, KV_LORA), per_b(DR, p),
                  _const_spec(w_ukv.shape)],
        out_specs=per_b(t, H_MLA * DV_MLA),
        out_shape=jax.ShapeDtypeStruct((b, t, H_MLA * DV_MLA), BF16),
        compiler_params=_params("parallel"),
        name="mla_sample",
    )(q, c, kr, cache_c, cache_kr, w_ukv)


def _rope_tables(first, count, repeat=1):
    inv = ROPE_BASE ** (-np.arange(0, DR, 2, dtype=np.float64) / DR)
    ang = (first + np.arange(count, dtype=np.float64))[:, None] * inv[None, :]
    cos, sin = np.cos(ang), np.sin(ang)
    pad = np.zeros((count, LANES - DR))
    table = lambda parts: jnp.asarray(np.tile(np.concatenate(parts, axis=1), (repeat, 1)), F32)
    return table([cos, cos, pad]), table([-sin, sin, pad])


def _prepare_weights(ffn1_norm, ffn1_wg, ffn1_wu, ffn1_wd, mix_norm, w_in, conv_w, a_log, dt_bias, gdn_norm,
                     q_norm, kv_norm, w_uq, w_ukv, w_out, ffn2_norm, ffn2_wg, ffn2_wu, ffn2_wd, final_norm):
    w = {}
    w['g1'], w['gm'], w['g2'] = ffn1_norm[0][None], mix_norm[0][None], ffn2_norm[0][None]
    w['gf'] = final_norm[None]
    assert D_FF % FF_CHUNK == 0
    w['wg1'], w['wu1'], w['wd1'] = (m[0].astype(BF16) for m in (ffn1_wg, ffn1_wu, ffn1_wd))
    w['wg2'], w['wu2'], w['wd2'] = (m[0].astype(BF16) for m in (ffn2_wg, ffn2_wu, ffn2_wd))
    o_z = CONV_DIM
    o_a = o_z + H_GDN * GDN_DV
    o_b = o_a + H_GDN
    o_cq = o_b + H_GDN
    o_kr = o_cq + Q_LORA + KV_LORA
    wi = w_in[0]
    w['w_qkv'] = wi[:, :o_z].astype(BF16)
    w['w_z'] = wi[:, o_z:o_a].astype(BF16)
    tail_pad = jnp.zeros((D_MODEL, LANES - DR - 2 * H_GDN), wi.dtype)
    w['w_mla'] = jnp.concatenate([wi[:, o_cq:o_kr], wi[:, o_kr:o_kr + DR], wi[:, o_a:o_cq], tail_pad], axis=1).astype(BF16)
    w['conv_w'] = conv_w[0]
    lane_vec = lambda v: jnp.zeros((1, LANES), F32).at[0, AB_LANE:AB_LANE + H_GDN].set(v.astype(F32))
    w['alog'], w['dtb'] = lane_vec(a_log[0]), lane_vec(dt_bias[0])
    w['gn'] = gdn_norm[0][None]
    w['gq'], w['gkv'] = q_norm[0][None], kv_norm[0][None]
    uq = w_uq[0].reshape(Q_LORA, H_MLA, DN + DR)
    assert QK_PAD - DN - DR == DR
    uq = jnp.concatenate([uq, uq[:, :, DN:]], axis=-1)
    w['w_uq'] = uq.reshape(Q_LORA, H_MLA * QK_PAD).astype(BF16)
    w['w_ukv'] = w_ukv[0].astype(BF16)
    ukv = w['w_ukv'].reshape(KV_LORA, H_MLA, DN + DV_MLA)
    w['w_uk'] = ukv[:, :, :DN].reshape(KV_LORA, H_MLA * DN)
    w['w_uv_t'] = ukv[:, :, DN:].reshape(KV_LORA, H_MLA * DV_MLA).T
    w['w_out_g'] = w_out[0][:H_GDN * GDN_DV].astype(BF16)
    w['w_out_m'] = w_out[0][H_GDN * GDN_DV:].astype(BF16)
    return w


def kernel(x_prompt, x_sample, cache_mla_ckv, cache_mla_krope, state_gdn, state_conv, meta, ffn1_norm, ffn1_wg,
           ffn1_wu, ffn1_wd, mix_norm, w_in, conv_w, a_log, dt_bias, gdn_norm, q_norm, kv_norm, w_uq, w_ukv, w_out,
           ffn2_norm, ffn2_wg, ffn2_wu, ffn2_wd, final_norm):
    assert ffn1_wg.shape[0] == 1, "one layer: the meta rows are not carried past the mixer"
    bsz, s_len, _ = x_prompt.shape
    dbs, d_seq, _ = x_sample.shape
    past = cache_mla_ckv.shape[2]
    w = _prepare_weights(ffn1_norm, ffn1_wg, ffn1_wu, ffn1_wd, mix_norm, w_in, conv_w, a_log, dt_bias, gdn_norm,
                         q_norm, kv_norm, w_uq, w_ukv, w_out, ffn2_norm, ffn2_wg, ffn2_wu, ffn2_wd, final_norm)

    front = CHUNK - N_META
    _, qkv_m, _, mla_m = ffn_proj(meta.astype(F32), w)
    meta_qkv = jnp.pad(qkv_m, ((front, 0), (0, 0)))[None]
    x1_p, feat_p, z_p, mla_p, tail_p = ffn_proj(x_prompt.reshape(bsz * s_len, D_MODEL), w,
                                                 hist=meta_qkv[0, CHUNK - GDN_HIST:], stream_len=s_len)
    x1_s, qkv_s, z_s, mla_s = ffn_proj(x_sample.reshape(dbs * d_seq, D_MODEL), w)

    _, m_meta = gdn(meta_qkv, jnp.zeros((1, CHUNK, H_GDN * GDN_DV), F32), jnp.pad(mla_m, ((front, 0), (0, 0)))[None],
                    jnp.zeros((1, H_GDN, GDN_DK, GDN_DV), F32), jnp.zeros((1, GDN_HIST, CONV_DIM), F32), w,
                    pad_front=front)
    m0_p = jnp.broadcast_to(m_meta, (bsz,) + m_meta.shape[1:])
    gdn_p, m_p = gdn(feat_p.reshape(bsz, s_len, CONV_DIM), z_p.reshape(bsz, s_len, -1),
                     mla_p.reshape(bsz, s_len, MLA_IN), m0_p, jnp.zeros((bsz, GDN_HIST, CONV_DIM), F32), w,
                     rows=GDN_PROMPT_ROWS, prepared=True)
    conv_p = tail_p.reshape(bsz, -1, GDN_HIST, CONV_DIM)[:, -1, GDN_HIST - (CONV_W - 1):]
    qkv_s3 = qkv_s.reshape(dbs, d_seq, CONV_DIM)
    conv0_s = jnp.pad(state_conv[0].astype(F32), ((0, 0), (GDN_HIST - (CONV_W - 1), 0), (0, 0)))
    gdn_s, m_s = gdn(qkv_s3, z_s.reshape(dbs, d_seq, -1), mla_s.reshape(dbs, d_seq, MLA_IN),
                     state_gdn[0].astype(F32), conv0_s, w, bg=GDN_SAMPLE_GROUP)

    cos_m, sin_m = _rope_tables(0, N_META)
    cos_p, sin_p = _rope_tables(N_META, s_len)
    cos_s, sin_s = _rope_tables(past, d_seq, repeat=min(dbs, MLA_PREP_TILE // d_seq))
    _, c_m, kr_m, k_m, vt_m = mla_prep(mla_m, cos_m, sin_m, w, expand=True)
    q_p, c_p, kr_p, k_p, vt_p = mla_prep(mla_p, cos_p, sin_p, w, expand=True)
    q_s, c_s, kr_s = mla_prep(mla_s, cos_s, sin_s, w, expand=False)
    mla_o_p = mla_prompt(q_p.reshape(bsz, s_len, -1), k_p.reshape(bsz, s_len, -1), vt_p, k_m, vt_m)
    mla_o_s = mla_sample(q_s.reshape(dbs, d_seq, -1), c_s.reshape(dbs, d_seq, -1), kr_s.reshape(dbs, d_seq, -1),
                         cache_mla_ckv[0].astype(F32), jnp.swapaxes(cache_mla_krope[0].astype(F32), 1, 2), w['w_ukv'])

    y_p = out_ffn(x1_p, gdn_p.reshape(bsz * s_len, -1), mla_o_p.reshape(bsz * s_len, -1), w)
    y_s = out_ffn(x1_s, gdn_s.reshape(dbs * d_seq, -1), mla_o_s.reshape(dbs * d_seq, -1), w)

    krope_t = jnp.concatenate([jnp.broadcast_to(kr_m[None], (bsz, DR, N_META)),
                               kr_p.reshape(DR, bsz, s_len).transpose(1, 0, 2)], axis=2)
    return (y_p.reshape(bsz, s_len, D_MODEL), y_s.reshape(dbs, d_seq, D_MODEL),
            jnp.concatenate([jnp.broadcast_to(c_m[None], (bsz,) + c_m.shape), c_p.reshape(bsz, s_len, KV_LORA)],
                            axis=1)[None],
            jnp.swapaxes(krope_t, 1, 2)[None],
            m_p[None], conv_p[None],
            c_s.reshape(dbs, d_seq, KV_LORA)[None], kr_s[:, :DR].reshape(dbs, d_seq, DR)[None],
            m_s[None], qkv_s3[:, d_seq - (CONV_W - 1):][None])
```
